```python
import jax, jax.numpy as jnp
from jax import lax
import numpy as np

D_MODEL = 2048
BATCH = 4
SEQ = 2048
DEPTH = 2

N_META = 16
HEAD_DIM = 128
ML_HEADS = 6
SB_HEADS = 6
POOL_WINDOWS = (2, 4, 8, 16)
POOL_GROUPS = 4
POOL_CH = 128
D_ML = ML_HEADS * HEAD_DIM
D_SB = SB_HEADS * HEAD_DIM
D_POOL = POOL_GROUPS * POOL_CH
D_MIX = D_ML + D_SB + D_POOL
ML_CHUNK = 64
SB_BLOCK = 128
CONV_W = 4
IN_SPLITS = (2 * D_ML, D_ML, D_ML, ML_HEADS, ML_HEADS, D_SB, D_SB, D_SB, D_POOL)
N_IN = 2 * D_ML + D_ML + D_ML + ML_HEADS + ML_HEADS + 3 * D_SB + D_POOL
N_EXPERTS = 64
TOP_K = 8
D_EXPERT = 512
D_SHARED = 512
ROUTED_SCALE = 2.5
EXPERT_BLOCK = 128
DN_ALPHA = (2 * DEPTH) ** 0.25
DN_BETA = (8 * DEPTH) ** -0.25
LN_EPS = 1e-5
RMS_EPS = 1e-6

kernel_name = "hybrid_mlstm_stickbreak_pool_moe_deepnorm"


def layer_norm(x, g, b):
    xf = x.astype(jnp.float32)
    mu = xf.mean(-1, keepdims=True)
    var = jnp.square(xf - mu).mean(-1, keepdims=True)
    return ((xf - mu) * lax.rsqrt(var + LN_EPS) * g + b).astype(x.dtype)


def rms_heads(h, g):
    H, d = h.shape[-2], h.shape[-1]
    return h * lax.rsqrt(jnp.square(h).mean(-1, keepdims=True) + RMS_EPS) * g.reshape(H, d).astype(jnp.float32)


def causal_conv(x, w):
    K, L = w.shape[0], x.shape[1]
    xp = jnp.pad(x, ((0, 0), (K - 1, 0), (0, 0)))
    return sum(xp[:, i:i + L] * w[i] for i in range(K))


def mlstm_chunk(state, inp):
    C, n, m = state
    q, k, v, li, lf = inp
    Lc = q.shape[2]
    b = jnp.cumsum(lf, axis=-1)
    causal = jnp.tril(jnp.ones((Lc, Lc), bool))
    D = jnp.where(causal, b[..., :, None] - b[..., None, :] + li[..., None, :], -jnp.inf)
    inter = b + m[..., None]
    m_t = jnp.maximum(inter, D.max(-1))
    w_inter = jnp.exp(inter - m_t)
    ws = jnp.exp(D - m_t[..., None]) * jnp.einsum("bhtk,bhsk->bhts", q, k)
    num = w_inter[..., None] * jnp.einsum("bhvk,bhtk->bhtv", C, q) + jnp.einsum("bhts,bhsv->bhtv", ws, v)
    den = w_inter * jnp.einsum("bhk,bhtk->bht", n, q) + ws.sum(-1)
    h = num / jnp.maximum(jnp.abs(den), jnp.exp(-m_t))[..., None]
    bL = b[..., -1]
    g = bL[..., None] - b + li
    m_new = jnp.maximum(bL + m, g.max(-1))
    wg = jnp.exp(g - m_new[..., None])
    decay = jnp.exp(bL + m - m_new)
    C_new = decay[..., None, None] * C + jnp.einsum("bhsv,bhsk->bhvk", v * wg[..., None], k)
    n_new = decay[..., None] * n + jnp.einsum("bhs,bhsk->bhk", wg, k)
    return (C_new, n_new, m_new), h


def mlstm(q, k, v, log_i, log_f):
    B, L, H, d = q.shape
    tr = lambda t: jnp.swapaxes(t.astype(jnp.float32), 1, 2)
    q = tr(q) * d ** -0.5
    k, v, li, lf = tr(k), tr(v), tr(log_i), tr(log_f)
    state = (jnp.zeros((B, H, d, d), jnp.float32), jnp.zeros((B, H, d), jnp.float32), jnp.zeros((B, H), jnp.float32))
    state, h_meta = mlstm_chunk(state, tuple(t[:, :, :N_META] for t in (q, k, v, li, lf)))
    n_chunks = (L - N_META) // ML_CHUNK

    def to_chunks(t):
        t = t[:, :, N_META:]
        return jnp.moveaxis(t.reshape((B, H, n_chunks, ML_CHUNK) + t.shape[3:]), 2, 0)

    _, h_rest = lax.scan(mlstm_chunk, state, tuple(to_chunks(t) for t in (q, k, v, li, lf)))
    h_rest = jnp.moveaxis(h_rest, 0, 2).reshape(B, H, L - N_META, d)
    return jnp.swapaxes(jnp.concatenate([h_meta, h_rest], axis=2), 1, 2)


def stick_breaking(q, k, v):
    B, L, H, d = q.shape
    tr = lambda t: jnp.swapaxes(t.astype(jnp.float32), 1, 2)
    q, k, v = tr(q) * d ** -0.5, tr(k), tr(v)
    bounds = [(0, N_META)] + [(s, min(s + SB_BLOCK, L)) for s in range(N_META, L, SB_BLOCK)]
    outs = []
    for t0, t1 in bounds:
        z = jnp.einsum("bhqd,bhkd->bhqk", q[:, :, t0:t1], k[:, :, :t1])
        causal = jnp.arange(t1)[None, :] < jnp.arange(t0, t1)[:, None]
        log_keep = jnp.where(causal, jax.nn.log_sigmoid(-z), 0.0)
        after = lax.cumsum(log_keep, axis=3, reverse=True) - log_keep
        a = jnp.where(causal, jnp.exp(jax.nn.log_sigmoid(z) + after), 0.0)
        outs.append(jnp.einsum("bhqk,bhkd->bhqd", a, v[:, :, :t1]))
    return jnp.swapaxes(jnp.concatenate(outs, axis=2), 1, 2)


def pool_mixer(u, pool_w, pool_scale):
    B, L, _ = u.shape
    uf = u.astype(jnp.float32)
    cs = jnp.concatenate([jnp.zeros((B, 1, D_POOL), jnp.float32), jnp.cumsum(uf, axis=1)], axis=1)
    csg = cs.reshape(B, L + 1, POOL_GROUPS, POOL_CH)
    ug = uf.reshape(B, L, POOL_GROUPS, POOL_CH)
    pos = jnp.arange(L)
    outs = []
    for g, w in enumerate(POOL_WINDOWS):
        lo = jnp.maximum(pos + 1 - w, 0)
        cnt = (pos + 1 - lo).astype(jnp.float32)
        mean = (csg[:, 1:, g] - csg[:, lo, g]) / cnt[None, :, None]
        outs.append(mean - ug[:, :, g])
    p = jnp.stack(outs, axis=2).astype(u.dtype)
    y = jnp.einsum("blgc,gcd->blgd", p, pool_w).reshape(B, L, D_POOL)
    return y * pool_scale


def hybrid_mixer(u, w_in, b_igate, b_fgate, conv_qk, ml_norm_g, sb_norm_g, pool_w, pool_scale, w_out):
    B, L, _ = u.shape
    proj = u @ w_in
    idx = [int(s) for s in np.cumsum(IN_SPLITS)[:-1]]
    ml_qk, ml_v, ml_o, ml_i, ml_f, sb_q, sb_k, sb_v, pool_in = jnp.split(proj, idx, axis=-1)
    heads = lambda t, H: t.reshape(B, L, H, HEAD_DIM)
    ml_q, ml_k = jnp.split(jax.nn.silu(causal_conv(ml_qk, conv_qk)), 2, axis=-1)
    log_i = (ml_i + b_igate).astype(jnp.float32)
    log_f = jax.nn.log_sigmoid((ml_f + b_fgate).astype(jnp.float32))
    h_ml = mlstm(heads(ml_q, ML_HEADS), heads(ml_k, ML_HEADS), heads(ml_v, ML_HEADS), log_i, log_f)
    h_ml = rms_heads(h_ml, ml_norm_g) * jax.nn.sigmoid(heads(ml_o, ML_HEADS).astype(jnp.float32))
    h_sb = rms_heads(stick_breaking(heads(sb_q, SB_HEADS), heads(sb_k, SB_HEADS), heads(sb_v, SB_HEADS)), sb_norm_g)
    h_pool = pool_mixer(pool_in, pool_w, pool_scale)
    cat = jnp.concatenate([h_ml.reshape(B, L, D_ML).astype(u.dtype), h_sb.reshape(B, L, D_SB).astype(u.dtype), h_pool], axis=-1)
    return cat @ w_out


def swiglu(x, w13, w2):
    gate, up = jnp.split(x @ w13, 2, axis=-1)
    return (jax.nn.silu(gate) * up) @ w2


def routed_experts(xt, idx, gate, w13, w2):
    T, D = xt.shape
    A = T * TOP_K
    e_flat = idx.reshape(-1)
    t_flat = jnp.repeat(jnp.arange(T, dtype=jnp.int32), TOP_K)
    w_flat = gate.reshape(-1)
    order = jnp.argsort(e_flat)
    e_s, t_s, w_s = e_flat[order], t_flat[order], w_flat[order]
    counts = jnp.bincount(e_flat, length=N_EXPERTS)
    padded = (counts + EXPERT_BLOCK - 1) // EXPERT_BLOCK * EXPERT_BLOCK
    off = jnp.cumsum(counts) - counts
    ends = jnp.cumsum(padded)
    poff = ends - padded
    pos = poff[e_s] + jnp.arange(A) - off[e_s]
    n_blocks = -(-A // EXPERT_BLOCK) + N_EXPERTS
    P = n_blocks * EXPERT_BLOCK
    buf_tok = jnp.full((P,), T, jnp.int32).at[pos].set(t_s)
    buf_w = jnp.zeros((P,), jnp.float32).at[pos].set(w_s)
    block_e = jnp.minimum(jnp.searchsorted(ends, jnp.arange(n_blocks) * EXPERT_BLOCK, side="right"), N_EXPERTS - 1)
    x_pad = jnp.concatenate([xt, jnp.zeros((1, D), xt.dtype)], axis=0)

    def run(args):
        tok, e = args
        return swiglu(x_pad[tok], w13[e], w2[e])

    y = lax.map(run, (buf_tok.reshape(n_blocks, EXPERT_BLOCK), block_e)).reshape(P, D)
    y = y * buf_w[:, None].astype(y.dtype)
    return jnp.zeros((T + 1, D), xt.dtype).at[buf_tok].add(y)[:T]


def moe_ffn(x, router_w, router_bias, exp_w13, exp_w2, sh_w13, sh_w2):
    B, L, D = x.shape
    xt = x.reshape(B * L, D)
    scores = jax.nn.sigmoid((xt @ router_w).astype(jnp.float32))
    _, idx = lax.top_k(scores + router_bias.astype(jnp.float32), TOP_K)
    gate = jnp.take_along_axis(scores, idx, axis=-1)
    gate = gate / gate.sum(-1, keepdims=True) * ROUTED_SCALE
    out = routed_experts(xt, idx, gate, exp_w13, exp_w2) + swiglu(xt, sh_w13, sh_w2)
    return out.reshape(B, L, D)


def setup_inputs(seed: int = 0) -> dict:
    key = jax.random.key(seed)
    ks = jax.random.split(key, 24)
    f32 = jnp.float32
    nrm = lambda k, shape, s: jax.random.normal(k, shape, f32) * s
    Lr = DEPTH
    return {
        "x": nrm(ks[0], (BATCH, SEQ, D_MODEL), 1.0),
        "meta": nrm(ks[1], (N_META, D_MODEL), 1.0),
        "ln_in_g": 1.0 + nrm(ks[2], (D_MODEL,), 0.02),
        "ln_in_b": nrm(ks[3], (D_MODEL,), 0.02),
        "w_in": nrm(ks[4], (Lr, D_MODEL, N_IN), D_MODEL ** -0.5),
        "b_igate": nrm(ks[5], (Lr, ML_HEADS), 0.1),
        "b_fgate": jnp.linspace(3.0, 6.0, ML_HEADS, dtype=f32) + nrm(ks[6], (Lr, ML_HEADS), 0.1),
        "conv_qk": nrm(ks[7], (Lr, CONV_W, 2 * D_ML), CONV_W ** -0.5),
        "ml_norm_g": 1.0 + nrm(ks[8], (Lr, D_ML), 0.02),
        "sb_norm_g": 1.0 + nrm(ks[9], (Lr, D_SB), 0.02),
        "pool_w": nrm(ks[10], (Lr, POOL_GROUPS, POOL_CH, POOL_CH), POOL_CH ** -0.5),
        "pool_scale": 1.0 + nrm(ks[11], (Lr, D_POOL), 0.02),
        "w_out": nrm(ks[12], (Lr, D_MIX, D_MODEL), D_MIX ** -0.5 * DN_BETA),
        "ln1_g": 1.0 + nrm(ks[13], (Lr, D_MODEL), 0.02),
        "ln1_b": nrm(ks[14], (Lr, D_MODEL), 0.02),
        "router_w": nrm(ks[15], (Lr, D_MODEL, N_EXPERTS), D_MODEL ** -0.5),
        "router_bias": nrm(ks[16], (Lr, N_EXPERTS), 0.01),
        "exp_w13": nrm(ks[17], (Lr, N_EXPERTS, D_MODEL, 2 * D_EXPERT), D_MODEL ** -0.5),
        "exp_w2": nrm(ks[18], (Lr, N_EXPERTS, D_EXPERT, D_MODEL), D_EXPERT ** -0.5 * DN_BETA),
        "sh_w13": nrm(ks[19], (Lr, D_MODEL, 2 * D_SHARED), D_MODEL ** -0.5),
        "sh_w2": nrm(ks[20], (Lr, D_SHARED, D_MODEL), D_SHARED ** -0.5 * DN_BETA),
        "ln2_g": 1.0 + nrm(ks[21], (Lr, D_MODEL), 0.02),
        "ln2_b": nrm(ks[22], (Lr, D_MODEL), 0.02),
    }


def reference(x, meta, ln_in_g, ln_in_b, w_in, b_igate, b_fgate, conv_qk, ml_norm_g, sb_norm_g, pool_w, pool_scale, w_out, ln1_g, ln1_b, router_w, router_bias, exp_w13, exp_w2, sh_w13, sh_w2, ln2_g, ln2_b):
    B = x.shape[0]
    h = jnp.concatenate([jnp.broadcast_to(meta[None].astype(x.dtype), (B, N_META, D_MODEL)), x], axis=1)
    h = layer_norm(h, ln_in_g, ln_in_b)
    for l in range(DEPTH):
        mix = hybrid_mixer(h, w_in[l], b_igate[l], b_fgate[l], conv_qk[l], ml_norm_g[l], sb_norm_g[l], pool_w[l], pool_scale[l], w_out[l])
        h = layer_norm(DN_ALPHA * h + mix, ln1_g[l], ln1_b[l])
        ffn = moe_ffn(h, router_w[l], router_bias[l], exp_w13[l], exp_w2[l], sh_w13[l], sh_w2[l])
        h = layer_norm(DN_ALPHA * h + ffn, ln2_g[l], ln2_b[l])
    return h[:, N_META:]
```

```python
import functools

import jax
import jax.numpy as jnp
from jax import lax
from jax.experimental import pallas as pl
from jax.experimental.pallas import tpu as pltpu

N_META = 16
HEAD_DIM = 128
ML_HEADS = 6
SB_HEADS = 6
POOL_WINDOWS = (2, 4, 8, 16)
POOL_GROUPS = 4
D_ML = ML_HEADS * HEAD_DIM
D_SB = SB_HEADS * HEAD_DIM
D_POOL = POOL_GROUPS * HEAD_DIM
CONV_W = 4
N_EXPERTS = 64
TOP_K = 8
ROUTED_SCALE = 2.5
LN_EPS = 1e-5
RMS_EPS = 1e-6

BLK = 128
PAD = (-N_META) % BLK
ML_HPS = 2
ML_GROUPS = ML_HEADS // ML_HPS
ROW_TILE = 512
EXPERT_TILE = 256
COMBINE_TILE = 128
NEG = -1e30
VMEM_LIMIT = 56 * 1024 * 1024

C_QK = 0
C_V = 2 * D_ML
C_O = C_V + D_ML
C_SBQ = C_O + D_ML
C_SBK = C_SBQ + D_SB
C_SBV = C_SBK + D_SB
C_POOL = C_SBV + D_SB
N_MAIN = C_POOL + D_POOL

F32 = jnp.float32
BF16 = jnp.bfloat16


def _cparams(sem, vmem=VMEM_LIMIT):
    return pltpu.CompilerParams(dimension_semantics=sem, vmem_limit_bytes=vmem)


def _layer_norm(x, g, b):
    mu = jnp.mean(x, axis=-1, keepdims=True)
    xc = x - mu
    var = jnp.mean(xc * xc, axis=-1, keepdims=True)
    return xc * lax.rsqrt(var + LN_EPS) * g + b


def _sigmoid(x):
    return 1.0 / (1.0 + jnp.exp(-x))


def _softplus(x):
    return jnp.maximum(x, 0.0) + jnp.log(1.0 + jnp.exp(-jnp.abs(x)))


def _split_dot(a, b, split_lhs):
    x = a if split_lhs else b
    out = None
    for _ in range(3):
        part = x.astype(BF16)
        x = x - part.astype(F32)
        term = (jnp.dot(part, b, preferred_element_type=F32) if split_lhs
                else jnp.dot(a, part, preferred_element_type=F32))
        out = term if out is None else out + term
    return out


def _ln_kernel(x_ref, g_ref, b_ref, o_ref, ob_ref):
    y = _layer_norm(x_ref[...], g_ref[...], b_ref[...])
    o_ref[...] = y
    ob_ref[...] = y.astype(BF16)


def _ln_rows(x, g, b):
    T, D = x.shape
    row = pl.BlockSpec((ROW_TILE, D), lambda i: (i, 0))
    vec = pl.BlockSpec((1, D), lambda i: (0, 0))
    return pl.pallas_call(
        _ln_kernel,
        grid=(T // ROW_TILE,),
        in_specs=[row, vec, vec],
        out_specs=[row, row],
        out_shape=[jax.ShapeDtypeStruct((T, D), F32), jax.ShapeDtypeStruct((T, D), BF16)],
        compiler_params=_cparams(("parallel",)),
        name="ln_in",
    )(x, g.reshape(1, D), b.reshape(1, D))


def _mm_kernel(a_ref, w_ref, o_ref):
    o_ref[...] = jnp.dot(a_ref[...], w_ref[...], preferred_element_type=F32).astype(o_ref.dtype)


def _in_proj(a, w):
    M, K = a.shape
    N = w.shape[1]
    tn = N // 2
    return pl.pallas_call(
        _mm_kernel,
        grid=(N // tn, M // ROW_TILE),
        in_specs=[pl.BlockSpec((ROW_TILE, K), lambda j, i: (i, 0)),
                  pl.BlockSpec((K, tn), lambda j, i: (0, j))],
        out_specs=pl.BlockSpec((ROW_TILE, tn), lambda j, i: (i, j)),
        out_shape=jax.ShapeDtypeStruct((M, N), BF16),
        compiler_params=_cparams(("parallel", "parallel")),
        name="in_proj",
    )(a, w)


def _gates_kernel(hb_ref, wg_ref, bias_ref, o_ref, *, tm):
    j = pl.program_id(1)
    g = jnp.dot(hb_ref[0], wg_ref[...], preferred_element_type=F32) + bias_ref[...]
    col = jnp.bitwise_and(lax.broadcasted_iota(jnp.int32, g.shape, 1), BLK - 1)
    row = lax.broadcasted_iota(jnp.int32, g.shape, 0) + j * tm
    is_f = jnp.logical_and(col >= ML_HPS, col < 2 * ML_HPS)
    val = jnp.where(is_f, -_softplus(-g), g)
    pad_val = jnp.where(is_f, 0.0, NEG)
    o_ref[0] = jnp.where(row < PAD, pad_val, val)


def _gates(hb3, wg, bias):
    B, Lp, D = hb3.shape
    tm = Lp // 4
    Ng = wg.shape[1]
    return pl.pallas_call(
        functools.partial(_gates_kernel, tm=tm),
        grid=(B, Lp // tm),
        in_specs=[pl.BlockSpec((1, tm, D), lambda b, j: (b, j, 0)),
                  pl.BlockSpec((D, Ng), lambda b, j: (0, 0)),
                  pl.BlockSpec((1, Ng), lambda b, j: (0, 0))],
        out_specs=pl.BlockSpec((1, tm, Ng), lambda b, j: (b, j, 0)),
        out_shape=jax.ShapeDtypeStruct((B, Lp, Ng), F32),
        compiler_params=_cparams(("parallel", "parallel")),
        name="ml_gates",
    )(hb3, wg, bias)


def _conv_kernel(x_ref, w_ref, o_ref):
    c = pl.program_id(1)
    x = x_ref[0].astype(F32)
    row = lax.broadcasted_iota(jnp.int32, x.shape, 0)
    x = jnp.where(row < PAD, 0.0, x)
    w = w_ref[...]
    y = x * w[CONV_W - 1:CONV_W]
    for s in range(1, CONV_W):
        y = y + pltpu.roll(x, s, axis=0) * w[CONV_W - 1 - s:CONV_W - s]
    y = y * _sigmoid(y)
    scale = jnp.where(c < ML_HEADS, HEAD_DIM ** -0.5, 1.0).astype(F32)
    o_ref[0] = (y * scale).astype(BF16)


def _conv_silu(proj3, conv_w):
    B, Lp, _ = proj3.shape
    nc = 2 * D_ML // BLK
    return pl.pallas_call(
        _conv_kernel,
        grid=(B, nc),
        in_specs=[pl.BlockSpec((1, Lp, BLK), lambda b, c: (b, 0, c)),
                  pl.BlockSpec((CONV_W, BLK), lambda b, c: (0, c))],
        out_specs=pl.BlockSpec((1, Lp, BLK), lambda b, c: (b, 0, c)),
        out_shape=jax.ShapeDtypeStruct((B, Lp, 2 * D_ML), BF16),
        compiler_params=_cparams(("parallel", "parallel")),
        name="conv_silu",
    )(proj3, conv_w)


def _mlstm_kernel(q_ref, k_ref, v_ref, o_ref, gc_ref, gr_ref, ng_ref, out_ref, st_ref, m_ref):
    n_chunks = q_ref.shape[1] // BLK
    st_ref[...] = jnp.zeros_like(st_ref)
    m_ref[...] = jnp.zeros_like(m_ref)

    ri = lax.broadcasted_iota(jnp.int32, (BLK, BLK), 0)
    ci = lax.broadcasted_iota(jnp.int32, (BLK, BLK), 1)
    causal = ci <= ri
    lower_incl = jnp.where(causal, 1.0, 0.0).astype(BF16)
    upper_incl = jnp.where(ri <= ci, 1.0, 0.0).astype(BF16)
    ones_col = jnp.where(ci == 0, 1.0, 0.0).astype(BF16)

    def chunk(c, carry):
        r0 = pl.multiple_of(c * BLK, BLK)
        gcol = gc_ref[0, pl.ds(r0, BLK), :]
        grow = gr_ref[0, 0, :, pl.ds(r0, BLK)]
        bcol = _split_dot(lower_incl, gcol, split_lhs=False)
        brow = _split_dot(grow, upper_incl, split_lhs=True)
        for hh in range(ML_HPS):
            cs = slice(hh * HEAD_DIM, (hh + 1) * HEAD_DIM)
            q = q_ref[0, pl.ds(r0, BLK), cs]
            k = k_ref[0, pl.ds(r0, BLK), cs]
            v = v_ref[0, pl.ds(r0, BLK), cs]
            li_c = gcol[:, hh:hh + 1]
            b_c = bcol[:, ML_HPS + hh:ML_HPS + hh + 1]
            li_r = grow[hh:hh + 1, :]
            b_r = brow[ML_HPS + hh:ML_HPS + hh + 1, :]
            m = m_ref[hh, 0:1, 0:1]
            st = st_ref[hh]

            inter = b_c + m
            dmat = jnp.where(causal, b_c - b_r + li_r, NEG)
            m_t = jnp.maximum(inter, jnp.max(dmat, axis=-1, keepdims=True))
            w_inter = jnp.exp(inter - m_t)
            s_qk = lax.dot_general(q, k, (((1,), (1,)), ((), ())), preferred_element_type=F32)
            ws = jnp.exp(dmat - m_t) * s_qk
            v_aug = jnp.concatenate([v, ones_col], axis=1)
            nd = (w_inter * jnp.dot(q, st.astype(BF16), preferred_element_type=F32)
                  + jnp.dot(ws.astype(BF16), v_aug, preferred_element_type=F32))
            num = nd[:, :HEAD_DIM]
            den = nd[:, HEAD_DIM:HEAD_DIM + 1]
            h = num / jnp.maximum(jnp.abs(den), jnp.exp(-m_t))

            og = o_ref[0, pl.ds(r0, BLK), cs].astype(F32)
            hn = h * lax.rsqrt(jnp.mean(h * h, axis=-1, keepdims=True) + RMS_EPS)
            hn = hn * ng_ref[:, cs] * _sigmoid(og)
            out_ref[0, pl.ds(r0, BLK), cs] = hn.astype(BF16)

            b_last = b_r[:, BLK - 1:BLK]
            g_r = b_last - b_r + li_r
            m_new = jnp.maximum(b_last + m, jnp.max(g_r, axis=-1, keepdims=True))
            wg = jnp.exp(b_last - b_c + li_c - m_new)
            decay = jnp.exp(b_last + m - m_new)
            wv = (wg * v_aug.astype(F32)).astype(BF16)
            kv = lax.dot_general(k, wv, (((0,), (0,)), ((), ())), preferred_element_type=F32)
            st_ref[hh] = decay * st + kv
            m_ref[hh] = jnp.broadcast_to(m_new, m_ref.shape[1:])
        return carry

    lax.fori_loop(0, n_chunks, chunk, 0)


def _mlstm(qk3, proj3, gcol, grow, norm_g):
    B, Lp, _ = qk3.shape
    W = ML_HPS * HEAD_DIM
    nq = D_ML // W
    seq = lambda off: pl.BlockSpec((1, Lp, W), lambda b, g: (b, 0, off + g))
    return pl.pallas_call(
        _mlstm_kernel,
        grid=(B, ML_GROUPS),
        in_specs=[seq(0), seq(nq),
                  pl.BlockSpec((1, Lp, W), lambda b, g: (b, 0, C_V // W + g)),
                  pl.BlockSpec((1, Lp, W), lambda b, g: (b, 0, C_O // W + g)),
                  pl.BlockSpec((1, Lp, BLK), lambda b, g: (b, 0, g)),
                  pl.BlockSpec((1, 1, 8, Lp), lambda b, g: (b, g, 0, 0)),
                  pl.BlockSpec((1, W), lambda b, g: (0, g))],
        out_specs=pl.BlockSpec((1, Lp, W), lambda b, g: (b, 0, g)),
        out_shape=jax.ShapeDtypeStruct((B, Lp, D_ML), BF16),
        scratch_shapes=[pltpu.VMEM((ML_HPS, HEAD_DIM, 2 * HEAD_DIM), F32),
                        pltpu.VMEM((ML_HPS, 8, BLK), F32)],
        compiler_params=_cparams(("parallel", "parallel")),
        name="mlstm",
    )(qk3, qk3, proj3, proj3, gcol, grow, norm_g.reshape(1, D_ML))


def _sb_kernel(q_ref, k_ref, v_ref, g_ref, o_ref):
    i = pl.program_id(2)
    q = q_ref[0]
    ri = lax.broadcasted_iota(jnp.int32, (BLK, BLK), 0)
    ci = lax.broadcasted_iota(jnp.int32, (BLK, BLK), 1)
    upper_strict = jnp.where(ri > ci, 1.0, 0.0).astype(BF16)
    scale = HEAD_DIM ** -0.5

    def step(t, state):
        carry, acc = state
        j = i - t
        r0 = pl.multiple_of(j * BLK, BLK)
        k = k_ref[0, pl.ds(r0, BLK), :]
        v = v_ref[0, pl.ds(r0, BLK), :]
        z = lax.dot_general(q, k, (((1,), (1,)), ((), ())), preferred_element_type=F32) * scale
        sp = _softplus(z)
        colp = ci + j * BLK
        valid = jnp.logical_and(colp < ri + i * BLK, colp >= PAD)
        lk = jnp.where(valid, -sp, 0.0)
        hi = lk.astype(BF16)
        lo = (lk - hi.astype(F32)).astype(BF16)
        after = (jnp.dot(hi, upper_strict, preferred_element_type=F32)
                 + jnp.dot(lo, upper_strict, preferred_element_type=F32))
        a = jnp.where(valid, jnp.exp(z - sp + after + carry), 0.0)
        acc = acc + jnp.dot(a.astype(BF16), v, preferred_element_type=F32)
        carry = carry + jnp.sum(lk, axis=-1, keepdims=True)
        return carry, acc

    init = (jnp.zeros((BLK, 1), F32), jnp.zeros((BLK, HEAD_DIM), F32))
    _, acc = lax.fori_loop(0, i + 1, step, init)
    hn = acc * lax.rsqrt(jnp.mean(acc * acc, axis=-1, keepdims=True) + RMS_EPS) * g_ref[...]
    o_ref[0] = hn.astype(BF16)


def _stick_breaking(proj3, norm_g):
    B, Lp, _ = proj3.shape
    nq = Lp // BLK
    return pl.pallas_call(
        _sb_kernel,
        grid=(B, SB_HEADS, nq),
        in_specs=[pl.BlockSpec((1, BLK, HEAD_DIM), lambda b, h, i: (b, i, C_SBQ // HEAD_DIM + h)),
                  pl.BlockSpec((1, Lp, HEAD_DIM), lambda b, h, i: (b, 0, C_SBK // HEAD_DIM + h)),
                  pl.BlockSpec((1, Lp, HEAD_DIM), lambda b, h, i: (b, 0, C_SBV // HEAD_DIM + h)),
                  pl.BlockSpec((1, HEAD_DIM), lambda b, h, i: (0, h))],
        out_specs=pl.BlockSpec((1, BLK, HEAD_DIM), lambda b, h, i: (b, i, h)),
        out_shape=jax.ShapeDtypeStruct((B, Lp, D_SB), BF16),
        compiler_params=_cparams(("parallel", "parallel", "parallel")),
        name="stick_breaking",
    )(proj3, proj3, proj3, norm_g.reshape(1, D_SB))


def _pool_kernel(x_ref, w_ref, s_ref, o_ref):
    g = pl.program_id(1)
    x = x_ref[0].astype(F32)
    row = lax.broadcasted_iota(jnp.int32, x.shape, 0)
    x = jnp.where(row < PAD, 0.0, x)
    pos1 = (row - PAD + 1).astype(F32)
    sums = x
    mean = jnp.zeros_like(x)
    shift = 1
    for gi, win in enumerate(POOL_WINDOWS):
        while shift < win:
            sums = sums + pltpu.roll(sums, shift, axis=0)
            shift *= 2
        cnt = jnp.maximum(jnp.minimum(pos1, float(win)), 1.0)
        mean = jnp.where(g == gi, sums / cnt, mean)
    p = (mean - x).astype(BF16)
    y = jnp.dot(p, w_ref[0].astype(BF16), preferred_element_type=F32) * s_ref[...]
    o_ref[0] = y.astype(BF16)


def _pool(proj3, pool_w, pool_scale):
    B, Lp, _ = proj3.shape
    return pl.pallas_call(
        _pool_kernel,
        grid=(B, POOL_GROUPS),
        in_specs=[pl.BlockSpec((1, Lp, HEAD_DIM), lambda b, g: (b, 0, C_POOL // HEAD_DIM + g)),
                  pl.BlockSpec((1, HEAD_DIM, HEAD_DIM), lambda b, g: (g, 0, 0)),
                  pl.BlockSpec((1, HEAD_DIM), lambda b, g: (0, g))],
        out_specs=pl.BlockSpec((1, Lp, HEAD_DIM), lambda b, g: (b, 0, g)),
        out_shape=jax.ShapeDtypeStruct((B, Lp, D_POOL), BF16),
        compiler_params=_cparams(("parallel", "parallel")),
        name="pool_mixer",
    )(proj3, pool_w, pool_scale.reshape(1, D_POOL))


def _outproj_kernel(a1_ref, a2_ref, a3_ref, w_ref, h_ref, g_ref, b_ref, o_ref, *, alpha):
    acc = jnp.dot(a1_ref[...], w_ref[0:D_ML, :], preferred_element_type=F32)
    acc = acc + jnp.dot(a2_ref[...], w_ref[D_ML:D_ML + D_SB, :], preferred_element_type=F32)
    acc = acc + jnp.dot(a3_ref[...], w_ref[D_ML + D_SB:, :], preferred_element_type=F32)
    o_ref[...] = _layer_norm(alpha * h_ref[...] + acc, g_ref[...], b_ref[...])


def _outproj_ln(a_ml, a_sb, a_pool, w, h, g, b, alpha):
    T, D = h.shape
    row = lambda n: pl.BlockSpec((ROW_TILE, n), lambda i: (i, 0))
    vec = pl.BlockSpec((1, D), lambda i: (0, 0))
    return pl.pallas_call(
        functools.partial(_outproj_kernel, alpha=alpha),
        grid=(T // ROW_TILE,),
        in_specs=[row(D_ML), row(D_SB), row(D_POOL),
                  pl.BlockSpec(w.shape, lambda i: (0, 0)), row(D), vec, vec],
        out_specs=row(D),
        out_shape=jax.ShapeDtypeStruct((T, D), F32),
        compiler_params=_cparams(("parallel",)),
        name="out_proj_ln",
    )(a_ml, a_sb, a_pool, w, h, g.reshape(1, D), b.reshape(1, D))


def _router_kernel(h_ref, rw_ref, rb_ref, idx_ref, gate_ref, rank_ref, cnt_ref, run_ref, *, tm):
    first = jnp.logical_and(pl.program_id(0) == 0, pl.program_id(1) == 0)

    @pl.when(first)
    def _():
        run_ref[...] = jnp.zeros_like(run_ref)

    x = h_ref[0]
    logits = jnp.dot(x, rw_ref[...], preferred_element_type=F32, precision=lax.Precision.HIGHEST)
    scores = _sigmoid(logits)
    work = scores + rb_ref[...]
    E = scores.shape[1]
    lane = lax.broadcasted_iota(jnp.int32, (tm, E), 1).astype(F32)
    row = lax.broadcasted_iota(jnp.int32, (tm, 1), 0) + pl.program_id(1) * tm
    real = (row >= PAD).astype(F32)

    picks = []
    gates = []
    member = jnp.zeros((tm, E), F32)
    for _ in range(TOP_K):
        mx = jnp.max(work, axis=-1, keepdims=True)
        pick = jnp.min(jnp.where(work == mx, lane, float(E)), axis=-1, keepdims=True)
        onehot = lane == pick
        gates.append(jnp.sum(jnp.where(onehot, scores, 0.0), axis=-1, keepdims=True))
        picks.append(pick)
        member = member + jnp.where(onehot, real, 0.0)
        work = jnp.where(onehot, -jnp.inf, work)
    gsum = gates[0]
    for gk in gates[1:]:
        gsum = gsum + gk

    ri = lax.broadcasted_iota(jnp.int32, (tm, tm), 0)
    ci = lax.broadcasted_iota(jnp.int32, (tm, tm), 1)
    lower_strict = jnp.where(ci < ri, 1.0, 0.0).astype(BF16)
    before = jnp.dot(lower_strict, member.astype(BF16), preferred_element_type=F32) + run_ref[...]

    lane_o = lax.broadcasted_iota(jnp.int32, (tm, BLK), 1)
    idx_o = jnp.zeros((tm, BLK), F32)
    gate_o = jnp.zeros((tm, BLK), F32)
    rank_o = jnp.zeros((tm, BLK), F32)
    for kk in range(TOP_K):
        onehot = lane == picks[kk]
        rk = jnp.sum(jnp.where(onehot, before, 0.0), axis=-1, keepdims=True)
        sel = lane_o == kk
        idx_o = jnp.where(sel, picks[kk], idx_o)
        gate_o = jnp.where(sel, gates[kk] / gsum * ROUTED_SCALE, gate_o)
        rank_o = jnp.where(sel, rk, rank_o)
    idx_ref[0] = idx_o.astype(jnp.int32)
    gate_ref[0] = gate_o
    rank_ref[0] = rank_o.astype(jnp.int32)
    run_ref[...] = run_ref[...] + jnp.sum(member, axis=0, keepdims=True)
    cnt_ref[...] = run_ref[...].astype(jnp.int32)


def _router(h3, router_w, router_bias):
    B, Lp, D = h3.shape
    E = router_w.shape[1]
    tm = Lp // 4
    out = lambda: pl.BlockSpec((1, tm, BLK), lambda b, j: (b, j, 0))
    return pl.pallas_call(
        functools.partial(_router_kernel, tm=tm),
        grid=(B, Lp // tm),
        in_specs=[pl.BlockSpec((1, tm, D), lambda b, j: (b, j, 0)),
                  pl.BlockSpec((D, E), lambda b, j: (0, 0)),
                  pl.BlockSpec((1, E), lambda b, j: (0, 0))],
        out_specs=[out(), out(), out(), pl.BlockSpec((1, E), lambda b, j: (0, 0))],
        out_shape=[jax.ShapeDtypeStruct((B, Lp, BLK), jnp.int32),
                   jax.ShapeDtypeStruct((B, Lp, BLK), F32),
                   jax.ShapeDtypeStruct((B, Lp, BLK), jnp.int32),
                   jax.ShapeDtypeStruct((1, E), jnp.int32)],
        scratch_shapes=[pltpu.VMEM((1, E), F32)],
        compiler_params=_cparams(("arbitrary", "arbitrary")),
        name="router",
    )(h3, router_w, router_bias.reshape(1, E))


def _swiglu(x, w13, w2):
    h1 = jnp.dot(x, w13, preferred_element_type=F32)
    f = h1.shape[1] // 2
    gate, up = h1[:, :f], h1[:, f:]
    act = (gate * _sigmoid(gate) * up).astype(BF16)
    return jnp.dot(act, w2, preferred_element_type=F32)


def _shared_kernel(x_ref, w13_ref, w2_ref, o_ref):
    o_ref[...] = _swiglu(x_ref[...].astype(BF16), w13_ref[...], w2_ref[...])


def _shared_expert(h, w13, w2):
    T, D = h.shape
    row = pl.BlockSpec((ROW_TILE, D), lambda i: (i, 0))
    return pl.pallas_call(
        _shared_kernel,
        grid=(T // ROW_TILE,),
        in_specs=[row, pl.BlockSpec(w13.shape, lambda i: (0, 0)),
                  pl.BlockSpec(w2.shape, lambda i: (0, 0))],
        out_specs=row,
        out_shape=jax.ShapeDtypeStruct((T, D), F32),
        compiler_params=_cparams(("parallel",)),
        name="shared_expert",
    )(h, w13, w2)


def _row_copy(x_hbm, buf, sem, tok, slot, r):
    return pltpu.make_async_copy(x_hbm.at[pl.ds(tok, 1)], buf.at[slot, pl.ds(r, 1)], sem.at[slot])


def _moe_kernel(be_ref, nb_ref, tok_ref, tokn_ref, x_hbm, w13_ref, w2_ref, y_ref,
                xbuf, sem, w13b, w2b):
    i = pl.program_id(0)
    n_used = nb_ref[0]
    slot = lax.rem(i, 2)

    def gather(tref, s):
        def body(r, c):
            _row_copy(x_hbm, xbuf, sem, tref[0, 0, r], s, r).start()
            return c
        lax.fori_loop(0, EXPERT_TILE, body, 0)

    @pl.when(jnp.logical_and(i == 0, n_used > 0))
    def _():
        gather(tok_ref, 0)

    @pl.when(i + 1 < n_used)
    def _():
        gather(tokn_ref, 1 - slot)

    @pl.when(i < n_used)
    def _():
        e = be_ref[i]
        prev = be_ref[jnp.maximum(i - 1, 0)]

        @pl.when(jnp.logical_or(i == 0, e != prev))
        def _():
            w13b[...] = w13_ref[0].astype(BF16)
            w2b[...] = w2_ref[0].astype(BF16)

        def wait(r, c):
            _row_copy(x_hbm, xbuf, sem, 0, slot, r).wait()
            return c
        lax.fori_loop(0, EXPERT_TILE, wait, 0)
        x = xbuf[slot].astype(BF16)
        y_ref[...] = _swiglu(x, w13b[...], w2b[...])

    @pl.when(i >= n_used)
    def _():
        y_ref[...] = jnp.zeros_like(y_ref)


def _routed_experts(h, tok_blocks, block_e, n_used, w13, w2):
    T, D = h.shape
    nb = tok_blocks.shape[0]
    E, _, F2 = w13.shape
    F = F2 // 2
    grid_spec = pltpu.PrefetchScalarGridSpec(
        num_scalar_prefetch=2,
        grid=(nb,),
        in_specs=[
            pl.BlockSpec((1, 1, EXPERT_TILE), lambda i, be, nu: (i, 0, 0), memory_space=pltpu.SMEM),
            pl.BlockSpec((1, 1, EXPERT_TILE), lambda i, be, nu: (jnp.minimum(i + 1, nb - 1), 0, 0),
                         memory_space=pltpu.SMEM),
            pl.BlockSpec(memory_space=pl.ANY),
            pl.BlockSpec((1, D, F2), lambda i, be, nu: (be[i], 0, 0)),
            pl.BlockSpec((1, F, D), lambda i, be, nu: (be[i], 0, 0)),
        ],
        out_specs=pl.BlockSpec((EXPERT_TILE, D), lambda i, be, nu: (i, 0)),
        scratch_shapes=[pltpu.VMEM((2, EXPERT_TILE, D), F32),
                        pltpu.SemaphoreType.DMA((2,)),
                        pltpu.VMEM((D, F2), BF16),
                        pltpu.VMEM((F, D), BF16)],
    )
    return pl.pallas_call(
        _moe_kernel,
        grid_spec=grid_spec,
        out_shape=jax.ShapeDtypeStruct((nb * EXPERT_TILE, D), F32),
        compiler_params=_cparams(("arbitrary",)),
        name="routed_experts",
    )(block_e, n_used, tok_blocks, tok_blocks, h, w13, w2)


def _combine_kernel(pos_ref, posn_ref, y_hbm, gate_ref, sh_ref, h_ref, g_ref, b_ref,
                    o_ref, ob_ref, gbuf, sem, *, alpha, n_steps):
    i = pl.program_id(0)
    slot = lax.rem(i, 2)
    n_rows = TOP_K * COMBINE_TILE

    def copy(pos, s, r):
        return pltpu.make_async_copy(y_hbm.at[pl.ds(pos, 1)], gbuf.at[s, pl.ds(r, 1)], sem.at[s])

    def gather(pref, s):
        def body(r, c):
            copy(pref[0, 0, r], s, r).start()
            return c
        lax.fori_loop(0, n_rows, body, 0)

    @pl.when(i == 0)
    def _():
        gather(pos_ref, 0)

    @pl.when(i + 1 < n_steps)
    def _():
        gather(posn_ref, 1 - slot)

    def wait(r, c):
        copy(0, slot, r).wait()
        return c
    lax.fori_loop(0, n_rows, wait, 0)

    gate = gate_ref[...]
    acc = alpha * h_ref[...] + sh_ref[...]
    for kk in range(TOP_K):
        rows = gbuf[slot, kk * COMBINE_TILE:(kk + 1) * COMBINE_TILE, :]
        acc = acc + gate[:, kk:kk + 1] * rows
    y = _layer_norm(acc, g_ref[...], b_ref[...])
    o_ref[...] = y
    ob_ref[...] = y.astype(BF16)


def _combine_ln(pos_blocks, y, gate, shared, h, g, b, alpha):
    T, D = h.shape
    n_steps = T // COMBINE_TILE
    n_rows = TOP_K * COMBINE_TILE
    row = lambda n: pl.BlockSpec((COMBINE_TILE, n), lambda i: (i, 0))
    vec = pl.BlockSpec((1, D), lambda i: (0, 0))
    return pl.pallas_call(
        functools.partial(_combine_kernel, alpha=alpha, n_steps=n_steps),
        grid=(n_steps,),
        in_specs=[
            pl.BlockSpec((1, 1, n_rows), lambda i: (i, 0, 0), memory_space=pltpu.SMEM),
            pl.BlockSpec((1, 1, n_rows), lambda i: (jnp.minimum(i + 1, n_steps - 1), 0, 0),
                         memory_space=pltpu.SMEM),
            pl.BlockSpec(memory_space=pl.ANY),
            row(BLK), row(D), row(D), vec, vec],
        out_specs=[row(D), row(D)],
        out_shape=[jax.ShapeDtypeStruct((T, D), F32), jax.ShapeDtypeStruct((T, D), BF16)],
        scratch_shapes=[pltpu.VMEM((2, n_rows, D), F32), pltpu.SemaphoreType.DMA((2,))],
        compiler_params=_cparams(("arbitrary",)),
        name="combine_ln",
    )(pos_blocks, pos_blocks, y, gate, shared, h, g.reshape(1, D), b.reshape(1, D))


def _arrange_in_proj(w_in, b_igate, b_fgate):
    o_v = 2 * D_ML
    o_o = o_v + D_ML
    o_i = o_o + D_ML
    o_f = o_i + ML_HEADS
    o_sb = o_f + ML_HEADS
    w_main = jnp.concatenate([w_in[:, :o_i], w_in[:, o_sb:]], axis=1).astype(BF16)
    D = w_in.shape[0]
    wg = jnp.zeros((D, ML_GROUPS, BLK), F32)
    bias = jnp.zeros((ML_GROUPS, BLK), F32)
    w_i = w_in[:, o_i:o_f].reshape(D, ML_GROUPS, ML_HPS)
    w_f = w_in[:, o_f:o_sb].reshape(D, ML_GROUPS, ML_HPS)
    wg = wg.at[:, :, :ML_HPS].set(w_i).at[:, :, ML_HPS:2 * ML_HPS].set(w_f)
    bias = bias.at[:, :ML_HPS].set(b_igate.reshape(ML_GROUPS, ML_HPS))
    bias = bias.at[:, ML_HPS:2 * ML_HPS].set(b_fgate.reshape(ML_GROUPS, ML_HPS))
    return w_main, wg.reshape(D, ML_GROUPS * BLK).astype(BF16), bias.reshape(1, ML_GROUPS * BLK)


def _dispatch_plan(idx, rank, counts, B, Lp):
    T = B * Lp
    A_max = B * (Lp - PAD) * TOP_K
    nb = -(-A_max // EXPERT_TILE) + N_EXPERTS
    P = nb * EXPERT_TILE
    padded = (counts + EXPERT_TILE - 1) // EXPERT_TILE * EXPERT_TILE
    ends = jnp.cumsum(padded)
    poff = ends - padded
    idx_k = idx[:, :, :TOP_K]
    pos = poff[idx_k] + rank[:, :, :TOP_K]
    real = (jnp.arange(Lp) >= PAD)[None, :, None]
    tok = jnp.broadcast_to(jnp.arange(T, dtype=jnp.int32).reshape(B, Lp, 1), pos.shape)
    scatter_pos = jnp.where(real, pos, P).reshape(-1)
    first_real = PAD
    buf_tok = jnp.full((P,), first_real, jnp.int32).at[scatter_pos].set(tok.reshape(-1), mode="drop")
    block_e = jnp.minimum(
        jnp.searchsorted(ends, jnp.arange(nb, dtype=jnp.int32) * EXPERT_TILE, side="right"),
        N_EXPERTS - 1).astype(jnp.int32)
    n_used = (ends[-1] // EXPERT_TILE).astype(jnp.int32).reshape(1)
    pos = jnp.where(real, pos, 0).astype(jnp.int32)
    return pos.reshape(T, TOP_K), buf_tok.reshape(nb, 1, EXPERT_TILE), block_e, n_used


def kernel(x, meta, ln_in_g, ln_in_b, w_in, b_igate, b_fgate, conv_qk, ml_norm_g, sb_norm_g, pool_w, pool_scale, w_out, ln1_g, ln1_b, router_w, router_bias, exp_w13, exp_w2, sh_w13, sh_w2, ln2_g, ln2_b):
    B, S, D = x.shape
    depth = w_in.shape[0]
    alpha = (2 * depth) ** 0.25
    Lp = PAD + N_META + S
    T = B * Lp
    assert Lp % BLK == 0 and T % ROW_TILE == 0 and T % COMBINE_TILE == 0

    h0 = jnp.concatenate([jnp.zeros((B, PAD, D), x.dtype),
                          jnp.broadcast_to(meta[None].astype(x.dtype), (B, N_META, D)), x], axis=1)
    h, hb = _ln_rows(h0.reshape(T, D), ln_in_g, ln_in_b)

    for l in range(depth):
        w_main, w_gate, gate_bias = _arrange_in_proj(w_in[l], b_igate[l], b_fgate[l])
        proj3 = _in_proj(hb, w_main).reshape(B, Lp, N_MAIN)
        gcol = _gates(hb.reshape(B, Lp, D), w_gate, gate_bias)
        grow = jnp.transpose(gcol.reshape(B, Lp, ML_GROUPS, BLK)[..., :8], (0, 2, 3, 1))
        qk3 = _conv_silu(proj3, conv_qk[l])
        a_ml = _mlstm(qk3, proj3, gcol, grow, ml_norm_g[l])
        a_sb = _stick_breaking(proj3, sb_norm_g[l])
        a_pool = _pool(proj3, pool_w[l], pool_scale[l])
        h1 = _outproj_ln(a_ml.reshape(T, D_ML), a_sb.reshape(T, D_SB), a_pool.reshape(T, D_POOL),
                         w_out[l].astype(BF16), h, ln1_g[l], ln1_b[l], alpha)

        idx, gate, rank, counts = _router(h1.reshape(B, Lp, D), router_w[l], router_bias[l])
        pos, tok_blocks, block_e, n_used = _dispatch_plan(idx, rank, counts[0], B, Lp)
        y = _routed_experts(h1, tok_blocks, block_e, n_used, exp_w13[l], exp_w2[l])
        shared = _shared_expert(h1, sh_w13[l].astype(BF16), sh_w2[l].astype(BF16))
        n_steps = T // COMBINE_TILE
        pos_blocks = jnp.transpose(pos.reshape(n_steps, COMBINE_TILE, TOP_K), (0, 2, 1))
        pos_blocks = pos_blocks.reshape(n_steps, 1, TOP_K * COMBINE_TILE)
        h, hb = _combine_ln(pos_blocks, y, gate.reshape(T, BLK), shared, h1,
                            ln2_g[l], ln2_b[l], alpha)

    return h.reshape(B, Lp, D)[:, PAD + N_META:]
```

```python
import functools

import jax
import jax.numpy as jnp
from jax import lax
from jax.experimental import pallas as pl
from jax.experimental.pallas import tpu as pltpu

N_META = 16
HEAD_DIM = 128
ML_HEADS = 6
SB_HEADS = 6
POOL_WINDOWS = (2, 4, 8, 16)
POOL_GROUPS = 4
D_ML = ML_HEADS * HEAD_DIM
D_SB = SB_HEADS * HEAD_DIM
D_POOL = POOL_GROUPS * HEAD_DIM
CONV_W = 4
N_EXPERTS = 64
TOP_K = 8
ROUTED_SCALE = 2.5
LN_EPS = 1e-5
RMS_EPS = 1e-6

BLK = 128
PAD = (-N_META) % BLK
ML_HPS = 2
ML_GROUPS = ML_HEADS // ML_HPS
SB_KEY_BLOCKS = 4
ROW_TILE = 512
EXPERT_TILE = 256
COMBINE_TILE = 128
TOK_ALIGN_LOG2 = 10
TOK_ALIGN = 1 << TOK_ALIGN_LOG2
TOK_WINDOW = 2 * TOK_ALIGN
NEG = -1e30
VMEM_LIMIT = 56 * 1024 * 1024

C_QK = 0
C_V = 2 * D_ML
C_O = C_V + D_ML
C_SBQ = C_O + D_ML
C_SBK = C_SBQ + D_SB
C_SBV = C_SBK + D_SB
C_POOL = C_SBV + D_SB
N_MAIN = C_POOL + D_POOL

F32 = jnp.float32
BF16 = jnp.bfloat16


def _cparams(sem, vmem=VMEM_LIMIT):
    return pltpu.CompilerParams(dimension_semantics=sem, vmem_limit_bytes=vmem)


def _layer_norm(x, g, b):
    mu = jnp.mean(x, axis=-1, keepdims=True)
    xc = x - mu
    var = jnp.mean(xc * xc, axis=-1, keepdims=True)
    return xc * lax.rsqrt(var + LN_EPS) * g + b


def _sigmoid(x):
    return 1.0 / (1.0 + jnp.exp(-x))


def _softplus(x):
    return jnp.maximum(x, 0.0) + jnp.log(1.0 + jnp.exp(-jnp.abs(x)))


def _split_dot(a, b, split_lhs):
    x = a if split_lhs else b
    out = None
    for _ in range(3):
        part = x.astype(BF16)
        x = x - part.astype(F32)
        term = (jnp.dot(part, b, preferred_element_type=F32) if split_lhs
                else jnp.dot(a, part, preferred_element_type=F32))
        out = term if out is None else out + term
    return out


def _ln_kernel(x_ref, g_ref, b_ref, o_ref, ob_ref):
    y = _layer_norm(x_ref[...], g_ref[...], b_ref[...])
    o_ref[...] = y
    ob_ref[...] = y.astype(BF16)


def _ln_rows(x, g, b):
    T, D = x.shape
    row = pl.BlockSpec((ROW_TILE, D), lambda i: (i, 0))
    vec = pl.BlockSpec((1, D), lambda i: (0, 0))
    return pl.pallas_call(
        _ln_kernel,
        grid=(T // ROW_TILE,),
        in_specs=[row, vec, vec],
        out_specs=[row, row],
        out_shape=[jax.ShapeDtypeStruct((T, D), F32), jax.ShapeDtypeStruct((T, D), BF16)],
        compiler_params=_cparams(("parallel",)),
        name="ln_in",
    )(x, g.reshape(1, D), b.reshape(1, D))


def _mm_kernel(a_ref, w_ref, o_ref):
    o_ref[...] = jnp.dot(a_ref[...], w_ref[...], preferred_element_type=F32).astype(o_ref.dtype)


def _in_proj(a, w):
    M, K = a.shape
    N = w.shape[1]
    tn = N // 2
    return pl.pallas_call(
        _mm_kernel,
        grid=(N // tn, M // ROW_TILE),
        in_specs=[pl.BlockSpec((ROW_TILE, K), lambda j, i: (i, 0)),
                  pl.BlockSpec((K, tn), lambda j, i: (0, j))],
        out_specs=pl.BlockSpec((ROW_TILE, tn), lambda j, i: (i, j)),
        out_shape=jax.ShapeDtypeStruct((M, N), BF16),
        compiler_params=_cparams(("parallel", "parallel")),
        name="in_proj",
    )(a, w)


def _gates_kernel(hb_ref, wg_ref, bias_ref, o_ref, *, tm):
    j = pl.program_id(1)
    g = jnp.dot(hb_ref[0], wg_ref[...], preferred_element_type=F32) + bias_ref[...]
    col = jnp.bitwise_and(lax.broadcasted_iota(jnp.int32, g.shape, 1), BLK - 1)
    row = lax.broadcasted_iota(jnp.int32, g.shape, 0) + j * tm
    is_f = jnp.logical_and(col >= ML_HPS, col < 2 * ML_HPS)
    val = jnp.where(is_f, -_softplus(-g), g)
    pad_val = jnp.where(is_f, 0.0, NEG)
    o_ref[0] = jnp.where(row < PAD, pad_val, val)


def _gates(hb3, wg, bias):
    B, Lp, D = hb3.shape
    tm = Lp // 4
    Ng = wg.shape[1]
    return pl.pallas_call(
        functools.partial(_gates_kernel, tm=tm),
        grid=(B, Lp // tm),
        in_specs=[pl.BlockSpec((1, tm, D), lambda b, j: (b, j, 0)),
                  pl.BlockSpec((D, Ng), lambda b, j: (0, 0)),
                  pl.BlockSpec((1, Ng), lambda b, j: (0, 0))],
        out_specs=pl.BlockSpec((1, tm, Ng), lambda b, j: (b, j, 0)),
        out_shape=jax.ShapeDtypeStruct((B, Lp, Ng), F32),
        compiler_params=_cparams(("parallel", "parallel")),
        name="ml_gates",
    )(hb3, wg, bias)


def _conv_kernel(x_ref, w_ref, o_ref):
    c = pl.program_id(1)
    x = x_ref[0].astype(F32)
    row = lax.broadcasted_iota(jnp.int32, x.shape, 0)
    x = jnp.where(row < PAD, 0.0, x)
    w = w_ref[...]
    y = x * w[CONV_W - 1:CONV_W]
    for s in range(1, CONV_W):
        y = y + pltpu.roll(x, s, axis=0) * w[CONV_W - 1 - s:CONV_W - s]
    y = y * _sigmoid(y)
    scale = jnp.where(c < ML_HEADS, HEAD_DIM ** -0.5, 1.0).astype(F32)
    o_ref[0] = (y * scale).astype(BF16)


def _conv_silu(proj3, conv_w):
    B, Lp, _ = proj3.shape
    nc = 2 * D_ML // BLK
    return pl.pallas_call(
        _conv_kernel,
        grid=(B, nc),
        in_specs=[pl.BlockSpec((1, Lp, BLK), lambda b, c: (b, 0, c)),
                  pl.BlockSpec((CONV_W, BLK), lambda b, c: (0, c))],
        out_specs=pl.BlockSpec((1, Lp, BLK), lambda b, c: (b, 0, c)),
        out_shape=jax.ShapeDtypeStruct((B, Lp, 2 * D_ML), BF16),
        compiler_params=_cparams(("parallel", "parallel")),
        name="conv_silu",
    )(proj3, conv_w)


def _mlstm_kernel(q_ref, k_ref, v_ref, o_ref, gc_ref, gr_ref, ng_ref, out_ref, st_ref, m_ref):
    n_chunks = q_ref.shape[1] // BLK
    st_ref[...] = jnp.zeros_like(st_ref)
    m_ref[...] = jnp.zeros_like(m_ref)

    ri = lax.broadcasted_iota(jnp.int32, (BLK, BLK), 0)
    ci = lax.broadcasted_iota(jnp.int32, (BLK, BLK), 1)
    causal = ci <= ri
    lower_incl = jnp.where(causal, 1.0, 0.0).astype(BF16)
    upper_incl = jnp.where(ri <= ci, 1.0, 0.0).astype(BF16)
    ones_col = jnp.where(ci == 0, 1.0, 0.0).astype(BF16)

    def chunk(c, carry):
        r0 = pl.multiple_of(c * BLK, BLK)
        gcol = gc_ref[0, pl.ds(r0, BLK), :]
        grow = gr_ref[0, 0, :, pl.ds(r0, BLK)]
        bcol = _split_dot(lower_incl, gcol, split_lhs=False)
        brow = _split_dot(grow, upper_incl, split_lhs=True)
        for hh in range(ML_HPS):
            cs = slice(hh * HEAD_DIM, (hh + 1) * HEAD_DIM)
            q = q_ref[0, pl.ds(r0, BLK), cs]
            k = k_ref[0, pl.ds(r0, BLK), cs]
            v = v_ref[0, pl.ds(r0, BLK), cs]
            li_c = gcol[:, hh:hh + 1]
            b_c = bcol[:, ML_HPS + hh:ML_HPS + hh + 1]
            li_r = grow[hh:hh + 1, :]
            b_r = brow[ML_HPS + hh:ML_HPS + hh + 1, :]
            m = m_ref[hh, 0:1, 0:1]
            st = st_ref[hh]

            inter = b_c + m
            dmat = jnp.where(causal, b_c - b_r + li_r, NEG)
            m_t = jnp.maximum(inter, jnp.max(dmat, axis=-1, keepdims=True))
            w_inter = jnp.exp(inter - m_t)
            s_qk = lax.dot_general(q, k, (((1,), (1,)), ((), ())), preferred_element_type=F32)
            ws = jnp.exp(dmat - m_t) * s_qk
            v_aug = jnp.concatenate([v, ones_col], axis=1)
            nd = (w_inter * jnp.dot(q, st.astype(BF16), preferred_element_type=F32)
                  + jnp.dot(ws.astype(BF16), v_aug, preferred_element_type=F32))
            num = nd[:, :HEAD_DIM]
            den = nd[:, HEAD_DIM:HEAD_DIM + 1]
            h = num / jnp.maximum(jnp.abs(den), jnp.exp(-m_t))

            og = o_ref[0, pl.ds(r0, BLK), cs].astype(F32)
            hn = h * lax.rsqrt(jnp.mean(h * h, axis=-1, keepdims=True) + RMS_EPS)
            hn = hn * ng_ref[:, cs] * _sigmoid(og)
            out_ref[0, pl.ds(r0, BLK), cs] = hn.astype(BF16)

            b_last = b_r[:, BLK - 1:BLK]
            g_r = b_last - b_r + li_r
            m_new = jnp.maximum(b_last + m, jnp.max(g_r, axis=-1, keepdims=True))
            wg = jnp.exp(b_last - b_c + li_c - m_new)
            decay = jnp.exp(b_last + m - m_new)
            wv = (wg * v_aug.astype(F32)).astype(BF16)
            kv = lax.dot_general(k, wv, (((0,), (0,)), ((), ())), preferred_element_type=F32)
            st_ref[hh] = decay * st + kv
            m_ref[hh] = jnp.broadcast_to(m_new, m_ref.shape[1:])
        return carry

    lax.fori_loop(0, n_chunks, chunk, 0)


def _mlstm(qk3, proj3, gcol, grow, norm_g):
    B, Lp, _ = qk3.shape
    W = ML_HPS * HEAD_DIM
    nq = D_ML // W
    seq = lambda off: pl.BlockSpec((1, Lp, W), lambda b, g: (b, 0, off + g))
    return pl.pallas_call(
        _mlstm_kernel,
        grid=(B, ML_GROUPS),
        in_specs=[seq(0), seq(nq),
                  pl.BlockSpec((1, Lp, W), lambda b, g: (b, 0, C_V // W + g)),
                  pl.BlockSpec((1, Lp, W), lambda b, g: (b, 0, C_O // W + g)),
                  pl.BlockSpec((1, Lp, BLK), lambda b, g: (b, 0, g)),
                  pl.BlockSpec((1, 1, 8, Lp), lambda b, g: (b, g, 0, 0)),
                  pl.BlockSpec((1, W), lambda b, g: (0, g))],
        out_specs=pl.BlockSpec((1, Lp, W), lambda b, g: (b, 0, g)),
        out_shape=jax.ShapeDtypeStruct((B, Lp, D_ML), BF16),
        scratch_shapes=[pltpu.VMEM((ML_HPS, HEAD_DIM, 2 * HEAD_DIM), F32),
                        pltpu.VMEM((ML_HPS, 8, BLK), F32)],
        compiler_params=_cparams(("parallel", "parallel")),
        name="mlstm",
    )(qk3, qk3, proj3, proj3, gcol, grow, norm_g.reshape(1, D_ML))


def _sb_kernel(q_ref, k_ref, v_ref, g_ref, o_ref):
    i = pl.program_id(2)
    q = q_ref[0]
    W = SB_KEY_BLOCKS * BLK
    r1 = lax.broadcasted_iota(jnp.int32, (BLK, BLK), 0)
    c1 = lax.broadcasted_iota(jnp.int32, (BLK, BLK), 1)
    upper_strict = jnp.where(r1 > c1, 1.0, 0.0).astype(BF16)
    rowp = lax.broadcasted_iota(jnp.int32, (BLK, W), 0) + i * BLK
    ci = lax.broadcasted_iota(jnp.int32, (BLK, W), 1)
    scale = HEAD_DIM ** -0.5

    def step(t, state):
        carry, acc = state
        end = (i + 1) * BLK - t * W
        s0 = pl.multiple_of(jnp.maximum(end - W, 0), BLK)
        k = k_ref[0, pl.ds(s0, W), :]
        v = v_ref[0, pl.ds(s0, W), :]
        z = lax.dot_general(q, k, (((1,), (1,)), ((), ())), preferred_element_type=F32) * scale
        sp = _softplus(z)
        colp = ci + s0
        valid = jnp.logical_and(colp < jnp.minimum(rowp, end), colp >= PAD)
        lk = jnp.where(valid, -sp, 0.0)
        hi = lk.astype(BF16)
        lo = (lk - hi.astype(F32)).astype(BF16)
        pieces = [None] * SB_KEY_BLOCKS
        off = carry
        for c in reversed(range(SB_KEY_BLOCKS)):
            sl = slice(c * BLK, (c + 1) * BLK)
            local = (jnp.dot(hi[:, sl], upper_strict, preferred_element_type=F32)
                     + jnp.dot(lo[:, sl], upper_strict, preferred_element_type=F32))
            pieces[c] = local + off
            off = off + jnp.sum(lk[:, sl], axis=-1, keepdims=True)
        after = jnp.concatenate(pieces, axis=1)
        a = jnp.where(valid, jnp.exp(z - sp + after), 0.0)
        acc = acc + jnp.dot(a.astype(BF16), v, preferred_element_type=F32)
        return off, acc

    init = (jnp.zeros((BLK, 1), F32), jnp.zeros((BLK, HEAD_DIM), F32))
    n_steps = (i + SB_KEY_BLOCKS) // SB_KEY_BLOCKS
    _, acc = lax.fori_loop(0, n_steps, step, init)
    hn = acc * lax.rsqrt(jnp.mean(acc * acc, axis=-1, keepdims=True) + RMS_EPS) * g_ref[...]
    o_ref[0] = hn.astype(BF16)


def _stick_breaking(proj3, norm_g):
    B, Lp, _ = proj3.shape
    nq = Lp // BLK
    return pl.pallas_call(
        _sb_kernel,
        grid=(B, SB_HEADS, nq),
        in_specs=[pl.BlockSpec((1, BLK, HEAD_DIM), lambda b, h, i: (b, i, C_SBQ // HEAD_DIM + h)),
                  pl.BlockSpec((1, Lp, HEAD_DIM), lambda b, h, i: (b, 0, C_SBK // HEAD_DIM + h)),
                  pl.BlockSpec((1, Lp, HEAD_DIM), lambda b, h, i: (b, 0, C_SBV // HEAD_DIM + h)),
                  pl.BlockSpec((1, HEAD_DIM), lambda b, h, i: (0, h))],
        out_specs=pl.BlockSpec((1, BLK, HEAD_DIM), lambda b, h, i: (b, i, h)),
        out_shape=jax.ShapeDtypeStruct((B, Lp, D_SB), BF16),
        compiler_params=_cparams(("parallel", "parallel", "parallel")),
        name="stick_breaking",
    )(proj3, proj3, proj3, norm_g.reshape(1, D_SB))


def _pool_kernel(x_ref, w_ref, s_ref, o_ref):
    g = pl.program_id(1)
    x = x_ref[0].astype(F32)
    row = lax.broadcasted_iota(jnp.int32, x.shape, 0)
    x = jnp.where(row < PAD, 0.0, x)
    pos1 = (row - PAD + 1).astype(F32)
    sums = x
    mean = jnp.zeros_like(x)
    shift = 1
    for gi, win in enumerate(POOL_WINDOWS):
        while shift < win:
            sums = sums + pltpu.roll(sums, shift, axis=0)
            shift *= 2
        cnt = jnp.maximum(jnp.minimum(pos1, float(win)), 1.0)
        mean = jnp.where(g == gi, sums / cnt, mean)
    p = (mean - x).astype(BF16)
    y = jnp.dot(p, w_ref[0].astype(BF16), preferred_element_type=F32) * s_ref[...]
    o_ref[0] = y.astype(BF16)


def _pool(proj3, pool_w, pool_scale):
    B, Lp, _ = proj3.shape
    return pl.pallas_call(
        _pool_kernel,
        grid=(B, POOL_GROUPS),
        in_specs=[pl.BlockSpec((1, Lp, HEAD_DIM), lambda b, g: (b, 0, C_POOL // HEAD_DIM + g)),
                  pl.BlockSpec((1, HEAD_DIM, HEAD_DIM), lambda b, g: (g, 0, 0)),
                  pl.BlockSpec((1, HEAD_DIM), lambda b, g: (0, g))],
        out_specs=pl.BlockSpec((1, Lp, HEAD_DIM), lambda b, g: (b, 0, g)),
        out_shape=jax.ShapeDtypeStruct((B, Lp, D_POOL), BF16),
        compiler_params=_cparams(("parallel", "parallel")),
        name="pool_mixer",
    )(proj3, pool_w, pool_scale.reshape(1, D_POOL))


def _outproj_kernel(a1_ref, a2_ref, a3_ref, w_ref, h_ref, g_ref, b_ref, o_ref, *, alpha):
    acc = jnp.dot(a1_ref[...], w_ref[0:D_ML, :], preferred_element_type=F32)
    acc = acc + jnp.dot(a2_ref[...], w_ref[D_ML:D_ML + D_SB, :], preferred_element_type=F32)
    acc = acc + jnp.dot(a3_ref[...], w_ref[D_ML + D_SB:, :], preferred_element_type=F32)
    o_ref[...] = _layer_norm(alpha * h_ref[...] + acc, g_ref[...], b_ref[...])


def _outproj_ln(a_ml, a_sb, a_pool, w, h, g, b, alpha):
    T, D = h.shape
    row = lambda n: pl.BlockSpec((ROW_TILE, n), lambda i: (i, 0))
    vec = pl.BlockSpec((1, D), lambda i: (0, 0))
    return pl.pallas_call(
        functools.partial(_outproj_kernel, alpha=alpha),
        grid=(T // ROW_TILE,),
        in_specs=[row(D_ML), row(D_SB), row(D_POOL),
                  pl.BlockSpec(w.shape, lambda i: (0, 0)), row(D), vec, vec],
        out_specs=row(D),
        out_shape=jax.ShapeDtypeStruct((T, D), F32),
        compiler_params=_cparams(("parallel",)),
        name="out_proj_ln",
    )(a_ml, a_sb, a_pool, w, h, g.reshape(1, D), b.reshape(1, D))


def _router_kernel(h_ref, rw_ref, rb_ref, idx_ref, gate_ref, rank_ref, cnt_ref, run_ref, *, tm):
    first = jnp.logical_and(pl.program_id(0) == 0, pl.program_id(1) == 0)

    @pl.when(first)
    def _():
        run_ref[...] = jnp.zeros_like(run_ref)

    x = h_ref[0]
    logits = jnp.dot(x, rw_ref[...], preferred_element_type=F32, precision=lax.Precision.HIGHEST)
    scores = _sigmoid(logits)
    work = scores + rb_ref[...]
    E = scores.shape[1]
    lane = lax.broadcasted_iota(jnp.int32, (tm, E), 1).astype(F32)
    row = lax.broadcasted_iota(jnp.int32, (tm, 1), 0) + pl.program_id(1) * tm
    real = (row >= PAD).astype(F32)

    picks = []
    gates = []
    member = jnp.zeros((tm, E), F32)
    for _ in range(TOP_K):
        mx = jnp.max(work, axis=-1, keepdims=True)
        pick = jnp.min(jnp.where(work == mx, lane, float(E)), axis=-1, keepdims=True)
        onehot = lane == pick
        gates.append(jnp.sum(jnp.where(onehot, scores, 0.0), axis=-1, keepdims=True))
        picks.append(pick)
        member = member + jnp.where(onehot, real, 0.0)
        work = jnp.where(onehot, -jnp.inf, work)
    gsum = gates[0]
    for gk in gates[1:]:
        gsum = gsum + gk

    ri = lax.broadcasted_iota(jnp.int32, (tm, tm), 0)
    ci = lax.broadcasted_iota(jnp.int32, (tm, tm), 1)
    lower_strict = jnp.where(ci < ri, 1.0, 0.0).astype(BF16)
    before = jnp.dot(lower_strict, member.astype(BF16), preferred_element_type=F32) + run_ref[...]

    lane_o = lax.broadcasted_iota(jnp.int32, (tm, BLK), 1)
    idx_o = jnp.zeros((tm, BLK), F32)
    gate_o = jnp.zeros((tm, BLK), F32)
    rank_o = jnp.zeros((tm, BLK), F32)
    for kk in range(TOP_K):
        onehot = lane == picks[kk]
        rk = jnp.sum(jnp.where(onehot, before, 0.0), axis=-1, keepdims=True)
        sel = lane_o == kk
        idx_o = jnp.where(sel, picks[kk], idx_o)
        gate_o = jnp.where(sel, gates[kk] / gsum * ROUTED_SCALE, gate_o)
        rank_o = jnp.where(sel, rk, rank_o)
    idx_ref[0] = idx_o.astype(jnp.int32)
    gate_ref[0] = gate_o
    rank_ref[0] = rank_o.astype(jnp.int32)
    run_ref[...] = run_ref[...] + jnp.sum(member, axis=0, keepdims=True)
    cnt_ref[...] = run_ref[...].astype(jnp.int32)


def _router(h3, router_w, router_bias):
    B, Lp, D = h3.shape
    E = router_w.shape[1]
    tm = Lp // 4
    out = lambda: pl.BlockSpec((1, tm, BLK), lambda b, j: (b, j, 0))
    return pl.pallas_call(
        functools.partial(_router_kernel, tm=tm),
        grid=(B, Lp // tm),
        in_specs=[pl.BlockSpec((1, tm, D), lambda b, j: (b, j, 0)),
                  pl.BlockSpec((D, E), lambda b, j: (0, 0)),
                  pl.BlockSpec((1, E), lambda b, j: (0, 0))],
        out_specs=[out(), out(), out(), pl.BlockSpec((1, E), lambda b, j: (0, 0))],
        out_shape=[jax.ShapeDtypeStruct((B, Lp, BLK), jnp.int32),
                   jax.ShapeDtypeStruct((B, Lp, BLK), F32),
                   jax.ShapeDtypeStruct((B, Lp, BLK), jnp.int32),
                   jax.ShapeDtypeStruct((1, E), jnp.int32)],
        scratch_shapes=[pltpu.VMEM((1, E), F32)],
        compiler_params=_cparams(("arbitrary", "arbitrary")),
        name="router",
    )(h3, router_w, router_bias.reshape(1, E))


def _swiglu(x, w13, w2):
    h1 = jnp.dot(x, w13, preferred_element_type=F32)
    f = h1.shape[1] // 2
    gate, up = h1[:, :f], h1[:, f:]
    act = (gate * _sigmoid(gate) * up).astype(BF16)
    return jnp.dot(act, w2, preferred_element_type=F32)


def _shared_kernel(x_ref, w13_ref, w2_ref, o_ref):
    o_ref[...] = _swiglu(x_ref[...].astype(BF16), w13_ref[...], w2_ref[...])


def _shared_expert(h, w13, w2):
    T, D = h.shape
    row = pl.BlockSpec((ROW_TILE, D), lambda i: (i, 0))
    return pl.pallas_call(
        _shared_kernel,
        grid=(T // ROW_TILE,),
        in_specs=[row, pl.BlockSpec(w13.shape, lambda i: (0, 0)),
                  pl.BlockSpec(w2.shape, lambda i: (0, 0))],
        out_specs=row,
        out_shape=jax.ShapeDtypeStruct((T, D), F32),
        compiler_params=_cparams(("parallel",)),
        name="shared_expert",
    )(h, w13, w2)


def _moe_kernel(be_ref, bs_ref, nb_ref, tok_hbm, x_hbm, w13_ref, w2_ref, y_ref,
                win, wsem, xbuf, xsem, w13b, w2b):
    i = pl.program_id(0)
    n_used = nb_ref[0]
    slot = lax.rem(i, 2)

    def window(blk, s):
        base = pl.multiple_of(lax.shift_left(lax.shift_right_logical(bs_ref[blk], TOK_ALIGN_LOG2),
                                             TOK_ALIGN_LOG2), TOK_ALIGN)
        return pltpu.make_async_copy(tok_hbm.at[pl.ds(base, TOK_WINDOW)], win.at[s], wsem.at[s])

    def gather(blk, s):
        first = jnp.bitwise_and(bs_ref[blk], TOK_ALIGN - 1)

        def body(r, c):
            tok = win[s, first + r]
            pltpu.make_async_copy(x_hbm.at[pl.ds(tok, 1)], xbuf.at[s, pl.ds(r, 1)], xsem.at[s]).start()
            return c
        lax.fori_loop(0, EXPERT_TILE, body, 0, unroll=8)

    @pl.when(jnp.logical_and(i == 0, n_used > 0))
    def _():
        window(0, 0).start()
        window(0, 0).wait()
        gather(0, 0)

        @pl.when(n_used > 1)
        def _():
            window(1, 1).start()

    @pl.when(i + 1 < n_used)
    def _():
        window(i + 1, 1 - slot).wait()
        gather(i + 1, 1 - slot)

    @pl.when(i + 2 < n_used)
    def _():
        window(i + 2, slot).start()

    @pl.when(i < n_used)
    def _():
        e = be_ref[i]
        prev = be_ref[jnp.maximum(i - 1, 0)]

        @pl.when(jnp.logical_or(i == 0, e != prev))
        def _():
            w13b[...] = w13_ref[0, 0].astype(BF16)
            w2b[...] = w2_ref[0, 0].astype(BF16)

        pltpu.make_async_copy(x_hbm.at[pl.ds(0, EXPERT_TILE)], xbuf.at[slot], xsem.at[slot]).wait()
        x = xbuf[slot].astype(BF16)
        y_ref[...] = _swiglu(x, w13b[...], w2b[...])

    @pl.when(i >= n_used)
    def _():
        y_ref[...] = jnp.zeros_like(y_ref)


def _routed_experts(h, sorted_tok, block_e, block_start, n_used, w13, w2, layer):
    T, D = h.shape
    nb = block_e.shape[0]
    _, E, _, F2 = w13.shape
    F = F2 // 2
    grid_spec = pltpu.PrefetchScalarGridSpec(
        num_scalar_prefetch=3,
        grid=(nb,),
        in_specs=[
            pl.BlockSpec(memory_space=pl.ANY),
            pl.BlockSpec(memory_space=pl.ANY),
            pl.BlockSpec((1, 1, D, F2), lambda i, be, bs, nu: (layer, be[i], 0, 0)),
            pl.BlockSpec((1, 1, F, D), lambda i, be, bs, nu: (layer, be[i], 0, 0)),
        ],
        out_specs=pl.BlockSpec((EXPERT_TILE, D), lambda i, be, bs, nu: (i, 0)),
        scratch_shapes=[pltpu.SMEM((2, TOK_WINDOW), jnp.int32),
                        pltpu.SemaphoreType.DMA((2,)),
                        pltpu.VMEM((2, EXPERT_TILE, D), F32),
                        pltpu.SemaphoreType.DMA((2,)),
                        pltpu.VMEM((D, F2), BF16),
                        pltpu.VMEM((F, D), BF16)],
    )
    return pl.pallas_call(
        _moe_kernel,
        grid_spec=grid_spec,
        out_shape=jax.ShapeDtypeStruct((nb * EXPERT_TILE, D), F32),
        compiler_params=_cparams(("arbitrary",)),
        name="routed_experts",
    )(block_e, block_start, n_used, sorted_tok, h, w13, w2)


def _combine_kernel(pos_ref, posn_ref, y_hbm, gate_ref, sh_ref, h_ref, g_ref, b_ref,
                    o_ref, ob_ref, gbuf, sem, *, alpha, n_steps):
    i = pl.program_id(0)
    slot = lax.rem(i, 2)
    n_rows = TOP_K * COMBINE_TILE

    def copy(pos, s, r):
        return pltpu.make_async_copy(y_hbm.at[pl.ds(pos, 1)], gbuf.at[s, pl.ds(r, 1)], sem.at[s])

    def gather(pref, s):
        def body(r, c):
            copy(pref[0, 0, r], s, r).start()
            return c
        lax.fori_loop(0, n_rows, body, 0, unroll=8)

    @pl.when(i == 0)
    def _():
        gather(pos_ref, 0)

    @pl.when(i + 1 < n_steps)
    def _():
        gather(posn_ref, 1 - slot)

    pltpu.make_async_copy(y_hbm.at[pl.ds(0, n_rows)], gbuf.at[slot], sem.at[slot]).wait()

    gate = gate_ref[...]
    acc = alpha * h_ref[...] + sh_ref[...]
    for kk in range(TOP_K):
        rows = gbuf[slot, kk * COMBINE_TILE:(kk + 1) * COMBINE_TILE, :]
        acc = acc + gate[:, kk:kk + 1] * rows
    y = _layer_norm(acc, g_ref[...], b_ref[...])
    o_ref[...] = y
    ob_ref[...] = y.astype(BF16)


def _combine_ln(pos_blocks, y, gate, shared, h, g, b, alpha):
    T, D = h.shape
    n_steps = T // COMBINE_TILE
    n_rows = TOP_K * COMBINE_TILE
    row = lambda n: pl.BlockSpec((COMBINE_TILE, n), lambda i: (i, 0))
    vec = pl.BlockSpec((1, D), lambda i: (0, 0))
    return pl.pallas_call(
        functools.partial(_combine_kernel, alpha=alpha, n_steps=n_steps),
        grid=(n_steps,),
        in_specs=[
            pl.BlockSpec((1, 1, n_rows), lambda i: (i, 0, 0), memory_space=pltpu.SMEM),
            pl.BlockSpec((1, 1, n_rows), lambda i: (jnp.minimum(i + 1, n_steps - 1), 0, 0),
                         memory_space=pltpu.SMEM),
            pl.BlockSpec(memory_space=pl.ANY),
            row(BLK), row(D), row(D), vec, vec],
        out_specs=[row(D), row(D)],
        out_shape=[jax.ShapeDtypeStruct((T, D), F32), jax.ShapeDtypeStruct((T, D), BF16)],
        scratch_shapes=[pltpu.VMEM((2, n_rows, D), F32), pltpu.SemaphoreType.DMA((2,))],
        compiler_params=_cparams(("arbitrary",)),
        name="combine_ln",
    )(pos_blocks, pos_blocks, y, gate, shared, h, g.reshape(1, D), b.reshape(1, D))


def _arrange_in_proj(w_in, b_igate, b_fgate):
    o_v = 2 * D_ML
    o_o = o_v + D_ML
    o_i = o_o + D_ML
    o_f = o_i + ML_HEADS
    o_sb = o_f + ML_HEADS
    w_main = jnp.concatenate([w_in[:, :o_i], w_in[:, o_sb:]], axis=1).astype(BF16)
    D = w_in.shape[0]
    wg = jnp.zeros((D, ML_GROUPS, BLK), F32)
    bias = jnp.zeros((ML_GROUPS, BLK), F32)
    w_i = w_in[:, o_i:o_f].reshape(D, ML_GROUPS, ML_HPS)
    w_f = w_in[:, o_f:o_sb].reshape(D, ML_GROUPS, ML_HPS)
    wg = wg.at[:, :, :ML_HPS].set(w_i).at[:, :, ML_HPS:2 * ML_HPS].set(w_f)
    bias = bias.at[:, :ML_HPS].set(b_igate.reshape(ML_GROUPS, ML_HPS))
    bias = bias.at[:, ML_HPS:2 * ML_HPS].set(b_fgate.reshape(ML_GROUPS, ML_HPS))
    return w_main, wg.reshape(D, ML_GROUPS * BLK).astype(BF16), bias.reshape(1, ML_GROUPS * BLK)


def _dispatch_plan(idx, rank, counts, B, Lp):
    T = B * Lp
    A = T * TOP_K
    A_real = B * (Lp - PAD) * TOP_K
    nb = -(-A_real // EXPERT_TILE) + N_EXPERTS
    E = N_EXPERTS
    eids = jnp.arange(E, dtype=jnp.int32)
    padded = (counts + EXPERT_TILE - 1) // EXPERT_TILE * EXPERT_TILE
    ends = jnp.cumsum(padded)
    poff = ends - padded
    off = jnp.cumsum(counts) - counts
    idx_k = idx[:, :, :TOP_K]
    onehot = idx_k[..., None] == eids
    pos = jnp.sum(jnp.where(onehot, poff, 0), axis=-1) + rank[:, :, :TOP_K]
    real = (jnp.arange(Lp) >= PAD)[None, :, None]
    pos = jnp.where(real, pos, 0).astype(jnp.int32)
    keys = jnp.where(real, idx_k, E).reshape(A)
    tok = jnp.broadcast_to(jnp.arange(T, dtype=jnp.int32).reshape(B, Lp, 1), idx_k.shape).reshape(A)
    _, sorted_tok = lax.sort((keys, tok), num_keys=1, is_stable=True)
    n_tok = -(-A // TOK_ALIGN) * TOK_ALIGN + TOK_WINDOW
    sorted_tok = jnp.concatenate([sorted_tok, jnp.full((n_tok - A,), PAD, jnp.int32)])
    blk0 = jnp.arange(nb, dtype=jnp.int32) * EXPERT_TILE
    block_e = jnp.minimum(jnp.sum(ends[None, :] <= blk0[:, None], axis=1), E - 1).astype(jnp.int32)
    sel = block_e[:, None] == eids
    block_start = jnp.sum(jnp.where(sel, off - poff, 0), axis=1) + blk0
    block_start = jnp.clip(block_start, 0, A).astype(jnp.int32)
    n_used = (ends[-1] // EXPERT_TILE).astype(jnp.int32).reshape(1)
    return pos.reshape(T, TOP_K), sorted_tok, block_e, block_start, n_used


def kernel(x, meta, ln_in_g, ln_in_b, w_in, b_igate, b_fgate, conv_qk, ml_norm_g, sb_norm_g, pool_w, pool_scale, w_out, ln1_g, ln1_b, router_w, router_bias, exp_w13, exp_w2, sh_w13, sh_w2, ln2_g, ln2_b):
    B, S, D = x.shape
    depth = w_in.shape[0]
    alpha = (2 * depth) ** 0.25
    Lp = PAD + N_META + S
    T = B * Lp
    assert Lp % BLK == 0 and T % ROW_TILE == 0 and T % COMBINE_TILE == 0

    h0 = jnp.concatenate([jnp.zeros((B, PAD, D), x.dtype),
                          jnp.broadcast_to(meta[None].astype(x.dtype), (B, N_META, D)), x], axis=1)
    h, hb = _ln_rows(h0.reshape(T, D), ln_in_g, ln_in_b)

    for l in range(depth):
        w_main, w_gate, gate_bias = _arrange_in_proj(w_in[l], b_igate[l], b_fgate[l])
        proj3 = _in_proj(hb, w_main).reshape(B, Lp, N_MAIN)
        gcol = _gates(hb.reshape(B, Lp, D), w_gate, gate_bias)
        grow = jnp.transpose(gcol.reshape(B, Lp, ML_GROUPS, BLK)[..., :8], (0, 2, 3, 1))
        qk3 = _conv_silu(proj3, conv_qk[l])
        a_ml = _mlstm(qk3, proj3, gcol, grow, ml_norm_g[l])
        a_sb = _stick_breaking(proj3, sb_norm_g[l])
        a_pool = _pool(proj3, pool_w[l], pool_scale[l])
        h1 = _outproj_ln(a_ml.reshape(T, D_ML), a_sb.reshape(T, D_SB), a_pool.reshape(T, D_POOL),
                         w_out[l].astype(BF16), h, ln1_g[l], ln1_b[l], alpha)

        idx, gate, rank, counts = _router(h1.reshape(B, Lp, D), router_w[l], router_bias[l])
        pos, sorted_tok, block_e, block_start, n_used = _dispatch_plan(idx, rank, counts[0], B, Lp)
        y = _routed_experts(h1, sorted_tok, block_e, block_start, n_used, exp_w13, exp_w2, l)
        shared = _shared_expert(h1, sh_w13[l].astype(BF16), sh_w2[l].astype(BF16))
        n_steps = T // COMBINE_TILE
        pos_blocks = jnp.transpose(pos.reshape(n_steps, COMBINE_TILE, TOP_K), (0, 2, 1))
        pos_blocks = pos_blocks.reshape(n_steps, 1, TOP_K * COMBINE_TILE)
        h, hb = _combine_ln(pos_blocks, y, gate.reshape(T, BLK), shared, h1,
                            ln2_g[l], ln2_b[l], alpha)

    return h.reshape(B, Lp, D)[:, PAD + N_META:]
```

```python
import functools

import jax
import jax.numpy as jnp
from jax import lax
from jax.experimental import pallas as pl
from jax.experimental.pallas import tpu as pltpu

N_META = 16
HEAD_DIM = 128
ML_HEADS = 6
SB_HEADS = 6
POOL_WINDOWS = (2, 4, 8, 16)
POOL_GROUPS = 4
D_ML = ML_HEADS * HEAD_DIM
D_SB = SB_HEADS * HEAD_DIM
D_POOL = POOL_GROUPS * HEAD_DIM
CONV_W = 4
N_EXPERTS = 64
TOP_K = 8
ROUTED_SCALE = 2.5
LN_EPS = 1e-5
RMS_EPS = 1e-6

BLK = 128
PAD = (-N_META) % BLK
ML_HPS = 2
ML_GROUPS = ML_HEADS // ML_HPS
SB_KEY_BLOCKS = 4
SB_HPS = 2
SB_W = SB_HPS * HEAD_DIM
ROW_TILE = 512
EXPERT_TILE = 256
COMBINE_TILE = 128
TOK_ALIGN_LOG2 = 10
TOK_ALIGN = 1 << TOK_ALIGN_LOG2
TOK_WINDOW = 2 * TOK_ALIGN
NEG = -1e30
VMEM_LIMIT = 56 * 1024 * 1024

C_QK = 0
C_V = 2 * D_ML
C_O = C_V + D_ML
C_SBQ = C_O + D_ML
C_SBK = C_SBQ + D_SB
C_SBV = C_SBK + D_SB
C_POOL = C_SBV + D_SB
N_MAIN = C_POOL + D_POOL
O_SB = C_SBQ + 2 * ML_HEADS
XROWS = 16

F32 = jnp.float32
BF16 = jnp.bfloat16


def _cparams(sem, vmem=VMEM_LIMIT):
    return pltpu.CompilerParams(dimension_semantics=sem, vmem_limit_bytes=vmem)


def _layer_norm(x, g, b):
    mu = jnp.mean(x, axis=-1, keepdims=True)
    xc = x - mu
    var = jnp.mean(xc * xc, axis=-1, keepdims=True)
    return xc * lax.rsqrt(var + LN_EPS) * g + b


def _sigmoid(x):
    return 1.0 / (1.0 + jnp.exp(-x))


def _softplus(x):
    return jnp.maximum(x, 0.0) + jnp.log(1.0 + jnp.exp(-jnp.abs(x)))


def _split_dot(a, b, split_lhs):
    x = a if split_lhs else b
    out = None
    for _ in range(3):
        part = x.astype(BF16)
        x = x - part.astype(F32)
        term = (jnp.dot(part, b, preferred_element_type=F32) if split_lhs
                else jnp.dot(a, part, preferred_element_type=F32))
        out = term if out is None else out + term
    return out


def _ln_kernel(x_ref, g_ref, b_ref, o_ref, ob_ref):
    y = _layer_norm(x_ref[...], g_ref[...], b_ref[...])
    o_ref[...] = y
    ob_ref[...] = y.astype(BF16)


def _ln_rows(x, g, b):
    T, D = x.shape
    row = pl.BlockSpec((ROW_TILE, D), lambda i: (i, 0))
    vec = pl.BlockSpec((1, D), lambda i: (0, 0))
    return pl.pallas_call(
        _ln_kernel,
        grid=(T // ROW_TILE,),
        in_specs=[row, vec, vec],
        out_specs=[row, row],
        out_shape=[jax.ShapeDtypeStruct((T, D), F32), jax.ShapeDtypeStruct((T, D), BF16)],
        compiler_params=_cparams(("parallel",)),
        name="ln_in",
    )(x, g.reshape(1, D), b.reshape(1, D))


def _mm_kernel(a_ref, w_ref, o_ref):
    o_ref[...] = jnp.dot(a_ref[...], w_ref[...], preferred_element_type=F32).astype(o_ref.dtype)


def _in_proj(a, w):
    M, K = a.shape
    N = w.shape[1]
    tn = N // 2
    return pl.pallas_call(
        _mm_kernel,
        grid=(N // tn, M // ROW_TILE),
        in_specs=[pl.BlockSpec((ROW_TILE, K), lambda j, i: (i, 0)),
                  pl.BlockSpec((K, tn), lambda j, i: (0, j))],
        out_specs=pl.BlockSpec((ROW_TILE, tn), lambda j, i: (i, j)),
        out_shape=jax.ShapeDtypeStruct((M, N), BF16),
        compiler_params=_cparams(("parallel", "parallel")),
        name="in_proj",
    )(a, w)


def _gates_kernel(hb_ref, wg_ref, bias_ref, o_ref, *, tm):
    j = pl.program_id(1)
    g = jnp.dot(hb_ref[0], wg_ref[...], preferred_element_type=F32) + bias_ref[...]
    col = jnp.bitwise_and(lax.broadcasted_iota(jnp.int32, g.shape, 1), BLK - 1)
    row = lax.broadcasted_iota(jnp.int32, g.shape, 0) + j * tm
    is_f = jnp.logical_and(col >= ML_HPS, col < 2 * ML_HPS)
    val = jnp.where(is_f, -_softplus(-g), g)
    pad_val = jnp.where(is_f, 0.0, NEG)
    o_ref[0] = jnp.where(row < PAD, pad_val, val)


def _gates(hb3, wg, bias):
    B, Lp, D = hb3.shape
    tm = Lp // 4
    Ng = wg.shape[1]
    return pl.pallas_call(
        functools.partial(_gates_kernel, tm=tm),
        grid=(B, Lp // tm),
        in_specs=[pl.BlockSpec((1, tm, D), lambda b, j: (b, j, 0)),
                  pl.BlockSpec((D, Ng), lambda b, j: (0, 0)),
                  pl.BlockSpec((1, Ng), lambda b, j: (0, 0))],
        out_specs=pl.BlockSpec((1, tm, Ng), lambda b, j: (b, j, 0)),
        out_shape=jax.ShapeDtypeStruct((B, Lp, Ng), F32),
        compiler_params=_cparams(("parallel", "parallel")),
        name="ml_gates",
    )(hb3, wg, bias)


def _conv_kernel(x_ref, w_ref, o_ref):
    c = pl.program_id(1)
    x = x_ref[0].astype(F32)
    row = lax.broadcasted_iota(jnp.int32, x.shape, 0)
    x = jnp.where(row < PAD, 0.0, x)
    w = w_ref[...]
    y = x * w[CONV_W - 1:CONV_W]
    for s in range(1, CONV_W):
        y = y + pltpu.roll(x, s, axis=0) * w[CONV_W - 1 - s:CONV_W - s]
    y = y * _sigmoid(y)
    scale = jnp.where(c < ML_HEADS, HEAD_DIM ** -0.5, 1.0).astype(F32)
    o_ref[0] = (y * scale).astype(BF16)


def _conv_silu(proj3, conv_w):
    B, Lp, _ = proj3.shape
    nc = 2 * D_ML // BLK
    return pl.pallas_call(
        _conv_kernel,
        grid=(B, nc),
        in_specs=[pl.BlockSpec((1, Lp, BLK), lambda b, c: (b, 0, c)),
                  pl.BlockSpec((CONV_W, BLK), lambda b, c: (0, c))],
        out_specs=pl.BlockSpec((1, Lp, BLK), lambda b, c: (b, 0, c)),
        out_shape=jax.ShapeDtypeStruct((B, Lp, 2 * D_ML), BF16),
        compiler_params=_cparams(("parallel", "parallel")),
        name="conv_silu",
    )(proj3, conv_w)


def _mlstm_kernel(q_ref, k_ref, v_ref, o_ref, gc_ref, gr_ref, ng_ref, out_ref, st_ref, m_ref):
    n_chunks = q_ref.shape[1] // BLK
    st_ref[...] = jnp.zeros_like(st_ref)
    m_ref[...] = jnp.zeros_like(m_ref)

    ri = lax.broadcasted_iota(jnp.int32, (BLK, BLK), 0)
    ci = lax.broadcasted_iota(jnp.int32, (BLK, BLK), 1)
    causal = ci <= ri
    lower_incl = jnp.where(causal, 1.0, 0.0).astype(BF16)
    upper_incl = jnp.where(ri <= ci, 1.0, 0.0).astype(BF16)
    ones_col = jnp.where(ci == 0, 1.0, 0.0).astype(BF16)

    def chunk(c, carry):
        r0 = pl.multiple_of(c * BLK, BLK)
        gcol = gc_ref[0, pl.ds(r0, BLK), :]
        grow = gr_ref[0, 0, :, pl.ds(r0, BLK)]
        bcol = _split_dot(lower_incl, gcol, split_lhs=False)
        brow = _split_dot(grow, upper_incl, split_lhs=True)
        for hh in range(ML_HPS):
            cs = slice(hh * HEAD_DIM, (hh + 1) * HEAD_DIM)
            q = q_ref[0, pl.ds(r0, BLK), cs]
            k = k_ref[0, pl.ds(r0, BLK), cs]
            v = v_ref[0, pl.ds(r0, BLK), cs]
            li_c = gcol[:, hh:hh + 1]
            b_c = bcol[:, ML_HPS + hh:ML_HPS + hh + 1]
            li_r = grow[hh:hh + 1, :]
            b_r = brow[ML_HPS + hh:ML_HPS + hh + 1, :]
            m = m_ref[hh, 0:1, 0:1]
            st = st_ref[hh]

            inter = b_c + m
            dmat = jnp.where(causal, b_c - b_r + li_r, NEG)
            m_t = jnp.maximum(inter, jnp.max(dmat, axis=-1, keepdims=True))
            w_inter = jnp.exp(inter - m_t)
            s_qk = lax.dot_general(q, k, (((1,), (1,)), ((), ())), preferred_element_type=F32)
            ws = jnp.exp(dmat - m_t) * s_qk
            v_aug = jnp.concatenate([v, ones_col], axis=1)
            nd = (w_inter * jnp.dot(q, st.astype(BF16), preferred_element_type=F32)
                  + jnp.dot(ws.astype(BF16), v_aug, preferred_element_type=F32))
            num = nd[:, :HEAD_DIM]
            den = nd[:, HEAD_DIM:HEAD_DIM + 1]
            h = num / jnp.maximum(jnp.abs(den), jnp.exp(-m_t))

            og = o_ref[0, pl.ds(r0, BLK), cs].astype(F32)
            hn = h * lax.rsqrt(jnp.mean(h * h, axis=-1, keepdims=True) + RMS_EPS)
            hn = hn * ng_ref[:, cs] * _sigmoid(og)
            out_ref[0, pl.ds(r0, BLK), cs] = hn.astype(BF16)

            b_last = b_r[:, BLK - 1:BLK]
            g_r = b_last - b_r + li_r
            m_new = jnp.maximum(b_last + m, jnp.max(g_r, axis=-1, keepdims=True))
            wg = jnp.exp(b_last - b_c + li_c - m_new)
            decay = jnp.exp(b_last + m - m_new)
            wv = (wg * v_aug.astype(F32)).astype(BF16)
            kv = lax.dot_general(k, wv, (((0,), (0,)), ((), ())), preferred_element_type=F32)
            st_ref[hh] = decay * st + kv
            m_ref[hh] = jnp.broadcast_to(m_new, m_ref.shape[1:])
        return carry

    lax.fori_loop(0, n_chunks, chunk, 0)


def _mlstm(qk3, proj3, gcol, grow, norm_g):
    B, Lp, _ = qk3.shape
    W = ML_HPS * HEAD_DIM
    nq = D_ML // W
    seq = lambda off: pl.BlockSpec((1, Lp, W), lambda b, g: (b, 0, off + g))
    return pl.pallas_call(
        _mlstm_kernel,
        grid=(B, ML_GROUPS),
        in_specs=[seq(0), seq(nq),
                  pl.BlockSpec((1, Lp, W), lambda b, g: (b, 0, C_V // W + g)),
                  pl.BlockSpec((1, Lp, W), lambda b, g: (b, 0, C_O // W + g)),
                  pl.BlockSpec((1, Lp, BLK), lambda b, g: (b, 0, g)),
                  pl.BlockSpec((1, 1, 8, Lp), lambda b, g: (b, g, 0, 0)),
                  pl.BlockSpec((1, W), lambda b, g: (0, g))],
        out_specs=pl.BlockSpec((1, Lp, W), lambda b, g: (b, 0, g)),
        out_shape=jax.ShapeDtypeStruct((B, Lp, D_ML), BF16),
        scratch_shapes=[pltpu.VMEM((ML_HPS, HEAD_DIM, 2 * HEAD_DIM), F32),
                        pltpu.VMEM((ML_HPS, 8, BLK), F32)],
        compiler_params=_cparams(("parallel", "parallel")),
        name="mlstm",
    )(qk3, qk3, proj3, proj3, gcol, grow, norm_g.reshape(1, D_ML))


def _sb_kernel(q_ref, k_ref, v_ref, g_ref, o_ref):
    i = pl.program_id(2)
    W = SB_KEY_BLOCKS * BLK
    r1 = lax.broadcasted_iota(jnp.int32, (BLK, BLK), 0)
    c1 = lax.broadcasted_iota(jnp.int32, (BLK, BLK), 1)
    upper_strict = jnp.where(r1 > c1, 1.0, 0.0).astype(BF16)
    rowp = lax.broadcasted_iota(jnp.int32, (BLK, W), 0) + i * BLK
    ci = lax.broadcasted_iota(jnp.int32, (BLK, W), 1)
    scale = HEAD_DIM ** -0.5
    heads = [slice(hh * HEAD_DIM, (hh + 1) * HEAD_DIM) for hh in range(SB_HPS)]

    def one_head(q, k, v, carry, acc, valid):
        z = lax.dot_general(q, k, (((1,), (1,)), ((), ())), preferred_element_type=F32) * scale
        sp = _softplus(z)
        lk = jnp.where(valid, -sp, 0.0)
        hi = lk.astype(BF16)
        lo = (lk - hi.astype(F32)).astype(BF16)
        pieces = [None] * SB_KEY_BLOCKS
        off = carry
        for c in reversed(range(SB_KEY_BLOCKS)):
            sl = slice(c * BLK, (c + 1) * BLK)
            local = (jnp.dot(hi[:, sl], upper_strict, preferred_element_type=F32)
                     + jnp.dot(lo[:, sl], upper_strict, preferred_element_type=F32))
            pieces[c] = local + off
            off = off + jnp.sum(lk[:, sl], axis=-1, keepdims=True)
        after = jnp.concatenate(pieces, axis=1)
        a = jnp.where(valid, jnp.exp(z - sp + after), 0.0)
        return off, acc + jnp.dot(a.astype(BF16), v, preferred_element_type=F32)

    def step(t, state):
        end = (i + 1) * BLK - t * W
        s0 = pl.multiple_of(jnp.maximum(end - W, 0), BLK)
        colp = ci + s0
        valid = jnp.logical_and(colp < jnp.minimum(rowp, end), colp >= PAD)
        return tuple(one_head(q_ref[0, :, hs], k_ref[0, pl.ds(s0, W), hs], v_ref[0, pl.ds(s0, W), hs],
                              carry, acc, valid)
                     for hs, (carry, acc) in zip(heads, state))

    init = tuple((jnp.zeros((BLK, 1), F32), jnp.zeros((BLK, HEAD_DIM), F32)) for _ in heads)
    n_steps = (i + SB_KEY_BLOCKS) // SB_KEY_BLOCKS
    final = lax.fori_loop(0, n_steps, step, init)
    for hs, (_, acc) in zip(heads, final):
        hn = acc * lax.rsqrt(jnp.mean(acc * acc, axis=-1, keepdims=True) + RMS_EPS) * g_ref[:, hs]
        o_ref[0, :, hs] = hn.astype(BF16)


def _stick_breaking(proj3, norm_g):
    B, Lp, _ = proj3.shape
    nq = Lp // BLK
    return pl.pallas_call(
        _sb_kernel,
        grid=(B, SB_HEADS // SB_HPS, nq),
        in_specs=[pl.BlockSpec((1, BLK, SB_W), lambda b, h, i: (b, i, C_SBQ // SB_W + h)),
                  pl.BlockSpec((1, Lp, SB_W), lambda b, h, i: (b, 0, C_SBK // SB_W + h)),
                  pl.BlockSpec((1, Lp, SB_W), lambda b, h, i: (b, 0, C_SBV // SB_W + h)),
                  pl.BlockSpec((1, SB_W), lambda b, h, i: (0, h))],
        out_specs=pl.BlockSpec((1, BLK, SB_W), lambda b, h, i: (b, i, h)),
        out_shape=jax.ShapeDtypeStruct((B, Lp, D_SB), BF16),
        compiler_params=_cparams(("parallel", "parallel", "parallel")),
        name="stick_breaking",
    )(proj3, proj3, proj3, norm_g.reshape(1, D_SB))


def _pool_kernel(x_ref, w_ref, s_ref, o_ref):
    g = pl.program_id(1)
    x = x_ref[0].astype(F32)
    row = lax.broadcasted_iota(jnp.int32, x.shape, 0)
    x = jnp.where(row < PAD, 0.0, x)
    pos1 = (row - PAD + 1).astype(F32)
    sums = x
    mean = jnp.zeros_like(x)
    shift = 1
    for gi, win in enumerate(POOL_WINDOWS):
        while shift < win:
            sums = sums + pltpu.roll(sums, shift, axis=0)
            shift *= 2
        cnt = jnp.maximum(jnp.minimum(pos1, float(win)), 1.0)
        mean = jnp.where(g == gi, sums / cnt, mean)
    p = (mean - x).astype(BF16)
    y = jnp.dot(p, w_ref[0].astype(BF16), preferred_element_type=F32) * s_ref[...]
    o_ref[0] = y.astype(BF16)


def _pool(proj3, pool_w, pool_scale):
    B, Lp, _ = proj3.shape
    return pl.pallas_call(
        _pool_kernel,
        grid=(B, POOL_GROUPS),
        in_specs=[pl.BlockSpec((1, Lp, HEAD_DIM), lambda b, g: (b, 0, C_POOL // HEAD_DIM + g)),
                  pl.BlockSpec((1, HEAD_DIM, HEAD_DIM), lambda b, g: (g, 0, 0)),
                  pl.BlockSpec((1, HEAD_DIM), lambda b, g: (0, g))],
        out_specs=pl.BlockSpec((1, Lp, HEAD_DIM), lambda b, g: (b, 0, g)),
        out_shape=jax.ShapeDtypeStruct((B, Lp, D_POOL), BF16),
        compiler_params=_cparams(("parallel", "parallel")),
        name="pool_mixer",
    )(proj3, pool_w, pool_scale.reshape(1, D_POOL))


def _to_row_tiles(ref, y):
    for s in range(XROWS):
        ref[:, s, :] = y[:, s * BLK:(s + 1) * BLK]


def _from_row_tiles(ref, *lead, rows=slice(None)):
    idx = lambda s: lead + (rows, s, slice(None))
    return jnp.concatenate([ref[idx(s)] for s in range(XROWS)], axis=1)


def _outproj_kernel(a1_ref, a2_ref, a3_ref, w_ref, h_ref, g_ref, b_ref, o_ref, ot_ref, *, alpha):
    acc = jnp.dot(a1_ref[...], w_ref[0:D_ML, :], preferred_element_type=F32)
    acc = acc + jnp.dot(a2_ref[...], w_ref[D_ML:D_ML + D_SB, :], preferred_element_type=F32)
    acc = acc + jnp.dot(a3_ref[...], w_ref[D_ML + D_SB:, :], preferred_element_type=F32)
    y = _layer_norm(alpha * h_ref[...] + acc, g_ref[...], b_ref[...])
    o_ref[...] = y
    _to_row_tiles(ot_ref, y)


def _outproj_ln(a_ml, a_sb, a_pool, w, h, g, b, alpha):
    T, D = h.shape
    row = lambda n: pl.BlockSpec((ROW_TILE, n), lambda i: (i, 0))
    vec = pl.BlockSpec((1, D), lambda i: (0, 0))
    return pl.pallas_call(
        functools.partial(_outproj_kernel, alpha=alpha),
        grid=(T // ROW_TILE,),
        in_specs=[row(D_ML), row(D_SB), row(D_POOL),
                  pl.BlockSpec(w.shape, lambda i: (0, 0)), row(D), vec, vec],
        out_specs=[row(D), pl.BlockSpec((ROW_TILE, XROWS, BLK), lambda i: (i, 0, 0))],
        out_shape=[jax.ShapeDtypeStruct((T, D), F32), jax.ShapeDtypeStruct((T, XROWS, BLK), F32)],
        compiler_params=_cparams(("parallel",)),
        name="out_proj_ln",
    )(a_ml, a_sb, a_pool, w, h, g.reshape(1, D), b.reshape(1, D))


def _router_kernel(h_ref, rw_ref, rb_ref, idx_ref, gate_ref, rank_ref, cnt_ref, run_ref, *, tm):
    first = jnp.logical_and(pl.program_id(0) == 0, pl.program_id(1) == 0)

    @pl.when(first)
    def _():
        run_ref[...] = jnp.zeros_like(run_ref)

    x = h_ref[0]
    logits = jnp.dot(x, rw_ref[...], preferred_element_type=F32, precision=lax.Precision.HIGHEST)
    scores = _sigmoid(logits)
    work = scores + rb_ref[...]
    E = scores.shape[1]
    lane = lax.broadcasted_iota(jnp.int32, (tm, E), 1).astype(F32)
    row = lax.broadcasted_iota(jnp.int32, (tm, 1), 0) + pl.program_id(1) * tm
    real = (row >= PAD).astype(F32)

    picks = []
    gates = []
    member = jnp.zeros((tm, E), F32)
    for _ in range(TOP_K):
        mx = jnp.max(work, axis=-1, keepdims=True)
        pick = jnp.min(jnp.where(work == mx, lane, float(E)), axis=-1, keepdims=True)
        onehot = lane == pick
        gates.append(jnp.sum(jnp.where(onehot, scores, 0.0), axis=-1, keepdims=True))
        picks.append(pick)
        member = member + jnp.where(onehot, real, 0.0)
        work = jnp.where(onehot, -jnp.inf, work)
    gsum = gates[0]
    for gk in gates[1:]:
        gsum = gsum + gk

    ri = lax.broadcasted_iota(jnp.int32, (tm, tm), 0)
    ci = lax.broadcasted_iota(jnp.int32, (tm, tm), 1)
    lower_strict = jnp.where(ci < ri, 1.0, 0.0).astype(BF16)
    before = jnp.dot(lower_strict, member.astype(BF16), preferred_element_type=F32) + run_ref[...]

    lane_o = lax.broadcasted_iota(jnp.int32, (tm, BLK), 1)
    idx_o = jnp.zeros((tm, BLK), F32)
    gate_o = jnp.zeros((tm, BLK), F32)
    rank_o = jnp.zeros((tm, BLK), F32)
    for kk in range(TOP_K):
        onehot = lane == picks[kk]
        rk = jnp.sum(jnp.where(onehot, before, 0.0), axis=-1, keepdims=True)
        sel = lane_o == kk
        idx_o = jnp.where(sel, picks[kk], idx_o)
        gate_o = jnp.where(sel, gates[kk] / gsum * ROUTED_SCALE, gate_o)
        rank_o = jnp.where(sel, rk, rank_o)
    idx_ref[0] = idx_o.astype(jnp.int32)
    gate_ref[0] = gate_o
    rank_ref[0] = rank_o.astype(jnp.int32)
    run_ref[...] = run_ref[...] + jnp.sum(member, axis=0, keepdims=True)
    cnt_ref[...] = run_ref[...].astype(jnp.int32)


def _router(h3, router_w, router_bias):
    B, Lp, D = h3.shape
    E = router_w.shape[1]
    tm = Lp // 4
    out = lambda: pl.BlockSpec((1, tm, BLK), lambda b, j: (b, j, 0))
    return pl.pallas_call(
        functools.partial(_router_kernel, tm=tm),
        grid=(B, Lp // tm),
        in_specs=[pl.BlockSpec((1, tm, D), lambda b, j: (b, j, 0)),
                  pl.BlockSpec((D, E), lambda b, j: (0, 0)),
                  pl.BlockSpec((1, E), lambda b, j: (0, 0))],
        out_specs=[out(), out(), out(), pl.BlockSpec((1, E), lambda b, j: (0, 0))],
        out_shape=[jax.ShapeDtypeStruct((B, Lp, BLK), jnp.int32),
                   jax.ShapeDtypeStruct((B, Lp, BLK), F32),
                   jax.ShapeDtypeStruct((B, Lp, BLK), jnp.int32),
                   jax.ShapeDtypeStruct((1, E), jnp.int32)],
        scratch_shapes=[pltpu.VMEM((1, E), F32)],
        compiler_params=_cparams(("arbitrary", "arbitrary")),
        name="router",
    )(h3, router_w, router_bias.reshape(1, E))


def _swiglu(x, w13, w2):
    h1 = jnp.dot(x, w13, preferred_element_type=F32)
    f = h1.shape[1] // 2
    gate, up = h1[:, :f], h1[:, f:]
    act = (gate * _sigmoid(gate) * up).astype(BF16)
    return jnp.dot(act, w2, preferred_element_type=F32)


def _shared_kernel(x_ref, w13_ref, w2_ref, o_ref):
    o_ref[...] = _swiglu(x_ref[...].astype(BF16), w13_ref[...], w2_ref[...])


def _shared_expert(h, w13, w2):
    T, D = h.shape
    row = pl.BlockSpec((ROW_TILE, D), lambda i: (i, 0))
    return pl.pallas_call(
        _shared_kernel,
        grid=(T // ROW_TILE,),
        in_specs=[row, pl.BlockSpec(w13.shape, lambda i: (0, 0)),
                  pl.BlockSpec(w2.shape, lambda i: (0, 0))],
        out_specs=row,
        out_shape=jax.ShapeDtypeStruct((T, D), F32),
        compiler_params=_cparams(("parallel",)),
        name="shared_expert",
    )(h, w13, w2)


def _moe_kernel(be_ref, bs_ref, bo_ref, bn_ref, nb_ref, tok_hbm, x_hbm, w13_hbm, w2_hbm, y_ref,
                win, wsem, xbuf, xsem, wf13, wf2, fsem, w13b, w2b, *, layer):
    i = pl.program_id(0)
    n_used = nb_ref[0]
    slot = lax.rem(i, 2)
    F = w2b.shape[0]

    def window(blk, s):
        base = pl.multiple_of(lax.shift_left(lax.shift_right_logical(bs_ref[blk], TOK_ALIGN_LOG2),
                                             TOK_ALIGN_LOG2), TOK_ALIGN)
        return pltpu.make_async_copy(tok_hbm.at[pl.ds(base, TOK_WINDOW)], win.at[s], wsem.at[s])

    def weights(e, s):
        return (pltpu.make_async_copy(w13_hbm.at[layer, e], wf13.at[s], fsem.at[0, s]),
                pltpu.make_async_copy(w2_hbm.at[layer, e], wf2.at[s], fsem.at[1, s]))

    def row_copy(tok, s, r):
        return pltpu.make_async_copy(x_hbm.at[tok], xbuf.at[s, r], xsem.at[s])

    def rows_wait(s):
        pltpu.make_async_copy(x_hbm.at[pl.ds(0, EXPERT_TILE)], xbuf.at[s], xsem.at[s]).wait()

    @pl.when(jnp.logical_and(i == 0, n_used > 0))
    def _():
        for c in weights(be_ref[0], 0):
            c.start()
        window(0, 0).start()
        window(0, 0).wait()
        first0 = jnp.bitwise_and(bs_ref[0], TOK_ALIGN - 1)

        def body(r, c):
            row_copy(win[0, first0 + r], 0, r).start()
            return c
        lax.fori_loop(0, EXPERT_TILE, body, 0)

        @pl.when(n_used > 1)
        def _():
            window(1, 1).start()

    @pl.when(i + 1 < n_used)
    def _():
        window(i + 1, 1 - slot).wait()

    @pl.when(i + 2 < n_used)
    def _():
        window(i + 2, slot).start()

    @pl.when(i < n_used)
    def _():
        e = be_ref[i]
        ws = lax.rem(bo_ref[i], 2)

        @pl.when(jnp.logical_or(i == 0, e != be_ref[jnp.maximum(i - 1, 0)]))
        def _():
            for c in weights(e, ws):
                c.wait()
            nxt = bn_ref[i]

            @pl.when(nxt < n_used)
            def _():
                for c in weights(be_ref[nxt], 1 - ws):
                    c.start()
            w13b[...] = wf13[ws].astype(BF16)
            w2b[...] = wf2[ws].astype(BF16)

        nblk = jnp.minimum(i + 1, n_used - 1)
        nwin = lax.rem(nblk, 2)
        first = jnp.bitwise_and(bs_ref[nblk], TOK_ALIGN - 1)
        groups = 2 + XROWS // 4
        per = EXPERT_TILE // groups

        def issue(g):
            hi = EXPERT_TILE if g == groups - 1 else (g + 1) * per
            for r in range(g * per, hi):
                row_copy(win[nwin, first + r], 1 - slot, r).start(priority=r % 2)

        rows_wait(slot)
        x = _from_row_tiles(xbuf, slot).astype(BF16)
        acts = []
        half = F // 2
        for c in range(2):
            gate = jnp.dot(x, w13b[:, c * half:(c + 1) * half], preferred_element_type=F32)
            up = jnp.dot(x, w13b[:, F + c * half:F + (c + 1) * half], preferred_element_type=F32)
            acts.append((gate * _sigmoid(gate) * up).astype(BF16))
            issue(c)
        act = jnp.concatenate(acts, axis=1)
        for c in range(XROWS // 4):
            y = jnp.dot(act, w2b[:, c * 4 * BLK:(c + 1) * 4 * BLK], preferred_element_type=F32)
            for s in range(4):
                y_ref[:, c * 4 + s, :] = y[:, s * BLK:(s + 1) * BLK]
            issue(2 + c)

        @pl.when(i == n_used - 1)
        def _():
            rows_wait(1 - slot)

    @pl.when(i >= n_used)
    def _():
        y_ref[...] = jnp.zeros_like(y_ref)


def _routed_experts(x_tiles, sorted_tok, block_e, block_start, block_ord, block_next, n_used, w13, w2, layer):
    nb = block_e.shape[0]
    _, E, D, F2 = w13.shape
    F = F2 // 2
    grid_spec = pltpu.PrefetchScalarGridSpec(
        num_scalar_prefetch=5,
        grid=(nb,),
        in_specs=[pl.BlockSpec(memory_space=pl.ANY)] * 4,
        out_specs=pl.BlockSpec((EXPERT_TILE, XROWS, BLK), lambda i, *_: (i, 0, 0)),
        scratch_shapes=[pltpu.SMEM((2, TOK_WINDOW), jnp.int32),
                        pltpu.SemaphoreType.DMA((2,)),
                        pltpu.VMEM((2, EXPERT_TILE, XROWS, BLK), F32),
                        pltpu.SemaphoreType.DMA((2,)),
                        pltpu.VMEM((2, D, F2), F32),
                        pltpu.VMEM((2, F, D), F32),
                        pltpu.SemaphoreType.DMA((2, 2)),
                        pltpu.VMEM((D, F2), BF16),
                        pltpu.VMEM((F, D), BF16)],
    )
    return pl.pallas_call(
        functools.partial(_moe_kernel, layer=layer),
        grid_spec=grid_spec,
        out_shape=jax.ShapeDtypeStruct((nb * EXPERT_TILE, XROWS, BLK), F32),
        compiler_params=_cparams(("arbitrary",)),
        name="routed_experts",
    )(block_e, block_start, block_ord, block_next, n_used, sorted_tok, x_tiles, w13, w2)


def _combine_kernel(pos_ref, posn_ref, y_hbm, gate_ref, sh_ref, h_ref, g_ref, b_ref,
                    o_ref, ob_ref, gbuf, sem, *, alpha, n_steps):
    i = pl.program_id(0)
    slot = lax.rem(i, 2)
    n_rows = TOP_K * COMBINE_TILE

    def copy(pos, s, r):
        return pltpu.make_async_copy(y_hbm.at[pos], gbuf.at[s, r], sem.at[s])

    def gather(pref, s):
        def body(r8, c):
            for u in range(8):
                r = r8 * 8 + u
                copy(pref[0, 0, r], s, r).start(priority=u % 2)
            return c
        lax.fori_loop(0, n_rows // 8, body, 0, unroll=2)

    @pl.when(i == 0)
    def _():
        gather(pos_ref, 0)

    @pl.when(i + 1 < n_steps)
    def _():
        gather(posn_ref, 1 - slot)

    pltpu.make_async_copy(y_hbm.at[pl.ds(0, n_rows)], gbuf.at[slot], sem.at[slot]).wait()

    gate = gate_ref[...]
    acc = alpha * h_ref[...] + sh_ref[...]
    for kk in range(TOP_K):
        rows = _from_row_tiles(gbuf, slot, rows=slice(kk * COMBINE_TILE, (kk + 1) * COMBINE_TILE))
        acc = acc + gate[:, kk:kk + 1] * rows
    y = _layer_norm(acc, g_ref[...], b_ref[...])
    o_ref[...] = y
    ob_ref[...] = y.astype(BF16)


def _combine_ln(pos_blocks, y, gate, shared, h, g, b, alpha):
    T, D = h.shape
    n_steps = T // COMBINE_TILE
    n_rows = TOP_K * COMBINE_TILE
    row = lambda n: pl.BlockSpec((COMBINE_TILE, n), lambda i: (i, 0))
    vec = pl.BlockSpec((1, D), lambda i: (0, 0))
    return pl.pallas_call(
        functools.partial(_combine_kernel, alpha=alpha, n_steps=n_steps),
        grid=(n_steps,),
        in_specs=[
            pl.BlockSpec((1, 1, n_rows), lambda i: (i, 0, 0), memory_space=pltpu.SMEM),
            pl.BlockSpec((1, 1, n_rows), lambda i: (jnp.minimum(i + 1, n_steps - 1), 0, 0),
                         memory_space=pltpu.SMEM),
            pl.BlockSpec(memory_space=pl.ANY),
            row(BLK), row(D), row(D), vec, vec],
        out_specs=[row(D), row(D)],
        out_shape=[jax.ShapeDtypeStruct((T, D), F32), jax.ShapeDtypeStruct((T, D), BF16)],
        scratch_shapes=[pltpu.VMEM((2, n_rows, XROWS, BLK), F32), pltpu.SemaphoreType.DMA((2,))],
        compiler_params=_cparams(("arbitrary",)),
        name="combine_ln",
    )(pos_blocks, pos_blocks, y, gate, shared, h, g.reshape(1, D), b.reshape(1, D))


def _wprep_kernel(w_ref, o_ref):
    o_ref[:, 0:C_SBQ] = w_ref[0, :, 0:C_SBQ].astype(BF16)
    o_ref[:, C_SBQ:N_MAIN] = w_ref[0, :, O_SB:O_SB + N_MAIN - C_SBQ].astype(BF16)


def _prep_in_proj(w_in_all, layer):
    _, D, n_in = w_in_all.shape
    tr = 256
    return pl.pallas_call(
        _wprep_kernel,
        grid=(D // tr,),
        in_specs=[pl.BlockSpec((1, tr, n_in), lambda i: (layer, i, 0))],
        out_specs=pl.BlockSpec((tr, N_MAIN), lambda i: (i, 0)),
        out_shape=jax.ShapeDtypeStruct((D, N_MAIN), BF16),
        compiler_params=_cparams(("parallel",)),
        name="w_in_prep",
    )(w_in_all)


def _arrange_gates(w_gates, b_igate, b_fgate):
    D = w_gates.shape[0]
    wg = jnp.zeros((D, ML_GROUPS, BLK), F32)
    bias = jnp.zeros((ML_GROUPS, BLK), F32)
    w_i = w_gates[:, :ML_HEADS].reshape(D, ML_GROUPS, ML_HPS)
    w_f = w_gates[:, ML_HEADS:].reshape(D, ML_GROUPS, ML_HPS)
    wg = wg.at[:, :, :ML_HPS].set(w_i).at[:, :, ML_HPS:2 * ML_HPS].set(w_f)
    bias = bias.at[:, :ML_HPS].set(b_igate.reshape(ML_GROUPS, ML_HPS))
    bias = bias.at[:, ML_HPS:2 * ML_HPS].set(b_fgate.reshape(ML_GROUPS, ML_HPS))
    return wg.reshape(D, ML_GROUPS * BLK).astype(BF16), bias.reshape(1, ML_GROUPS * BLK)


def _dispatch_plan(idx, rank, counts, B, Lp):
    T = B * Lp
    A = T * TOP_K
    A_real = B * (Lp - PAD) * TOP_K
    nb = -(-A_real // EXPERT_TILE) + N_EXPERTS
    E = N_EXPERTS
    eids = jnp.arange(E, dtype=jnp.int32)
    padded = (counts + EXPERT_TILE - 1) // EXPERT_TILE * EXPERT_TILE
    ends = jnp.cumsum(padded)
    poff = ends - padded
    off = jnp.cumsum(counts) - counts
    idx_k = idx[:, :, :TOP_K]
    onehot = idx_k[..., None] == eids
    pos = jnp.sum(jnp.where(onehot, poff, 0), axis=-1) + rank[:, :, :TOP_K]
    real = (jnp.arange(Lp) >= PAD)[None, :, None]
    pos = jnp.where(real, pos, 0).astype(jnp.int32)
    keys = jnp.where(real, idx_k, E).reshape(A)
    tok = jnp.broadcast_to(jnp.arange(T, dtype=jnp.int32).reshape(B, Lp, 1), idx_k.shape).reshape(A)
    _, sorted_tok = lax.sort((keys, tok), num_keys=1, is_stable=True)
    n_tok = -(-A // TOK_ALIGN) * TOK_ALIGN + TOK_WINDOW
    sorted_tok = jnp.concatenate([sorted_tok, jnp.full((n_tok - A,), PAD, jnp.int32)])
    blk0 = jnp.arange(nb, dtype=jnp.int32) * EXPERT_TILE
    block_e = jnp.minimum(jnp.sum(ends[None, :] <= blk0[:, None], axis=1), E - 1).astype(jnp.int32)
    sel = block_e[:, None] == eids
    block_start = jnp.sum(jnp.where(sel, off - poff, 0), axis=1) + blk0
    block_start = jnp.clip(block_start, 0, A).astype(jnp.int32)
    n_used = (ends[-1] // EXPERT_TILE).astype(jnp.int32).reshape(1)
    changed = jnp.concatenate([jnp.zeros((1,), jnp.int32),
                               (block_e[1:] != block_e[:-1]).astype(jnp.int32)])
    block_ord = jnp.cumsum(changed).astype(jnp.int32)
    block_next = (jnp.sum(jnp.where(sel, ends, 0), axis=1) // EXPERT_TILE).astype(jnp.int32)
    return pos.reshape(T, TOP_K), sorted_tok, (block_e, block_start, block_ord, block_next, n_used)


def kernel(x, meta, ln_in_g, ln_in_b, w_in, b_igate, b_fgate, conv_qk, ml_norm_g, sb_norm_g, pool_w, pool_scale, w_out, ln1_g, ln1_b, router_w, router_bias, exp_w13, exp_w2, sh_w13, sh_w2, ln2_g, ln2_b):
    B, S, D = x.shape
    depth = w_in.shape[0]
    alpha = (2 * depth) ** 0.25
    Lp = PAD + N_META + S
    T = B * Lp
    assert Lp % BLK == 0 and T % ROW_TILE == 0 and T % COMBINE_TILE == 0

    h0 = jnp.concatenate([jnp.zeros((B, PAD, D), x.dtype),
                          jnp.broadcast_to(meta[None].astype(x.dtype), (B, N_META, D)), x], axis=1)
    h, hb = _ln_rows(h0.reshape(T, D), ln_in_g, ln_in_b)

    for l in range(depth):
        w_gate, gate_bias = _arrange_gates(w_in[l, :, C_SBQ:O_SB], b_igate[l], b_fgate[l])
        proj3 = _in_proj(hb, _prep_in_proj(w_in, l)).reshape(B, Lp, N_MAIN)
        gcol = _gates(hb.reshape(B, Lp, D), w_gate, gate_bias)
        grow = jnp.transpose(gcol.reshape(B, Lp, ML_GROUPS, BLK)[..., :8], (0, 2, 3, 1))
        qk3 = _conv_silu(proj3, conv_qk[l])
        a_ml = _mlstm(qk3, proj3, gcol, grow, ml_norm_g[l])
        a_sb = _stick_breaking(proj3, sb_norm_g[l])
        a_pool = _pool(proj3, pool_w[l], pool_scale[l])
        h1, h1_tiles = _outproj_ln(a_ml.reshape(T, D_ML), a_sb.reshape(T, D_SB), a_pool.reshape(T, D_POOL),
                                   w_out[l].astype(BF16), h, ln1_g[l], ln1_b[l], alpha)

        idx, gate, rank, counts = _router(h1.reshape(B, Lp, D), router_w[l], router_bias[l])
        pos, sorted_tok, tables = _dispatch_plan(idx, rank, counts[0], B, Lp)
        y = _routed_experts(h1_tiles, sorted_tok, *tables, exp_w13, exp_w2, l)
        shared = _shared_expert(h1, sh_w13[l].astype(BF16), sh_w2[l].astype(BF16))
        n_steps = T // COMBINE_TILE
        pos_blocks = jnp.transpose(pos.reshape(n_steps, COMBINE_TILE, TOP_K), (0, 2, 1))
        pos_blocks = pos_blocks.reshape(n_steps, 1, TOP_K * COMBINE_TILE)
        h, hb = _combine_ln(pos_blocks, y, gate.reshape(T, BLK), shared, h1,
                            ln2_g[l], ln2_b[l], alpha)

    return h.reshape(B, Lp, D)[:, PAD + N_META:]
```

```python
import functools

import jax
import jax.numpy as jnp
from jax import lax
from jax.experimental import pallas as pl
from jax.experimental.pallas import tpu as pltpu

N_META = 16
HEAD_DIM = 128
ML_HEADS = 6
SB_HEADS = 6
POOL_WINDOWS = (2, 4, 8, 16)
POOL_GROUPS = 4
D_ML = ML_HEADS * HEAD_DIM
D_SB = SB_HEADS * HEAD_DIM
D_POOL = POOL_GROUPS * HEAD_DIM
CONV_W = 4
N_EXPERTS = 64
TOP_K = 8
ROUTED_SCALE = 2.5
LN_EPS = 1e-5
RMS_EPS = 1e-6

BLK = 128
PAD = (-N_META) % BLK
ML_HPS = 2
ML_GROUPS = ML_HEADS // ML_HPS
SB_KEY_BLOCKS = 4
SB_HPS = 2
SB_W = SB_HPS * HEAD_DIM
ROW_TILE = 512
EXPERT_TILE = 256
COMBINE_TILE = 128
TOK_ALIGN_LOG2 = 10
TOK_ALIGN = 1 << TOK_ALIGN_LOG2
TOK_WINDOW = 2 * TOK_ALIGN
NEG = -1e30
VMEM_LIMIT = 56 * 1024 * 1024

C_QK = 0
C_V = 2 * D_ML
C_O = C_V + D_ML
C_SBQ = C_O + D_ML
C_SBK = C_SBQ + D_SB
C_SBV = C_SBK + D_SB
C_POOL = C_SBV + D_SB
N_MAIN = C_POOL + D_POOL
O_SB = C_SBQ + 2 * ML_HEADS
XROWS = 16
ROW_PITCH = XROWS + 1

F32 = jnp.float32
BF16 = jnp.bfloat16


def _cparams(sem, vmem=VMEM_LIMIT):
    return pltpu.CompilerParams(dimension_semantics=sem, vmem_limit_bytes=vmem)


def _layer_norm(x, g, b):
    mu = jnp.mean(x, axis=-1, keepdims=True)
    xc = x - mu
    var = jnp.mean(xc * xc, axis=-1, keepdims=True)
    return xc * lax.rsqrt(var + LN_EPS) * g + b


def _sigmoid(x):
    return 1.0 / (1.0 + jnp.exp(-x))


def _softplus(x):
    return jnp.maximum(x, 0.0) + jnp.log(1.0 + jnp.exp(-jnp.abs(x)))


def _split_dot(a, b, split_lhs):
    x = a if split_lhs else b
    out = None
    for _ in range(3):
        part = x.astype(BF16)
        x = x - part.astype(F32)
        term = (jnp.dot(part, b, preferred_element_type=F32) if split_lhs
                else jnp.dot(a, part, preferred_element_type=F32))
        out = term if out is None else out + term
    return out


def _ln_kernel(x_ref, g_ref, b_ref, o_ref, ob_ref):
    y = _layer_norm(x_ref[...], g_ref[...], b_ref[...])
    o_ref[...] = y
    ob_ref[...] = y.astype(BF16)


def _ln_rows(x, g, b):
    T, D = x.shape
    row = pl.BlockSpec((ROW_TILE, D), lambda i: (i, 0))
    vec = pl.BlockSpec((1, D), lambda i: (0, 0))
    return pl.pallas_call(
        _ln_kernel,
        grid=(T // ROW_TILE,),
        in_specs=[row, vec, vec],
        out_specs=[row, row],
        out_shape=[jax.ShapeDtypeStruct((T, D), F32), jax.ShapeDtypeStruct((T, D), BF16)],
        compiler_params=_cparams(("parallel",)),
        name="ln_in",
    )(x, g.reshape(1, D), b.reshape(1, D))


def _mm_kernel(a_ref, w_ref, o_ref):
    o_ref[...] = jnp.dot(a_ref[...], w_ref[...], preferred_element_type=F32).astype(o_ref.dtype)


def _in_proj(a, w):
    M, K = a.shape
    N = w.shape[1]
    tn = N // 2
    return pl.pallas_call(
        _mm_kernel,
        grid=(N // tn, M // ROW_TILE),
        in_specs=[pl.BlockSpec((ROW_TILE, K), lambda j, i: (i, 0)),
                  pl.BlockSpec((K, tn), lambda j, i: (0, j))],
        out_specs=pl.BlockSpec((ROW_TILE, tn), lambda j, i: (i, j)),
        out_shape=jax.ShapeDtypeStruct((M, N), BF16),
        compiler_params=_cparams(("parallel", "parallel")),
        name="in_proj",
    )(a, w)


def _gates_kernel(hb_ref, wg_ref, bias_ref, o_ref, *, tm):
    j = pl.program_id(1)
    g = jnp.dot(hb_ref[0], wg_ref[...], preferred_element_type=F32) + bias_ref[...]
    col = jnp.bitwise_and(lax.broadcasted_iota(jnp.int32, g.shape, 1), BLK - 1)
    row = lax.broadcasted_iota(jnp.int32, g.shape, 0) + j * tm
    is_f = jnp.logical_and(col >= ML_HPS, col < 2 * ML_HPS)
    val = jnp.where(is_f, -_softplus(-g), g)
    pad_val = jnp.where(is_f, 0.0, NEG)
    o_ref[0] = jnp.where(row < PAD, pad_val, val)


def _gates(hb3, wg, bias):
    B, Lp, D = hb3.shape
    tm = Lp // 4
    Ng = wg.shape[1]
    return pl.pallas_call(
        functools.partial(_gates_kernel, tm=tm),
        grid=(B, Lp // tm),
        in_specs=[pl.BlockSpec((1, tm, D), lambda b, j: (b, j, 0)),
                  pl.BlockSpec((D, Ng), lambda b, j: (0, 0)),
                  pl.BlockSpec((1, Ng), lambda b, j: (0, 0))],
        out_specs=pl.BlockSpec((1, tm, Ng), lambda b, j: (b, j, 0)),
        out_shape=jax.ShapeDtypeStruct((B, Lp, Ng), F32),
        compiler_params=_cparams(("parallel", "parallel")),
        name="ml_gates",
    )(hb3, wg, bias)


def _conv_kernel(x_ref, w_ref, o_ref):
    c = pl.program_id(1)
    x = x_ref[0].astype(F32)
    row = lax.broadcasted_iota(jnp.int32, x.shape, 0)
    x = jnp.where(row < PAD, 0.0, x)
    w = w_ref[...]
    y = x * w[CONV_W - 1:CONV_W]
    for s in range(1, CONV_W):
        y = y + pltpu.roll(x, s, axis=0) * w[CONV_W - 1 - s:CONV_W - s]
    y = y * _sigmoid(y)
    scale = jnp.where(c < ML_HEADS, HEAD_DIM ** -0.5, 1.0).astype(F32)
    o_ref[0] = (y * scale).astype(BF16)


def _conv_silu(proj3, conv_w):
    B, Lp, _ = proj3.shape
    nc = 2 * D_ML // BLK
    return pl.pallas_call(
        _conv_kernel,
        grid=(B, nc),
        in_specs=[pl.BlockSpec((1, Lp, BLK), lambda b, c: (b, 0, c)),
                  pl.BlockSpec((CONV_W, BLK), lambda b, c: (0, c))],
        out_specs=pl.BlockSpec((1, Lp, BLK), lambda b, c: (b, 0, c)),
        out_shape=jax.ShapeDtypeStruct((B, Lp, 2 * D_ML), BF16),
        compiler_params=_cparams(("parallel", "parallel")),
        name="conv_silu",
    )(proj3, conv_w)


def _mlstm_kernel(q_ref, k_ref, v_ref, o_ref, gc_ref, gr_ref, ng_ref, out_ref, st_ref, m_ref):
    n_chunks = q_ref.shape[1] // BLK
    st_ref[...] = jnp.zeros_like(st_ref)
    m_ref[...] = jnp.zeros_like(m_ref)

    ri = lax.broadcasted_iota(jnp.int32, (BLK, BLK), 0)
    ci = lax.broadcasted_iota(jnp.int32, (BLK, BLK), 1)
    causal = ci <= ri
    lower_incl = jnp.where(causal, 1.0, 0.0).astype(BF16)
    upper_incl = jnp.where(ri <= ci, 1.0, 0.0).astype(BF16)
    ones_col = jnp.where(ci == 0, 1.0, 0.0).astype(BF16)

    def chunk(c, carry):
        r0 = pl.multiple_of(c * BLK, BLK)
        gcol = gc_ref[0, pl.ds(r0, BLK), :]
        grow = gr_ref[0, 0, :, pl.ds(r0, BLK)]
        bcol = _split_dot(lower_incl, gcol, split_lhs=False)
        brow = _split_dot(grow, upper_incl, split_lhs=True)
        for hh in range(ML_HPS):
            cs = slice(hh * HEAD_DIM, (hh + 1) * HEAD_DIM)
            q = q_ref[0, pl.ds(r0, BLK), cs]
            k = k_ref[0, pl.ds(r0, BLK), cs]
            v = v_ref[0, pl.ds(r0, BLK), cs]
            li_c = gcol[:, hh:hh + 1]
            b_c = bcol[:, ML_HPS + hh:ML_HPS + hh + 1]
            li_r = grow[hh:hh + 1, :]
            b_r = brow[ML_HPS + hh:ML_HPS + hh + 1, :]
            m = m_ref[hh, 0:1, 0:1]
            st = st_ref[hh]

            inter = b_c + m
            dmat = jnp.where(causal, b_c - b_r + li_r, NEG)
            m_t = jnp.maximum(inter, jnp.max(dmat, axis=-1, keepdims=True))
            w_inter = jnp.exp(inter - m_t)
            s_qk = lax.dot_general(q, k, (((1,), (1,)), ((), ())), preferred_element_type=F32)
            ws = jnp.exp(dmat - m_t) * s_qk
            v_aug = jnp.concatenate([v, ones_col], axis=1)
            nd = (w_inter * jnp.dot(q, st.astype(BF16), preferred_element_type=F32)
                  + jnp.dot(ws.astype(BF16), v_aug, preferred_element_type=F32))
            num = nd[:, :HEAD_DIM]
            den = nd[:, HEAD_DIM:HEAD_DIM + 1]
            h = num / jnp.maximum(jnp.abs(den), jnp.exp(-m_t))

            og = o_ref[0, pl.ds(r0, BLK), cs].astype(F32)
            hn = h * lax.rsqrt(jnp.mean(h * h, axis=-1, keepdims=True) + RMS_EPS)
            hn = hn * ng_ref[:, cs] * _sigmoid(og)
            out_ref[0, pl.ds(r0, BLK), cs] = hn.astype(BF16)

            b_last = b_r[:, BLK - 1:BLK]
            g_r = b_last - b_r + li_r
            m_new = jnp.maximum(b_last + m, jnp.max(g_r, axis=-1, keepdims=True))
            wg = jnp.exp(b_last - b_c + li_c - m_new)
            decay = jnp.exp(b_last + m - m_new)
            wv = (wg * v_aug.astype(F32)).astype(BF16)
            kv = lax.dot_general(k, wv, (((0,), (0,)), ((), ())), preferred_element_type=F32)
            st_ref[hh] = decay * st + kv
            m_ref[hh] = jnp.broadcast_to(m_new, m_ref.shape[1:])
        return carry

    lax.fori_loop(0, n_chunks, chunk, 0)


def _mlstm(qk3, proj3, gcol, grow, norm_g):
    B, Lp, _ = qk3.shape
    W = ML_HPS * HEAD_DIM
    nq = D_ML // W
    seq = lambda off: pl.BlockSpec((1, Lp, W), lambda b, g: (b, 0, off + g))
    return pl.pallas_call(
        _mlstm_kernel,
        grid=(B, ML_GROUPS),
        in_specs=[seq(0), seq(nq),
                  pl.BlockSpec((1, Lp, W), lambda b, g: (b, 0, C_V // W + g)),
                  pl.BlockSpec((1, Lp, W), lambda b, g: (b, 0, C_O // W + g)),
                  pl.BlockSpec((1, Lp, BLK), lambda b, g: (b, 0, g)),
                  pl.BlockSpec((1, 1, 8, Lp), lambda b, g: (b, g, 0, 0)),
                  pl.BlockSpec((1, W), lambda b, g: (0, g))],
        out_specs=pl.BlockSpec((1, Lp, W), lambda b, g: (b, 0, g)),
        out_shape=jax.ShapeDtypeStruct((B, Lp, D_ML), BF16),
        scratch_shapes=[pltpu.VMEM((ML_HPS, HEAD_DIM, 2 * HEAD_DIM), F32),
                        pltpu.VMEM((ML_HPS, 8, BLK), F32)],
        compiler_params=_cparams(("parallel", "parallel")),
        name="mlstm",
    )(qk3, qk3, proj3, proj3, gcol, grow, norm_g.reshape(1, D_ML))


def _sb_kernel(q_ref, k_ref, v_ref, g_ref, o_ref):
    i = pl.program_id(2)
    W = SB_KEY_BLOCKS * BLK
    r1 = lax.broadcasted_iota(jnp.int32, (BLK, BLK), 0)
    c1 = lax.broadcasted_iota(jnp.int32, (BLK, BLK), 1)
    upper_strict = jnp.where(r1 > c1, 1.0, 0.0).astype(BF16)
    rowp = lax.broadcasted_iota(jnp.int32, (BLK, W), 0) + i * BLK
    ci = lax.broadcasted_iota(jnp.int32, (BLK, W), 1)
    scale = HEAD_DIM ** -0.5
    heads = [slice(hh * HEAD_DIM, (hh + 1) * HEAD_DIM) for hh in range(SB_HPS)]

    def one_head(q, k, v, carry, acc, valid):
        z = lax.dot_general(q, k, (((1,), (1,)), ((), ())), preferred_element_type=F32) * scale
        sp = _softplus(z)
        lk = jnp.where(valid, -sp, 0.0)
        hi = lk.astype(BF16)
        lo = (lk - hi.astype(F32)).astype(BF16)
        pieces = [None] * SB_KEY_BLOCKS
        off = carry
        for c in reversed(range(SB_KEY_BLOCKS)):
            sl = slice(c * BLK, (c + 1) * BLK)
            local = (jnp.dot(hi[:, sl], upper_strict, preferred_element_type=F32)
                     + jnp.dot(lo[:, sl], upper_strict, preferred_element_type=F32))
            pieces[c] = local + off
            off = off + jnp.sum(lk[:, sl], axis=-1, keepdims=True)
        after = jnp.concatenate(pieces, axis=1)
        a = jnp.where(valid, jnp.exp(z - sp + after), 0.0)
        return off, acc + jnp.dot(a.astype(BF16), v, preferred_element_type=F32)

    def step(t, state):
        end = (i + 1) * BLK - t * W
        s0 = pl.multiple_of(jnp.maximum(end - W, 0), BLK)
        colp = ci + s0
        valid = jnp.logical_and(colp < jnp.minimum(rowp, end), colp >= PAD)
        return tuple(one_head(q_ref[0, :, hs], k_ref[0, pl.ds(s0, W), hs], v_ref[0, pl.ds(s0, W), hs],
                              carry, acc, valid)
                     for hs, (carry, acc) in zip(heads, state))

    init = tuple((jnp.zeros((BLK, 1), F32), jnp.zeros((BLK, HEAD_DIM), F32)) for _ in heads)
    n_steps = (i + SB_KEY_BLOCKS) // SB_KEY_BLOCKS
    final = lax.fori_loop(0, n_steps, step, init)
    for hs, (_, acc) in zip(heads, final):
        hn = acc * lax.rsqrt(jnp.mean(acc * acc, axis=-1, keepdims=True) + RMS_EPS) * g_ref[:, hs]
        o_ref[0, :, hs] = hn.astype(BF16)


def _stick_breaking(proj3, norm_g):
    B, Lp, _ = proj3.shape
    nq = Lp // BLK
    return pl.pallas_call(
        _sb_kernel,
        grid=(B, SB_HEADS // SB_HPS, nq),
        in_specs=[pl.BlockSpec((1, BLK, SB_W), lambda b, h, i: (b, i, C_SBQ // SB_W + h)),
                  pl.BlockSpec((1, Lp, SB_W), lambda b, h, i: (b, 0, C_SBK // SB_W + h)),
                  pl.BlockSpec((1, Lp, SB_W), lambda b, h, i: (b, 0, C_SBV // SB_W + h)),
                  pl.BlockSpec((1, SB_W), lambda b, h, i: (0, h))],
        out_specs=pl.BlockSpec((1, BLK, SB_W), lambda b, h, i: (b, i, h)),
        out_shape=jax.ShapeDtypeStruct((B, Lp, D_SB), BF16),
        compiler_params=_cparams(("parallel", "parallel", "parallel")),
        name="stick_breaking",
    )(proj3, proj3, proj3, norm_g.reshape(1, D_SB))


def _pool_kernel(x_ref, w_ref, s_ref, o_ref):
    g = pl.program_id(1)
    x = x_ref[0].astype(F32)
    row = lax.broadcasted_iota(jnp.int32, x.shape, 0)
    x = jnp.where(row < PAD, 0.0, x)
    pos1 = (row - PAD + 1).astype(F32)
    sums = x
    mean = jnp.zeros_like(x)
    shift = 1
    for gi, win in enumerate(POOL_WINDOWS):
        while shift < win:
            sums = sums + pltpu.roll(sums, shift, axis=0)
            shift *= 2
        cnt = jnp.maximum(jnp.minimum(pos1, float(win)), 1.0)
        mean = jnp.where(g == gi, sums / cnt, mean)
    p = (mean - x).astype(BF16)
    y = jnp.dot(p, w_ref[0].astype(BF16), preferred_element_type=F32) * s_ref[...]
    o_ref[0] = y.astype(BF16)


def _pool(proj3, pool_w, pool_scale):
    B, Lp, _ = proj3.shape
    return pl.pallas_call(
        _pool_kernel,
        grid=(B, POOL_GROUPS),
        in_specs=[pl.BlockSpec((1, Lp, HEAD_DIM), lambda b, g: (b, 0, C_POOL // HEAD_DIM + g)),
                  pl.BlockSpec((1, HEAD_DIM, HEAD_DIM), lambda b, g: (g, 0, 0)),
                  pl.BlockSpec((1, HEAD_DIM), lambda b, g: (0, g))],
        out_specs=pl.BlockSpec((1, Lp, HEAD_DIM), lambda b, g: (b, 0, g)),
        out_shape=jax.ShapeDtypeStruct((B, Lp, D_POOL), BF16),
        compiler_params=_cparams(("parallel", "parallel")),
        name="pool_mixer",
    )(proj3, pool_w, pool_scale.reshape(1, D_POOL))


def _to_row_tiles(ref, y):
    for s in range(XROWS):
        ref[:, s, :] = y[:, s * BLK:(s + 1) * BLK]


def _from_pitched(ref, slot, n, first=0):
    return jnp.concatenate(
        [ref[slot, pl.ds(first * ROW_PITCH + s, n, stride=ROW_PITCH), :] for s in range(XROWS)], axis=1)


def _outproj_kernel(a1_ref, a2_ref, a3_ref, w_ref, h_ref, g_ref, b_ref, o_ref, ot_ref, *, alpha):
    acc = jnp.dot(a1_ref[...], w_ref[0:D_ML, :], preferred_element_type=F32)
    acc = acc + jnp.dot(a2_ref[...], w_ref[D_ML:D_ML + D_SB, :], preferred_element_type=F32)
    acc = acc + jnp.dot(a3_ref[...], w_ref[D_ML + D_SB:, :], preferred_element_type=F32)
    y = _layer_norm(alpha * h_ref[...] + acc, g_ref[...], b_ref[...])
    o_ref[...] = y
    _to_row_tiles(ot_ref, y)


def _outproj_ln(a_ml, a_sb, a_pool, w, h, g, b, alpha):
    T, D = h.shape
    row = lambda n: pl.BlockSpec((ROW_TILE, n), lambda i: (i, 0))
    vec = pl.BlockSpec((1, D), lambda i: (0, 0))
    return pl.pallas_call(
        functools.partial(_outproj_kernel, alpha=alpha),
        grid=(T // ROW_TILE,),
        in_specs=[row(D_ML), row(D_SB), row(D_POOL),
                  pl.BlockSpec(w.shape, lambda i: (0, 0)), row(D), vec, vec],
        out_specs=[row(D), pl.BlockSpec((ROW_TILE, XROWS, BLK), lambda i: (i, 0, 0))],
        out_shape=[jax.ShapeDtypeStruct((T, D), F32), jax.ShapeDtypeStruct((T, XROWS, BLK), F32)],
        compiler_params=_cparams(("parallel",)),
        name="out_proj_ln",
    )(a_ml, a_sb, a_pool, w, h, g.reshape(1, D), b.reshape(1, D))


def _router_kernel(h_ref, rw_ref, rb_ref, idx_ref, gate_ref, rank_ref, cnt_ref, run_ref, *, tm):
    first = jnp.logical_and(pl.program_id(0) == 0, pl.program_id(1) == 0)

    @pl.when(first)
    def _():
        run_ref[...] = jnp.zeros_like(run_ref)

    x = h_ref[0]
    logits = jnp.dot(x, rw_ref[...], preferred_element_type=F32, precision=lax.Precision.HIGHEST)
    scores = _sigmoid(logits)
    work = scores + rb_ref[...]
    E = scores.shape[1]
    lane = lax.broadcasted_iota(jnp.int32, (tm, E), 1).astype(F32)
    row = lax.broadcasted_iota(jnp.int32, (tm, 1), 0) + pl.program_id(1) * tm
    real = (row >= PAD).astype(F32)

    picks = []
    gates = []
    member = jnp.zeros((tm, E), F32)
    for _ in range(TOP_K):
        mx = jnp.max(work, axis=-1, keepdims=True)
        pick = jnp.min(jnp.where(work == mx, lane, float(E)), axis=-1, keepdims=True)
        onehot = lane == pick
        gates.append(jnp.sum(jnp.where(onehot, scores, 0.0), axis=-1, keepdims=True))
        picks.append(pick)
        member = member + jnp.where(onehot, real, 0.0)
        work = jnp.where(onehot, -jnp.inf, work)
    gsum = gates[0]
    for gk in gates[1:]:
        gsum = gsum + gk

    ri = lax.broadcasted_iota(jnp.int32, (tm, tm), 0)
    ci = lax.broadcasted_iota(jnp.int32, (tm, tm), 1)
    lower_strict = jnp.where(ci < ri, 1.0, 0.0).astype(BF16)
    before = jnp.dot(lower_strict, member.astype(BF16), preferred_element_type=F32) + run_ref[...]

    lane_o = lax.broadcasted_iota(jnp.int32, (tm, BLK), 1)
    idx_o = jnp.zeros((tm, BLK), F32)
    gate_o = jnp.zeros((tm, BLK), F32)
    rank_o = jnp.zeros((tm, BLK), F32)
    for kk in range(TOP_K):
        onehot = lane == picks[kk]
        rk = jnp.sum(jnp.where(onehot, before, 0.0), axis=-1, keepdims=True)
        sel = lane_o == kk
        idx_o = jnp.where(sel, picks[kk], idx_o)
        gate_o = jnp.where(sel, gates[kk] / gsum * ROUTED_SCALE, gate_o)
        rank_o = jnp.where(sel, rk, rank_o)
    idx_ref[0] = idx_o.astype(jnp.int32)
    gate_ref[0] = gate_o
    rank_ref[0] = rank_o.astype(jnp.int32)
    run_ref[...] = run_ref[...] + jnp.sum(member, axis=0, keepdims=True)
    cnt_ref[...] = run_ref[...].astype(jnp.int32)


def _router(h3, router_w, router_bias):
    B, Lp, D = h3.shape
    E = router_w.shape[1]
    tm = Lp // 4
    out = lambda: pl.BlockSpec((1, tm, BLK), lambda b, j: (b, j, 0))
    return pl.pallas_call(
        functools.partial(_router_kernel, tm=tm),
        grid=(B, Lp // tm),
        in_specs=[pl.BlockSpec((1, tm, D), lambda b, j: (b, j, 0)),
                  pl.BlockSpec((D, E), lambda b, j: (0, 0)),
                  pl.BlockSpec((1, E), lambda b, j: (0, 0))],
        out_specs=[out(), out(), out(), pl.BlockSpec((1, E), lambda b, j: (0, 0))],
        out_shape=[jax.ShapeDtypeStruct((B, Lp, BLK), jnp.int32),
                   jax.ShapeDtypeStruct((B, Lp, BLK), F32),
                   jax.ShapeDtypeStruct((B, Lp, BLK), jnp.int32),
                   jax.ShapeDtypeStruct((1, E), jnp.int32)],
        scratch_shapes=[pltpu.VMEM((1, E), F32)],
        compiler_params=_cparams(("arbitrary", "arbitrary")),
        name="router",
    )(h3, router_w, router_bias.reshape(1, E))


def _swiglu(x, w13, w2):
    h1 = jnp.dot(x, w13, preferred_element_type=F32)
    f = h1.shape[1] // 2
    gate, up = h1[:, :f], h1[:, f:]
    act = (gate * _sigmoid(gate) * up).astype(BF16)
    return jnp.dot(act, w2, preferred_element_type=F32)


def _shared_kernel(x_ref, w13_ref, w2_ref, o_ref):
    o_ref[...] = _swiglu(x_ref[...].astype(BF16), w13_ref[...], w2_ref[...])


def _shared_expert(h, w13, w2):
    T, D = h.shape
    row = pl.BlockSpec((ROW_TILE, D), lambda i: (i, 0))
    return pl.pallas_call(
        _shared_kernel,
        grid=(T // ROW_TILE,),
        in_specs=[row, pl.BlockSpec(w13.shape, lambda i: (0, 0)),
                  pl.BlockSpec(w2.shape, lambda i: (0, 0))],
        out_specs=row,
        out_shape=jax.ShapeDtypeStruct((T, D), F32),
        compiler_params=_cparams(("parallel",)),
        name="shared_expert",
    )(h, w13, w2)


def _moe_kernel(be_ref, bs_ref, bo_ref, bn_ref, nb_ref, tok_hbm, x_hbm, w13_hbm, w2_hbm, y_hbm,
                win, wsem, xbuf, xsem, ybuf, ysem, zbuf, zsem, wf13, wf2, fsem, w13b, w2b, *, layer):
    i = pl.program_id(0)
    n_used = nb_ref[0]
    slot = lax.rem(i, 2)
    F = w2b.shape[0]

    @pl.when(i == 0)
    def _():
        zbuf[...] = jnp.zeros_like(zbuf)

    def window(blk, s):
        base = pl.multiple_of(lax.shift_left(lax.shift_right_logical(bs_ref[blk], TOK_ALIGN_LOG2),
                                             TOK_ALIGN_LOG2), TOK_ALIGN)
        return pltpu.make_async_copy(tok_hbm.at[pl.ds(base, TOK_WINDOW)], win.at[s], wsem.at[s])

    def weights(e, s):
        return (pltpu.make_async_copy(w13_hbm.at[layer, e], wf13.at[s], fsem.at[0, s]),
                pltpu.make_async_copy(w2_hbm.at[layer, e], wf2.at[s], fsem.at[1, s]))

    def row_copy(tok, s, r):
        return pltpu.make_async_copy(x_hbm.at[tok], xbuf.at[s, pl.ds(r * ROW_PITCH, XROWS), :], xsem.at[s])

    def out_copy(s, r):
        return pltpu.make_async_copy(ybuf.at[s, pl.ds(r * ROW_PITCH, XROWS), :],
                                     y_hbm.at[i * EXPERT_TILE + r], ysem.at[s])

    def tile_bytes_wait(sem_ref):
        pltpu.make_async_copy(x_hbm.at[pl.ds(0, EXPERT_TILE)], zbuf, sem_ref).wait()

    def rows_wait(s):
        tile_bytes_wait(xsem.at[s])

    @pl.when(jnp.logical_and(i == 0, n_used > 0))
    def _():
        for c in weights(be_ref[0], 0):
            c.start()
        window(0, 0).start()
        window(0, 0).wait()
        first0 = jnp.bitwise_and(bs_ref[0], TOK_ALIGN - 1)

        def body(r, c):
            row_copy(win[0, first0 + r], 0, r).start()
            return c
        lax.fori_loop(0, EXPERT_TILE, body, 0)

        @pl.when(n_used > 1)
        def _():
            window(1, 1).start()

    @pl.when(i + 1 < n_used)
    def _():
        window(i + 1, 1 - slot).wait()

    @pl.when(i + 2 < n_used)
    def _():
        window(i + 2, slot).start()

    @pl.when(jnp.logical_and(i >= 2, i < n_used))
    def _():
        tile_bytes_wait(ysem.at[slot])

    @pl.when(i < n_used)
    def _():
        e = be_ref[i]
        ws = lax.rem(bo_ref[i], 2)

        @pl.when(jnp.logical_or(i == 0, e != be_ref[jnp.maximum(i - 1, 0)]))
        def _():
            for c in weights(e, ws):
                c.wait()
            nxt = bn_ref[i]

            @pl.when(nxt < n_used)
            def _():
                for c in weights(be_ref[nxt], 1 - ws):
                    c.start()
            w13b[...] = wf13[ws].astype(BF16)
            w2b[...] = wf2[ws].astype(BF16)

        nblk = jnp.minimum(i + 1, n_used - 1)
        nwin = lax.rem(nblk, 2)
        first = jnp.bitwise_and(bs_ref[nblk], TOK_ALIGN - 1)
        groups = 2 + XROWS // 4
        per = EXPERT_TILE // groups

        def issue(g):
            hi = EXPERT_TILE if g == groups - 1 else (g + 1) * per
            for r in range(g * per, hi):
                row_copy(win[nwin, first + r], 1 - slot, r).start(priority=r % 2)

        rows_wait(slot)
        x = _from_pitched(xbuf, slot, EXPERT_TILE).astype(BF16)
        acts = []
        half = F // 2
        for c in range(2):
            gate = jnp.dot(x, w13b[:, c * half:(c + 1) * half], preferred_element_type=F32)
            up = jnp.dot(x, w13b[:, F + c * half:F + (c + 1) * half], preferred_element_type=F32)
            acts.append((gate * _sigmoid(gate) * up).astype(BF16))
            issue(c)
        act = jnp.concatenate(acts, axis=1)
        for c in range(XROWS // 4):
            y = jnp.dot(act, w2b[:, c * 4 * BLK:(c + 1) * 4 * BLK], preferred_element_type=F32)
            for s in range(4):
                ybuf[slot, pl.ds(c * 4 + s, EXPERT_TILE, stride=ROW_PITCH), :] = y[:, s * BLK:(s + 1) * BLK]
            issue(2 + c)
        for r in range(EXPERT_TILE):
            out_copy(slot, r).start(priority=r % 2)

        @pl.when(i == n_used - 1)
        def _():
            rows_wait(1 - slot)
            tile_bytes_wait(ysem.at[slot])

            @pl.when(i >= 1)
            def _():
                tile_bytes_wait(ysem.at[1 - slot])

    @pl.when(i >= n_used)
    def _():
        zero = pltpu.make_async_copy(zbuf, y_hbm.at[pl.ds(i * EXPERT_TILE, EXPERT_TILE)], zsem.at[0])
        zero.start()
        zero.wait()


def _routed_experts(x_tiles, sorted_tok, block_e, block_start, block_ord, block_next, n_used, w13, w2, layer):
    nb = block_e.shape[0]
    _, E, D, F2 = w13.shape
    F = F2 // 2
    grid_spec = pltpu.PrefetchScalarGridSpec(
        num_scalar_prefetch=5,
        grid=(nb,),
        in_specs=[pl.BlockSpec(memory_space=pl.ANY)] * 4,
        out_specs=pl.BlockSpec(memory_space=pl.ANY),
        scratch_shapes=[pltpu.SMEM((2, TOK_WINDOW), jnp.int32),
                        pltpu.SemaphoreType.DMA((2,)),
                        pltpu.VMEM((2, EXPERT_TILE * ROW_PITCH, BLK), F32),
                        pltpu.SemaphoreType.DMA((2,)),
                        pltpu.VMEM((2, EXPERT_TILE * ROW_PITCH, BLK), F32),
                        pltpu.SemaphoreType.DMA((2,)),
                        pltpu.VMEM((EXPERT_TILE, XROWS, BLK), F32),
                        pltpu.SemaphoreType.DMA((1,)),
                        pltpu.VMEM((2, D, F2), F32),
                        pltpu.VMEM((2, F, D), F32),
                        pltpu.SemaphoreType.DMA((2, 2)),
                        pltpu.VMEM((D, F2), BF16),
                        pltpu.VMEM((F, D), BF16)],
    )
    return pl.pallas_call(
        functools.partial(_moe_kernel, layer=layer),
        grid_spec=grid_spec,
        out_shape=jax.ShapeDtypeStruct((nb * EXPERT_TILE, XROWS, BLK), F32),
        compiler_params=_cparams(("arbitrary",)),
        name="routed_experts",
    )(block_e, block_start, block_ord, block_next, n_used, sorted_tok, x_tiles, w13, w2)


def _combine_kernel(pos_ref, posn_ref, y_hbm, gate_ref, sh_ref, h_ref, g_ref, b_ref,
                    o_ref, ob_ref, gbuf, sem, *, alpha, n_steps):
    i = pl.program_id(0)
    slot = lax.rem(i, 2)
    n_rows = TOP_K * COMBINE_TILE

    def copy(pos, s, r):
        return pltpu.make_async_copy(y_hbm.at[pos], gbuf.at[s, pl.ds(r * ROW_PITCH, XROWS), :], sem.at[s])

    def gather(pref, s):
        def body(r8, c):
            for u in range(8):
                r = r8 * 8 + u
                copy(pref[0, 0, r], s, r).start(priority=u % 2)
            return c
        lax.fori_loop(0, n_rows // 8, body, 0, unroll=2)

    @pl.when(i == 0)
    def _():
        gather(pos_ref, 0)

    @pl.when(i + 1 < n_steps)
    def _():
        gather(posn_ref, 1 - slot)

    pltpu.make_async_copy(y_hbm.at[pl.ds(0, n_rows)], y_hbm.at[pl.ds(0, n_rows)], sem.at[slot]).wait()

    gate = gate_ref[...]
    acc = alpha * h_ref[...] + sh_ref[...]
    for kk in range(TOP_K):
        rows = _from_pitched(gbuf, slot, COMBINE_TILE, first=kk * COMBINE_TILE)
        acc = acc + gate[:, kk:kk + 1] * rows
    y = _layer_norm(acc, g_ref[...], b_ref[...])
    o_ref[...] = y
    ob_ref[...] = y.astype(BF16)


def _combine_ln(pos_blocks, y, gate, shared, h, g, b, alpha):
    T, D = h.shape
    n_steps = T // COMBINE_TILE
    n_rows = TOP_K * COMBINE_TILE
    row = lambda n: pl.BlockSpec((COMBINE_TILE, n), lambda i: (i, 0))
    vec = pl.BlockSpec((1, D), lambda i: (0, 0))
    return pl.pallas_call(
        functools.partial(_combine_kernel, alpha=alpha, n_steps=n_steps),
        grid=(n_steps,),
        in_specs=[
            pl.BlockSpec((1, 1, n_rows), lambda i: (i, 0, 0), memory_space=pltpu.SMEM),
            pl.BlockSpec((1, 1, n_rows), lambda i: (jnp.minimum(i + 1, n_steps - 1), 0, 0),
                         memory_space=pltpu.SMEM),
            pl.BlockSpec(memory_space=pl.ANY),
            row(BLK), row(D), row(D), vec, vec],
        out_specs=[row(D), row(D)],
        out_shape=[jax.ShapeDtypeStruct((T, D), F32), jax.ShapeDtypeStruct((T, D), BF16)],
        scratch_shapes=[pltpu.VMEM((2, n_rows * ROW_PITCH, BLK), F32), pltpu.SemaphoreType.DMA((2,))],
        compiler_params=_cparams(("arbitrary",)),
        name="combine_ln",
    )(pos_blocks, pos_blocks, y, gate, shared, h, g.reshape(1, D), b.reshape(1, D))


def _wprep_kernel(w_ref, o_ref):
    o_ref[:, 0:C_SBQ] = w_ref[0, :, 0:C_SBQ].astype(BF16)
    o_ref[:, C_SBQ:N_MAIN] = w_ref[0, :, O_SB:O_SB + N_MAIN - C_SBQ].astype(BF16)


def _prep_in_proj(w_in_all, layer):
    _, D, n_in = w_in_all.shape
    tr = 256
    return pl.pallas_call(
        _wprep_kernel,
        grid=(D // tr,),
        in_specs=[pl.BlockSpec((1, tr, n_in), lambda i: (layer, i, 0))],
        out_specs=pl.BlockSpec((tr, N_MAIN), lambda i: (i, 0)),
        out_shape=jax.ShapeDtypeStruct((D, N_MAIN), BF16),
        compiler_params=_cparams(("parallel",)),
        name="w_in_prep",
    )(w_in_all)


def _arrange_gates(w_gates, b_igate, b_fgate):
    D = w_gates.shape[0]
    wg = jnp.zeros((D, ML_GROUPS, BLK), F32)
    bias = jnp.zeros((ML_GROUPS, BLK), F32)
    w_i = w_gates[:, :ML_HEADS].reshape(D, ML_GROUPS, ML_HPS)
    w_f = w_gates[:, ML_HEADS:].reshape(D, ML_GROUPS, ML_HPS)
    wg = wg.at[:, :, :ML_HPS].set(w_i).at[:, :, ML_HPS:2 * ML_HPS].set(w_f)
    bias = bias.at[:, :ML_HPS].set(b_igate.reshape(ML_GROUPS, ML_HPS))
    bias = bias.at[:, ML_HPS:2 * ML_HPS].set(b_fgate.reshape(ML_GROUPS, ML_HPS))
    return wg.reshape(D, ML_GROUPS * BLK).astype(BF16), bias.reshape(1, ML_GROUPS * BLK)


def _dispatch_plan(idx, rank, counts, B, Lp):
    T = B * Lp
    A = T * TOP_K
    A_real = B * (Lp - PAD) * TOP_K
    nb = -(-A_real // EXPERT_TILE) + N_EXPERTS
    E = N_EXPERTS
    eids = jnp.arange(E, dtype=jnp.int32)
    padded = (counts + EXPERT_TILE - 1) // EXPERT_TILE * EXPERT_TILE
    ends = jnp.cumsum(padded)
    poff = ends - padded
    off = jnp.cumsum(counts) - counts
    idx_k = idx[:, :, :TOP_K]
    onehot = idx_k[..., None] == eids
    pos = jnp.sum(jnp.where(onehot, poff, 0), axis=-1) + rank[:, :, :TOP_K]
    real = (jnp.arange(Lp) >= PAD)[None, :, None]
    pos = jnp.where(real, pos, 0).astype(jnp.int32)
    keys = jnp.where(real, idx_k, E).reshape(A)
    tok = jnp.broadcast_to(jnp.arange(T, dtype=jnp.int32).reshape(B, Lp, 1), idx_k.shape).reshape(A)
    _, sorted_tok = lax.sort((keys, tok), num_keys=1, is_stable=True)
    n_tok = -(-A // TOK_ALIGN) * TOK_ALIGN + TOK_WINDOW
    sorted_tok = jnp.concatenate([sorted_tok, jnp.full((n_tok - A,), PAD, jnp.int32)])
    blk0 = jnp.arange(nb, dtype=jnp.int32) * EXPERT_TILE
    block_e = jnp.minimum(jnp.sum(ends[None, :] <= blk0[:, None], axis=1), E - 1).astype(jnp.int32)
    sel = block_e[:, None] == eids
    block_start = jnp.sum(jnp.where(sel, off - poff, 0), axis=1) + blk0
    block_start = jnp.clip(block_start, 0, A).astype(jnp.int32)
    n_used = (ends[-1] // EXPERT_TILE).astype(jnp.int32).reshape(1)
    changed = jnp.concatenate([jnp.zeros((1,), jnp.int32),
                               (block_e[1:] != block_e[:-1]).astype(jnp.int32)])
    block_ord = jnp.cumsum(changed).astype(jnp.int32)
    block_next = (jnp.sum(jnp.where(sel, ends, 0), axis=1) // EXPERT_TILE).astype(jnp.int32)
    return pos.reshape(T, TOP_K), sorted_tok, (block_e, block_start, block_ord, block_next, n_used)


def kernel(x, meta, ln_in_g, ln_in_b, w_in, b_igate, b_fgate, conv_qk, ml_norm_g, sb_norm_g, pool_w, pool_scale, w_out, ln1_g, ln1_b, router_w, router_bias, exp_w13, exp_w2, sh_w13, sh_w2, ln2_g, ln2_b):
    B, S, D = x.shape
    depth = w_in.shape[0]
    alpha = (2 * depth) ** 0.25
    Lp = PAD + N_META + S
    T = B * Lp
    assert Lp % BLK == 0 and T % ROW_TILE == 0 and T % COMBINE_TILE == 0

    h0 = jnp.concatenate([jnp.zeros((B, PAD, D), x.dtype),
                          jnp.broadcast_to(meta[None].astype(x.dtype), (B, N_META, D)), x], axis=1)
    h, hb = _ln_rows(h0.reshape(T, D), ln_in_g, ln_in_b)

    for l in range(depth):
        w_gate, gate_bias = _arrange_gates(w_in[l, :, C_SBQ:O_SB], b_igate[l], b_fgate[l])
        proj3 = _in_proj(hb, _prep_in_proj(w_in, l)).reshape(B, Lp, N_MAIN)
        gcol = _gates(hb.reshape(B, Lp, D), w_gate, gate_bias)
        grow = jnp.transpose(gcol.reshape(B, Lp, ML_GROUPS, BLK)[..., :8], (0, 2, 3, 1))
        qk3 = _conv_silu(proj3, conv_qk[l])
        a_ml = _mlstm(qk3, proj3, gcol, grow, ml_norm_g[l])
        a_sb = _stick_breaking(proj3, sb_norm_g[l])
        a_pool = _pool(proj3, pool_w[l], pool_scale[l])
        h1, h1_tiles = _outproj_ln(a_ml.reshape(T, D_ML), a_sb.reshape(T, D_SB), a_pool.reshape(T, D_POOL),
                                   w_out[l].astype(BF16), h, ln1_g[l], ln1_b[l], alpha)

        idx, gate, rank, counts = _router(h1.reshape(B, Lp, D), router_w[l], router_bias[l])
        pos, sorted_tok, tables = _dispatch_plan(idx, rank, counts[0], B, Lp)
        y = _routed_experts(h1_tiles, sorted_tok, *tables, exp_w13, exp_w2, l)
        shared = _shared_expert(h1, sh_w13[l].astype(BF16), sh_w2[l].astype(BF16))
        n_steps = T // COMBINE_TILE
        pos_blocks = jnp.transpose(pos.reshape(n_steps, COMBINE_TILE, TOP_K), (0, 2, 1))
        pos_blocks = pos_blocks.reshape(n_steps, 1, TOP_K * COMBINE_TILE)
        h, hb = _combine_ln(pos_blocks, y, gate.reshape(T, BLK), shared, h1,
                            ln2_g[l], ln2_b[l], alpha)

    return h.reshape(B, Lp, D)[:, PAD + N_META:]
```

```python
import functools

import jax
import jax.numpy as jnp
from jax import lax
from jax.experimental import pallas as pl
from jax.experimental.pallas import tpu as pltpu

N_META = 16
HEAD_DIM = 128
ML_HEADS = 6
SB_HEADS = 6
POOL_WINDOWS = (2, 4, 8, 16)
POOL_GROUPS = 4
D_ML = ML_HEADS * HEAD_DIM
D_SB = SB_HEADS * HEAD_DIM
D_POOL = POOL_GROUPS * HEAD_DIM
CONV_W = 4
N_EXPERTS = 64
TOP_K = 8
ROUTED_SCALE = 2.5
LN_EPS = 1e-5
RMS_EPS = 1e-6

BLK = 128
PAD = (-N_META) % BLK
ML_HPS = 3
ML_GROUPS = ML_HEADS // ML_HPS
SB_KEY_BLOCKS = 4
SB_HPS = 2
SB_W = SB_HPS * HEAD_DIM
ROW_TILE = 512
EXPERT_TILE = 256
COMBINE_TILE = 128
TOK_ALIGN_LOG2 = 10
TOK_ALIGN = 1 << TOK_ALIGN_LOG2
TOK_WINDOW = 2 * TOK_ALIGN
NEG = -1e30
VMEM_LIMIT = 56 * 1024 * 1024

C_QK = 0
C_V = 2 * D_ML
C_O = C_V + D_ML
C_SBQ = C_O + D_ML
C_SBK = C_SBQ + D_SB
C_SBV = C_SBK + D_SB
C_POOL = C_SBV + D_SB
N_MAIN = C_POOL + D_POOL
O_SB = C_SBQ + 2 * ML_HEADS
XROWS = 16
ROW_PITCH = XROWS + 1

F32 = jnp.float32
BF16 = jnp.bfloat16


def _cparams(sem, vmem=VMEM_LIMIT):
    return pltpu.CompilerParams(dimension_semantics=sem, vmem_limit_bytes=vmem)


def _layer_norm(x, g, b):
    mu = jnp.mean(x, axis=-1, keepdims=True)
    xc = x - mu
    var = jnp.mean(xc * xc, axis=-1, keepdims=True)
    return xc * lax.rsqrt(var + LN_EPS) * g + b


def _sigmoid(x):
    return 1.0 / (1.0 + jnp.exp(-x))


def _softplus(x):
    return jnp.maximum(x, 0.0) + jnp.log(1.0 + jnp.exp(-jnp.abs(x)))


def _split_dot(a, b, split_lhs):
    x = a if split_lhs else b
    out = None
    for _ in range(3):
        part = x.astype(BF16)
        x = x - part.astype(F32)
        term = (jnp.dot(part, b, preferred_element_type=F32) if split_lhs
                else jnp.dot(a, part, preferred_element_type=F32))
        out = term if out is None else out + term
    return out


def _ln_kernel(x_ref, g_ref, b_ref, o_ref, ob_ref):
    y = _layer_norm(x_ref[...], g_ref[...], b_ref[...])
    o_ref[...] = y
    ob_ref[...] = y.astype(BF16)


def _ln_rows(x, g, b):
    T, D = x.shape
    row = pl.BlockSpec((ROW_TILE, D), lambda i: (i, 0))
    vec = pl.BlockSpec((1, D), lambda i: (0, 0))
    return pl.pallas_call(
        _ln_kernel,
        grid=(T // ROW_TILE,),
        in_specs=[row, vec, vec],
        out_specs=[row, row],
        out_shape=[jax.ShapeDtypeStruct((T, D), F32), jax.ShapeDtypeStruct((T, D), BF16)],
        compiler_params=_cparams(("parallel",)),
        name="ln_in",
    )(x, g.reshape(1, D), b.reshape(1, D))


def _mm_kernel(a_ref, w_ref, o_ref):
    o_ref[...] = jnp.dot(a_ref[...], w_ref[...], preferred_element_type=F32).astype(o_ref.dtype)


def _in_proj(a, w):
    M, K = a.shape
    N = w.shape[1]
    tn = N // 2
    return pl.pallas_call(
        _mm_kernel,
        grid=(N // tn, M // ROW_TILE),
        in_specs=[pl.BlockSpec((ROW_TILE, K), lambda j, i: (i, 0)),
                  pl.BlockSpec((K, tn), lambda j, i: (0, j))],
        out_specs=pl.BlockSpec((ROW_TILE, tn), lambda j, i: (i, j)),
        out_shape=jax.ShapeDtypeStruct((M, N), BF16),
        compiler_params=_cparams(("parallel", "parallel")),
        name="in_proj",
    )(a, w)


def _gates_kernel(hb_ref, wg_ref, bias_ref, o_ref, *, tm):
    j = pl.program_id(1)
    g = jnp.dot(hb_ref[0], wg_ref[...], preferred_element_type=F32) + bias_ref[...]
    col = jnp.bitwise_and(lax.broadcasted_iota(jnp.int32, g.shape, 1), BLK - 1)
    row = lax.broadcasted_iota(jnp.int32, g.shape, 0) + j * tm
    is_f = jnp.logical_and(col >= ML_HPS, col < 2 * ML_HPS)
    val = jnp.where(is_f, -_softplus(-g), g)
    pad_val = jnp.where(is_f, 0.0, NEG)
    o_ref[0] = jnp.where(row < PAD, pad_val, val)


def _gates(hb3, wg, bias):
    B, Lp, D = hb3.shape
    tm = Lp // 4
    Ng = wg.shape[1]
    return pl.pallas_call(
        functools.partial(_gates_kernel, tm=tm),
        grid=(B, Lp // tm),
        in_specs=[pl.BlockSpec((1, tm, D), lambda b, j: (b, j, 0)),
                  pl.BlockSpec((D, Ng), lambda b, j: (0, 0)),
                  pl.BlockSpec((1, Ng), lambda b, j: (0, 0))],
        out_specs=pl.BlockSpec((1, tm, Ng), lambda b, j: (b, j, 0)),
        out_shape=jax.ShapeDtypeStruct((B, Lp, Ng), F32),
        compiler_params=_cparams(("parallel", "parallel")),
        name="ml_gates",
    )(hb3, wg, bias)


def _conv_kernel(x_ref, w_ref, o_ref):
    c = pl.program_id(1)
    x = x_ref[0].astype(F32)
    row = lax.broadcasted_iota(jnp.int32, x.shape, 0)
    x = jnp.where(row < PAD, 0.0, x)
    w = w_ref[...]
    y = x * w[CONV_W - 1:CONV_W]
    for s in range(1, CONV_W):
        y = y + pltpu.roll(x, s, axis=0) * w[CONV_W - 1 - s:CONV_W - s]
    y = y * _sigmoid(y)
    scale = jnp.where(c < ML_HEADS, HEAD_DIM ** -0.5, 1.0).astype(F32)
    o_ref[0] = (y * scale).astype(BF16)


def _conv_silu(proj3, conv_w):
    B, Lp, _ = proj3.shape
    nc = 2 * D_ML // BLK
    return pl.pallas_call(
        _conv_kernel,
        grid=(B, nc),
        in_specs=[pl.BlockSpec((1, Lp, BLK), lambda b, c: (b, 0, c)),
                  pl.BlockSpec((CONV_W, BLK), lambda b, c: (0, c))],
        out_specs=pl.BlockSpec((1, Lp, BLK), lambda b, c: (b, 0, c)),
        out_shape=jax.ShapeDtypeStruct((B, Lp, 2 * D_ML), BF16),
        compiler_params=_cparams(("parallel", "parallel")),
        name="conv_silu",
    )(proj3, conv_w)


def _mlstm_kernel(q_ref, k_ref, v_ref, o_ref, gc_ref, gr_ref, ng_ref, out_ref, st_ref, m_ref):
    n_chunks = q_ref.shape[1] // BLK
    st_ref[...] = jnp.zeros_like(st_ref)
    m_ref[...] = jnp.zeros_like(m_ref)

    ri = lax.broadcasted_iota(jnp.int32, (BLK, BLK), 0)
    ci = lax.broadcasted_iota(jnp.int32, (BLK, BLK), 1)
    causal = ci <= ri
    lower_incl = jnp.where(causal, 1.0, 0.0).astype(BF16)
    upper_incl = jnp.where(ri <= ci, 1.0, 0.0).astype(BF16)
    ones_col = jnp.where(ci == 0, 1.0, 0.0).astype(BF16)

    def chunk(c, carry):
        r0 = pl.multiple_of(c * BLK, BLK)
        gcol = gc_ref[0, pl.ds(r0, BLK), :]
        grow = gr_ref[0, 0, :, pl.ds(r0, BLK)]
        bcol = _split_dot(lower_incl, gcol, split_lhs=False)
        brow = _split_dot(grow, upper_incl, split_lhs=True)
        for hh in range(ML_HPS):
            cs = slice(hh * HEAD_DIM, (hh + 1) * HEAD_DIM)
            q = q_ref[0, pl.ds(r0, BLK), cs]
            k = k_ref[0, pl.ds(r0, BLK), cs]
            v = v_ref[0, pl.ds(r0, BLK), cs]
            li_c = gcol[:, hh:hh + 1]
            b_c = bcol[:, ML_HPS + hh:ML_HPS + hh + 1]
            li_r = grow[hh:hh + 1, :]
            b_r = brow[ML_HPS + hh:ML_HPS + hh + 1, :]
            m = m_ref[hh, 0:1, 0:1]
            st = st_ref[hh]

            inter = b_c + m
            dmat = jnp.where(causal, b_c - b_r + li_r, NEG)
            m_t = jnp.maximum(inter, jnp.max(dmat, axis=-1, keepdims=True))
            w_inter = jnp.exp(inter - m_t)
            s_qk = lax.dot_general(q, k, (((1,), (1,)), ((), ())), preferred_element_type=F32)
            ws = jnp.exp(dmat - m_t) * s_qk
            v_aug = jnp.concatenate([v, ones_col], axis=1)
            nd = (w_inter * jnp.dot(q, st.astype(BF16), preferred_element_type=F32)
                  + jnp.dot(ws.astype(BF16), v_aug, preferred_element_type=F32))
            num = nd[:, :HEAD_DIM]
            den = nd[:, HEAD_DIM:HEAD_DIM + 1]
            h = num / jnp.maximum(jnp.abs(den), jnp.exp(-m_t))

            og = o_ref[0, pl.ds(r0, BLK), cs].astype(F32)
            hn = h * lax.rsqrt(jnp.mean(h * h, axis=-1, keepdims=True) + RMS_EPS)
            hn = hn * ng_ref[:, cs] * _sigmoid(og)
            out_ref[0, pl.ds(r0, BLK), cs] = hn.astype(BF16)

            b_last = b_r[:, BLK - 1:BLK]
            g_r = b_last - b_r + li_r
            m_new = jnp.maximum(b_last + m, jnp.max(g_r, axis=-1, keepdims=True))
            wg = jnp.exp(b_last - b_c + li_c - m_new)
            decay = jnp.exp(b_last + m - m_new)
            wv = (wg * v_aug.astype(F32)).astype(BF16)
            kv = lax.dot_general(k, wv, (((0,), (0,)), ((), ())), preferred_element_type=F32)
            st_ref[hh] = decay * st + kv
            m_ref[hh] = jnp.broadcast_to(m_new, m_ref.shape[1:])
        return carry

    lax.fori_loop(0, n_chunks, chunk, 0)


def _mlstm(qk3, proj3, gcol, grow, norm_g):
    B, Lp, _ = qk3.shape
    W = ML_HPS * HEAD_DIM
    nq = D_ML // W
    seq = lambda off: pl.BlockSpec((1, Lp, W), lambda b, g: (b, 0, off + g))
    return pl.pallas_call(
        _mlstm_kernel,
        grid=(B, ML_GROUPS),
        in_specs=[seq(0), seq(nq),
                  pl.BlockSpec((1, Lp, W), lambda b, g: (b, 0, C_V // W + g)),
                  pl.BlockSpec((1, Lp, W), lambda b, g: (b, 0, C_O // W + g)),
                  pl.BlockSpec((1, Lp, BLK), lambda b, g: (b, 0, g)),
                  pl.BlockSpec((1, 1, 8, Lp), lambda b, g: (b, g, 0, 0)),
                  pl.BlockSpec((1, W), lambda b, g: (0, g))],
        out_specs=pl.BlockSpec((1, Lp, W), lambda b, g: (b, 0, g)),
        out_shape=jax.ShapeDtypeStruct((B, Lp, D_ML), BF16),
        scratch_shapes=[pltpu.VMEM((ML_HPS, HEAD_DIM, 2 * HEAD_DIM), F32),
                        pltpu.VMEM((ML_HPS, 8, BLK), F32)],
        compiler_params=_cparams(("parallel", "parallel")),
        name="mlstm",
    )(qk3, qk3, proj3, proj3, gcol, grow, norm_g.reshape(1, D_ML))


def _sb_kernel(q_ref, k_ref, v_ref, g_ref, o_ref):
    i = pl.program_id(2)
    W = SB_KEY_BLOCKS * BLK
    r1 = lax.broadcasted_iota(jnp.int32, (BLK, BLK), 0)
    c1 = lax.broadcasted_iota(jnp.int32, (BLK, BLK), 1)
    upper_strict = jnp.where(r1 > c1, 1.0, 0.0).astype(BF16)
    rowp = lax.broadcasted_iota(jnp.int32, (BLK, W), 0) + i * BLK
    ci = lax.broadcasted_iota(jnp.int32, (BLK, W), 1)
    scale = HEAD_DIM ** -0.5
    heads = [slice(hh * HEAD_DIM, (hh + 1) * HEAD_DIM) for hh in range(SB_HPS)]
    qs = [(q_ref[0, :, hs].astype(F32) * scale).astype(BF16) for hs in heads]

    def one_head(q, k, v, carry, acc, valid):
        z = lax.dot_general(q, k, (((1,), (1,)), ((), ())), preferred_element_type=F32)
        sp = _softplus(z)
        lk = jnp.where(valid, -sp, 0.0)
        hi = lk.astype(BF16)
        lo = (lk - hi.astype(F32)).astype(BF16)
        pieces = [None] * SB_KEY_BLOCKS
        off = carry
        for c in reversed(range(SB_KEY_BLOCKS)):
            sl = slice(c * BLK, (c + 1) * BLK)
            local = (jnp.dot(hi[:, sl], upper_strict, preferred_element_type=F32)
                     + jnp.dot(lo[:, sl], upper_strict, preferred_element_type=F32))
            pieces[c] = local + off
            off = off + jnp.sum(lk[:, sl], axis=-1, keepdims=True)
        after = jnp.concatenate(pieces, axis=1)
        a = jnp.where(valid, jnp.exp(z - sp + after), 0.0)
        return off, acc + jnp.dot(a.astype(BF16), v, preferred_element_type=F32)

    def step(t, state):
        end = (i + 1) * BLK - t * W
        s0 = pl.multiple_of(jnp.maximum(end - W, 0), BLK)
        colp = ci + s0
        valid = jnp.logical_and(colp < jnp.minimum(rowp, end), colp >= PAD)
        return tuple(one_head(qs[hh], k_ref[0, pl.ds(s0, W), hs], v_ref[0, pl.ds(s0, W), hs],
                              carry, acc, valid)
                     for hh, (hs, (carry, acc)) in enumerate(zip(heads, state)))

    init = tuple((jnp.zeros((BLK, 1), F32), jnp.zeros((BLK, HEAD_DIM), F32)) for _ in heads)
    n_steps = (i + SB_KEY_BLOCKS) // SB_KEY_BLOCKS
    final = lax.fori_loop(0, n_steps, step, init)
    for hs, (_, acc) in zip(heads, final):
        hn = acc * lax.rsqrt(jnp.mean(acc * acc, axis=-1, keepdims=True) + RMS_EPS) * g_ref[:, hs]
        o_ref[0, :, hs] = hn.astype(BF16)


def _stick_breaking(proj3, norm_g):
    B, Lp, _ = proj3.shape
    nq = Lp // BLK
    return pl.pallas_call(
        _sb_kernel,
        grid=(B, SB_HEADS // SB_HPS, nq),
        in_specs=[pl.BlockSpec((1, BLK, SB_W), lambda b, h, i: (b, i, C_SBQ // SB_W + h)),
                  pl.BlockSpec((1, Lp, SB_W), lambda b, h, i: (b, 0, C_SBK // SB_W + h)),
                  pl.BlockSpec((1, Lp, SB_W), lambda b, h, i: (b, 0, C_SBV // SB_W + h)),
                  pl.BlockSpec((1, SB_W), lambda b, h, i: (0, h))],
        out_specs=pl.BlockSpec((1, BLK, SB_W), lambda b, h, i: (b, i, h)),
        out_shape=jax.ShapeDtypeStruct((B, Lp, D_SB), BF16),
        compiler_params=_cparams(("parallel", "parallel", "parallel")),
        name="stick_breaking",
    )(proj3, proj3, proj3, norm_g.reshape(1, D_SB))


def _pool_kernel(x_ref, w_ref, s_ref, o_ref):
    g = pl.program_id(1)
    x = x_ref[0].astype(F32)
    row = lax.broadcasted_iota(jnp.int32, x.shape, 0)
    x = jnp.where(row < PAD, 0.0, x)
    pos1 = (row - PAD + 1).astype(F32)
    sums = x
    mean = jnp.zeros_like(x)
    shift = 1
    for gi, win in enumerate(POOL_WINDOWS):
        while shift < win:
            sums = sums + pltpu.roll(sums, shift, axis=0)
            shift *= 2
        cnt = jnp.maximum(jnp.minimum(pos1, float(win)), 1.0)
        mean = jnp.where(g == gi, sums / cnt, mean)
    p = (mean - x).astype(BF16)
    y = jnp.dot(p, w_ref[0].astype(BF16), preferred_element_type=F32) * s_ref[...]
    o_ref[0] = y.astype(BF16)


def _pool(proj3, pool_w, pool_scale):
    B, Lp, _ = proj3.shape
    return pl.pallas_call(
        _pool_kernel,
        grid=(B, POOL_GROUPS),
        in_specs=[pl.BlockSpec((1, Lp, HEAD_DIM), lambda b, g: (b, 0, C_POOL // HEAD_DIM + g)),
                  pl.BlockSpec((1, HEAD_DIM, HEAD_DIM), lambda b, g: (g, 0, 0)),
                  pl.BlockSpec((1, HEAD_DIM), lambda b, g: (0, g))],
        out_specs=pl.BlockSpec((1, Lp, HEAD_DIM), lambda b, g: (b, 0, g)),
        out_shape=jax.ShapeDtypeStruct((B, Lp, D_POOL), BF16),
        compiler_params=_cparams(("parallel", "parallel")),
        name="pool_mixer",
    )(proj3, pool_w, pool_scale.reshape(1, D_POOL))


def _to_row_tiles(ref, y):
    for s in range(XROWS):
        ref[:, s, :] = y[:, s * BLK:(s + 1) * BLK]


def _from_pitched(ref, slot, n, first=0):
    return jnp.concatenate(
        [ref[slot, pl.ds(first * ROW_PITCH + s, n, stride=ROW_PITCH), :] for s in range(XROWS)], axis=1)


def _outproj_kernel(a1_ref, a2_ref, a3_ref, w_ref, h_ref, g_ref, b_ref, o_ref, ot_ref, *, alpha):
    acc = jnp.dot(a1_ref[...], w_ref[0:D_ML, :], preferred_element_type=F32)
    acc = acc + jnp.dot(a2_ref[...], w_ref[D_ML:D_ML + D_SB, :], preferred_element_type=F32)
    acc = acc + jnp.dot(a3_ref[...], w_ref[D_ML + D_SB:, :], preferred_element_type=F32)
    y = _layer_norm(alpha * h_ref[...] + acc, g_ref[...], b_ref[...])
    o_ref[...] = y
    _to_row_tiles(ot_ref, y)


def _outproj_ln(a_ml, a_sb, a_pool, w, h, g, b, alpha):
    T, D = h.shape
    row = lambda n: pl.BlockSpec((ROW_TILE, n), lambda i: (i, 0))
    vec = pl.BlockSpec((1, D), lambda i: (0, 0))
    return pl.pallas_call(
        functools.partial(_outproj_kernel, alpha=alpha),
        grid=(T // ROW_TILE,),
        in_specs=[row(D_ML), row(D_SB), row(D_POOL),
                  pl.BlockSpec(w.shape, lambda i: (0, 0)), row(D), vec, vec],
        out_specs=[row(D), pl.BlockSpec((ROW_TILE, XROWS, BLK), lambda i: (i, 0, 0))],
        out_shape=[jax.ShapeDtypeStruct((T, D), F32), jax.ShapeDtypeStruct((T, XROWS, BLK), F32)],
        compiler_params=_cparams(("parallel",)),
        name="out_proj_ln",
    )(a_ml, a_sb, a_pool, w, h, g.reshape(1, D), b.reshape(1, D))


def _router_kernel(h_ref, rw_ref, rb_ref, idx_ref, gate_ref, rank_ref, cnt_ref, run_ref, *, tm):
    first = jnp.logical_and(pl.program_id(0) == 0, pl.program_id(1) == 0)

    @pl.when(first)
    def _():
        run_ref[...] = jnp.zeros_like(run_ref)

    x = h_ref[0]
    logits = jnp.dot(x, rw_ref[...], preferred_element_type=F32, precision=lax.Precision.HIGHEST)
    scores = _sigmoid(logits)
    work = scores + rb_ref[...]
    E = scores.shape[1]
    lane = lax.broadcasted_iota(jnp.int32, (tm, E), 1).astype(F32)
    row = lax.broadcasted_iota(jnp.int32, (tm, 1), 0) + pl.program_id(1) * tm
    real = (row >= PAD).astype(F32)

    picks = []
    gates = []
    member = jnp.zeros((tm, E), F32)
    for _ in range(TOP_K):
        mx = jnp.max(work, axis=-1, keepdims=True)
        pick = jnp.min(jnp.where(work == mx, lane, float(E)), axis=-1, keepdims=True)
        onehot = lane == pick
        gates.append(jnp.sum(jnp.where(onehot, scores, 0.0), axis=-1, keepdims=True))
        picks.append(pick)
        member = member + jnp.where(onehot, real, 0.0)
        work = jnp.where(onehot, -jnp.inf, work)
    gsum = gates[0]
    for gk in gates[1:]:
        gsum = gsum + gk

    ri = lax.broadcasted_iota(jnp.int32, (tm, tm), 0)
    ci = lax.broadcasted_iota(jnp.int32, (tm, tm), 1)
    lower_strict = jnp.where(ci < ri, 1.0, 0.0).astype(BF16)
    before = jnp.dot(lower_strict, member.astype(BF16), preferred_element_type=F32) + run_ref[...]

    lane_o = lax.broadcasted_iota(jnp.int32, (tm, BLK), 1)
    idx_o = jnp.zeros((tm, BLK), F32)
    gate_o = jnp.zeros((tm, BLK), F32)
    rank_o = jnp.zeros((tm, BLK), F32)
    for kk in range(TOP_K):
        onehot = lane == picks[kk]
        rk = jnp.sum(jnp.where(onehot, before, 0.0), axis=-1, keepdims=True)
        sel = lane_o == kk
        idx_o = jnp.where(sel, picks[kk], idx_o)
        gate_o = jnp.where(sel, gates[kk] / gsum * ROUTED_SCALE, gate_o)
        rank_o = jnp.where(sel, rk, rank_o)
    idx_ref[0] = idx_o.astype(jnp.int32)
    gate_ref[0] = gate_o
    rank_ref[0] = rank_o.astype(jnp.int32)
    run_ref[...] = run_ref[...] + jnp.sum(member, axis=0, keepdims=True)
    cnt_ref[...] = run_ref[...].astype(jnp.int32)


def _router(h3, router_w, router_bias):
    B, Lp, D = h3.shape
    E = router_w.shape[1]
    tm = Lp // 4
    out = lambda: pl.BlockSpec((1, tm, BLK), lambda b, j: (b, j, 0))
    return pl.pallas_call(
        functools.partial(_router_kernel, tm=tm),
        grid=(B, Lp // tm),
        in_specs=[pl.BlockSpec((1, tm, D), lambda b, j: (b, j, 0)),
                  pl.BlockSpec((D, E), lambda b, j: (0, 0)),
                  pl.BlockSpec((1, E), lambda b, j: (0, 0))],
        out_specs=[out(), out(), out(), pl.BlockSpec((1, E), lambda b, j: (0, 0))],
        out_shape=[jax.ShapeDtypeStruct((B, Lp, BLK), jnp.int32),
                   jax.ShapeDtypeStruct((B, Lp, BLK), F32),
                   jax.ShapeDtypeStruct((B, Lp, BLK), jnp.int32),
                   jax.ShapeDtypeStruct((1, E), jnp.int32)],
        scratch_shapes=[pltpu.VMEM((1, E), F32)],
        compiler_params=_cparams(("arbitrary", "arbitrary")),
        name="router",
    )(h3, router_w, router_bias.reshape(1, E))


def _swiglu(x, w13, w2):
    h1 = jnp.dot(x, w13, preferred_element_type=F32)
    f = h1.shape[1] // 2
    gate, up = h1[:, :f], h1[:, f:]
    act = (gate * _sigmoid(gate) * up).astype(BF16)
    return jnp.dot(act, w2, preferred_element_type=F32)


def _shared_kernel(x_ref, w13_ref, w2_ref, o_ref):
    o_ref[...] = _swiglu(x_ref[...].astype(BF16), w13_ref[...], w2_ref[...])


def _shared_expert(h, w13, w2):
    T, D = h.shape
    row = pl.BlockSpec((ROW_TILE, D), lambda i: (i, 0))
    return pl.pallas_call(
        _shared_kernel,
        grid=(T // ROW_TILE,),
        in_specs=[row, pl.BlockSpec(w13.shape, lambda i: (0, 0)),
                  pl.BlockSpec(w2.shape, lambda i: (0, 0))],
        out_specs=row,
        out_shape=jax.ShapeDtypeStruct((T, D), F32),
        compiler_params=_cparams(("parallel",)),
        name="shared_expert",
    )(h, w13, w2)


def _moe_kernel(be_ref, bs_ref, bo_ref, bn_ref, nb_ref, tok_hbm, x_hbm, w13_hbm, w2_hbm, y_ref,
                win0, win1, wsem, xbuf, xsem, wf13, wf2, fsem, w13b, w2b, *, layer):
    i = pl.program_id(0)
    n_used = nb_ref[0]
    slot = lax.rem(i, 2)
    F = w2b.shape[0]
    wins = (win0, win1)

    def by_parity(p, fn):
        for s in range(2):
            pl.when(p == s)(functools.partial(fn, s))

    def window(blk, s):
        base = pl.multiple_of(lax.shift_left(lax.shift_right_logical(bs_ref[blk], TOK_ALIGN_LOG2),
                                             TOK_ALIGN_LOG2), TOK_ALIGN)
        return pltpu.make_async_copy(tok_hbm.at[pl.ds(base, TOK_WINDOW)], wins[s], wsem.at[s])

    def weights(e, s):
        return (pltpu.make_async_copy(w13_hbm.at[layer, e], wf13.at[s], fsem.at[0, s]),
                pltpu.make_async_copy(w2_hbm.at[layer, e], wf2.at[s], fsem.at[1, s]))

    def row_copy(tok, s, r):
        return pltpu.make_async_copy(x_hbm.at[tok], xbuf.at[s, pl.ds(r * ROW_PITCH, XROWS), :], xsem.at[s])

    def rows_wait(s):
        rows = x_hbm.at[pl.ds(0, EXPERT_TILE)]
        pltpu.make_async_copy(rows, rows, xsem.at[s]).wait()

    @pl.when(jnp.logical_and(i == 0, n_used > 0))
    def _():
        for c in weights(be_ref[0], 0):
            c.start()
        window(0, 0).start()
        window(0, 0).wait()
        first0 = jnp.bitwise_and(bs_ref[0], TOK_ALIGN - 1)

        def body(r, c):
            row_copy(win0[first0 + r], 0, r).start()
            return c
        lax.fori_loop(0, EXPERT_TILE, body, 0)

        @pl.when(n_used > 1)
        def _():
            window(1, 1).start()

    @pl.when(i + 1 < n_used)
    def _():
        by_parity(1 - slot, lambda s: window(i + 1, s).wait())

    @pl.when(i + 2 < n_used)
    def _():
        by_parity(slot, lambda s: window(i + 2, s).start())

    @pl.when(i < n_used)
    def _():
        e = be_ref[i]
        ws = lax.rem(bo_ref[i], 2)

        @pl.when(jnp.logical_or(i == 0, e != be_ref[jnp.maximum(i - 1, 0)]))
        def _():
            for c in weights(e, ws):
                c.wait()
            nxt = bn_ref[i]

            @pl.when(nxt < n_used)
            def _():
                for c in weights(be_ref[nxt], 1 - ws):
                    c.start()
            w13b[...] = wf13[ws].astype(BF16)
            w2b[...] = wf2[ws].astype(BF16)

        nblk = jnp.minimum(i + 1, n_used - 1)
        first = jnp.bitwise_and(bs_ref[nblk], TOK_ALIGN - 1)

        def issue(s):
            for r in range(EXPERT_TILE):
                row_copy(wins[s][first + r], 1 - slot, r).start(priority=r % 2)
        by_parity(lax.rem(nblk, 2), issue)

        rows_wait(slot)
        x = _from_pitched(xbuf, slot, EXPERT_TILE).astype(BF16)
        acts = []
        half = F // 2
        for c in range(2):
            gate = jnp.dot(x, w13b[:, c * half:(c + 1) * half], preferred_element_type=F32)
            up = jnp.dot(x, w13b[:, F + c * half:F + (c + 1) * half], preferred_element_type=F32)
            acts.append((gate * _sigmoid(gate) * up).astype(BF16))
        act = jnp.concatenate(acts, axis=1)
        for c in range(XROWS // 4):
            y = jnp.dot(act, w2b[:, c * 4 * BLK:(c + 1) * 4 * BLK], preferred_element_type=F32)
            for s in range(4):
                y_ref[pl.ds(c * 4 + s, EXPERT_TILE, stride=ROW_PITCH), :] = y[:, s * BLK:(s + 1) * BLK]
        y_ref[pl.ds(XROWS, EXPERT_TILE, stride=ROW_PITCH), :] = jnp.zeros((EXPERT_TILE, BLK), F32)

        @pl.when(i == n_used - 1)
        def _():
            rows_wait(1 - slot)

    @pl.when(i >= n_used)
    def _():
        y_ref[...] = jnp.zeros_like(y_ref)


def _routed_experts(x_tiles, sorted_tok, block_e, block_start, block_ord, block_next, n_used, w13, w2, layer):
    nb = block_e.shape[0]
    _, E, D, F2 = w13.shape
    F = F2 // 2
    grid_spec = pltpu.PrefetchScalarGridSpec(
        num_scalar_prefetch=5,
        grid=(nb,),
        in_specs=[pl.BlockSpec(memory_space=pl.ANY)] * 4,
        out_specs=pl.BlockSpec((EXPERT_TILE * ROW_PITCH, BLK), lambda i, *_: (i, 0)),
        scratch_shapes=[pltpu.SMEM((TOK_WINDOW,), jnp.int32),
                        pltpu.SMEM((TOK_WINDOW,), jnp.int32),
                        pltpu.SemaphoreType.DMA((2,)),
                        pltpu.VMEM((2, EXPERT_TILE * ROW_PITCH, BLK), F32),
                        pltpu.SemaphoreType.DMA((2,)),
                        pltpu.VMEM((2, D, F2), F32),
                        pltpu.VMEM((2, F, D), F32),
                        pltpu.SemaphoreType.DMA((2, 2)),
                        pltpu.VMEM((D, F2), BF16),
                        pltpu.VMEM((F, D), BF16)],
    )
    return pl.pallas_call(
        functools.partial(_moe_kernel, layer=layer),
        grid_spec=grid_spec,
        out_shape=jax.ShapeDtypeStruct((nb * EXPERT_TILE * ROW_PITCH, BLK), F32),
        compiler_params=_cparams(("arbitrary",)),
        name="routed_experts",
    )(block_e, block_start, block_ord, block_next, n_used, sorted_tok, x_tiles, w13, w2)


def _combine_kernel(pos_ref, posn_ref, y_hbm, gate_ref, sh_ref, h_ref, g_ref, b_ref,
                    o_ref, ob_ref, gbuf, sem, *, alpha, n_steps):
    i = pl.program_id(0)
    slot = lax.rem(i, 2)
    n_rows = TOP_K * COMBINE_TILE

    def copy(pos, s, r):
        return pltpu.make_async_copy(y_hbm.at[pl.ds(pos * ROW_PITCH, XROWS), :],
                                     gbuf.at[s, pl.ds(r * ROW_PITCH, XROWS), :], sem.at[s])

    def gather(pref, s):
        def body(r8, c):
            for u in range(8):
                r = r8 * 8 + u
                copy(pref[0, 0, r], s, r).start(priority=u % 2)
            return c
        lax.fori_loop(0, n_rows // 8, body, 0, unroll=2)

    @pl.when(i == 0)
    def _():
        gather(pos_ref, 0)

    @pl.when(i + 1 < n_steps)
    def _():
        gather(posn_ref, 1 - slot)

    all_rows = y_hbm.at[pl.ds(0, n_rows * XROWS), :]
    pltpu.make_async_copy(all_rows, all_rows, sem.at[slot]).wait()

    gate = gate_ref[...]
    acc = alpha * h_ref[...] + sh_ref[...]
    for kk in range(TOP_K):
        rows = _from_pitched(gbuf, slot, COMBINE_TILE, first=kk * COMBINE_TILE)
        acc = acc + gate[:, kk:kk + 1] * rows
    y = _layer_norm(acc, g_ref[...], b_ref[...])
    o_ref[...] = y
    ob_ref[...] = y.astype(BF16)


def _combine_ln(pos_blocks, y, gate, shared, h, g, b, alpha):
    T, D = h.shape
    n_steps = T // COMBINE_TILE
    n_rows = TOP_K * COMBINE_TILE
    row = lambda n: pl.BlockSpec((COMBINE_TILE, n), lambda i: (i, 0))
    vec = pl.BlockSpec((1, D), lambda i: (0, 0))
    return pl.pallas_call(
        functools.partial(_combine_kernel, alpha=alpha, n_steps=n_steps),
        grid=(n_steps,),
        in_specs=[
            pl.BlockSpec((1, 1, n_rows), lambda i: (i, 0, 0), memory_space=pltpu.SMEM),
            pl.BlockSpec((1, 1, n_rows), lambda i: (jnp.minimum(i + 1, n_steps - 1), 0, 0),
                         memory_space=pltpu.SMEM),
            pl.BlockSpec(memory_space=pl.ANY),
            row(BLK), row(D), row(D), vec, vec],
        out_specs=[row(D), row(D)],
        out_shape=[jax.ShapeDtypeStruct((T, D), F32), jax.ShapeDtypeStruct((T, D), BF16)],
        scratch_shapes=[pltpu.VMEM((2, n_rows * ROW_PITCH, BLK), F32), pltpu.SemaphoreType.DMA((2,))],
        compiler_params=_cparams(("arbitrary",)),
        name="combine_ln",
    )(pos_blocks, pos_blocks, y, gate, shared, h, g.reshape(1, D), b.reshape(1, D))


def _wprep_kernel(w_ref, o_ref):
    o_ref[:, 0:C_SBQ] = w_ref[0, :, 0:C_SBQ].astype(BF16)
    o_ref[:, C_SBQ:N_MAIN] = w_ref[0, :, O_SB:O_SB + N_MAIN - C_SBQ].astype(BF16)


def _prep_in_proj(w_in_all, layer):
    _, D, n_in = w_in_all.shape
    tr = 256
    return pl.pallas_call(
        _wprep_kernel,
        grid=(D // tr,),
        in_specs=[pl.BlockSpec((1, tr, n_in), lambda i: (layer, i, 0))],
        out_specs=pl.BlockSpec((tr, N_MAIN), lambda i: (i, 0)),
        out_shape=jax.ShapeDtypeStruct((D, N_MAIN), BF16),
        compiler_params=_cparams(("parallel",)),
        name="w_in_prep",
    )(w_in_all)


def _arrange_gates(w_gates, b_igate, b_fgate):
    D = w_gates.shape[0]
    wg = jnp.zeros((D, ML_GROUPS, BLK), F32)
    bias = jnp.zeros((ML_GROUPS, BLK), F32)
    w_i = w_gates[:, :ML_HEADS].reshape(D, ML_GROUPS, ML_HPS)
    w_f = w_gates[:, ML_HEADS:].reshape(D, ML_GROUPS, ML_HPS)
    wg = wg.at[:, :, :ML_HPS].set(w_i).at[:, :, ML_HPS:2 * ML_HPS].set(w_f)
    bias = bias.at[:, :ML_HPS].set(b_igate.reshape(ML_GROUPS, ML_HPS))
    bias = bias.at[:, ML_HPS:2 * ML_HPS].set(b_fgate.reshape(ML_GROUPS, ML_HPS))
    return wg.reshape(D, ML_GROUPS * BLK).astype(BF16), bias.reshape(1, ML_GROUPS * BLK)


def _dispatch_plan(idx, rank, counts, B, Lp):
    T = B * Lp
    A = T * TOP_K
    A_real = B * (Lp - PAD) * TOP_K
    nb = -(-A_real // EXPERT_TILE) + N_EXPERTS
    E = N_EXPERTS
    eids = jnp.arange(E, dtype=jnp.int32)
    padded = (counts + EXPERT_TILE - 1) // EXPERT_TILE * EXPERT_TILE
    ends = jnp.cumsum(padded)
    poff = ends - padded
    off = jnp.cumsum(counts) - counts
    idx_k = idx[:, :, :TOP_K]
    onehot = idx_k[..., None] == eids
    pos = jnp.sum(jnp.where(onehot, poff, 0), axis=-1) + rank[:, :, :TOP_K]
    real = (jnp.arange(Lp) >= PAD)[None, :, None]
    pos = jnp.where(real, pos, 0).astype(jnp.int32)
    keys = jnp.where(real, idx_k, E).reshape(A)
    tok = jnp.broadcast_to(jnp.arange(T, dtype=jnp.int32).reshape(B, Lp, 1), idx_k.shape).reshape(A)
    _, sorted_tok = lax.sort((keys, tok), num_keys=1, is_stable=True)
    n_tok = -(-A // TOK_ALIGN) * TOK_ALIGN + TOK_WINDOW
    sorted_tok = jnp.concatenate([sorted_tok, jnp.full((n_tok - A,), PAD, jnp.int32)])
    blk0 = jnp.arange(nb, dtype=jnp.int32) * EXPERT_TILE
    block_e = jnp.minimum(jnp.sum(ends[None, :] <= blk0[:, None], axis=1), E - 1).astype(jnp.int32)
    sel = block_e[:, None] == eids
    block_start = jnp.sum(jnp.where(sel, off - poff, 0), axis=1) + blk0
    block_start = jnp.clip(block_start, 0, A).astype(jnp.int32)
    n_used = (ends[-1] // EXPERT_TILE).astype(jnp.int32).reshape(1)
    changed = jnp.concatenate([jnp.zeros((1,), jnp.int32),
                               (block_e[1:] != block_e[:-1]).astype(jnp.int32)])
    block_ord = jnp.cumsum(changed).astype(jnp.int32)
    block_next = (jnp.sum(jnp.where(sel, ends, 0), axis=1) // EXPERT_TILE).astype(jnp.int32)
    return pos.reshape(T, TOP_K), sorted_tok, (block_e, block_start, block_ord, block_next, n_used)


def kernel(x, meta, ln_in_g, ln_in_b, w_in, b_igate, b_fgate, conv_qk, ml_norm_g, sb_norm_g, pool_w, pool_scale, w_out, ln1_g, ln1_b, router_w, router_bias, exp_w13, exp_w2, sh_w13, sh_w2, ln2_g, ln2_b):
    B, S, D = x.shape
    depth = w_in.shape[0]
    alpha = (2 * depth) ** 0.25
    Lp = PAD + N_META + S
    T = B * Lp
    assert Lp % BLK == 0 and T % ROW_TILE == 0 and T % COMBINE_TILE == 0

    h0 = jnp.concatenate([jnp.zeros((B, PAD, D), x.dtype),
                          jnp.broadcast_to(meta[None].astype(x.dtype), (B, N_META, D)), x], axis=1)
    h, hb = _ln_rows(h0.reshape(T, D), ln_in_g, ln_in_b)

    for l in range(depth):
        w_gate, gate_bias = _arrange_gates(w_in[l, :, C_SBQ:O_SB], b_igate[l], b_fgate[l])
        proj3 = _in_proj(hb, _prep_in_proj(w_in, l)).reshape(B, Lp, N_MAIN)
        gcol = _gates(hb.reshape(B, Lp, D), w_gate, gate_bias)
        grow = jnp.transpose(gcol.reshape(B, Lp, ML_GROUPS, BLK)[..., :8], (0, 2, 3, 1))
        qk3 = _conv_silu(proj3, conv_qk[l])
        a_ml = _mlstm(qk3, proj3, gcol, grow, ml_norm_g[l])
        a_sb = _stick_breaking(proj3, sb_norm_g[l])
        a_pool = _pool(proj3, pool_w[l], pool_scale[l])
        h1, h1_tiles = _outproj_ln(a_ml.reshape(T, D_ML), a_sb.reshape(T, D_SB), a_pool.reshape(T, D_POOL),
                                   w_out[l].astype(BF16), h, ln1_g[l], ln1_b[l], alpha)

        idx, gate, rank, counts = _router(h1.reshape(B, Lp, D), router_w[l], router_bias[l])
        pos, sorted_tok, tables = _dispatch_plan(idx, rank, counts[0], B, Lp)
        y = _routed_experts(h1_tiles, sorted_tok, *tables, exp_w13, exp_w2, l)
        shared = _shared_expert(h1, sh_w13[l].astype(BF16), sh_w2[l].astype(BF16))
        n_steps = T // COMBINE_TILE
        pos_blocks = jnp.transpose(pos.reshape(n_steps, COMBINE_TILE, TOP_K), (0, 2, 1))
        pos_blocks = pos_blocks.reshape(n_steps, 1, TOP_K * COMBINE_TILE)
        h, hb = _combine_ln(pos_blocks, y, gate.reshape(T, BLK), shared, h1,
                            ln2_g[l], ln2_b[l], alpha)

    return h.reshape(B, Lp, D)[:, PAD + N_META:]
```

```python
import functools

import jax
import jax.numpy as jnp
from jax import lax
from jax.experimental import pallas as pl
from jax.experimental.pallas import tpu as pltpu

N_META = 16
HEAD_DIM = 128
ML_HEADS = 6
SB_HEADS = 6
POOL_WINDOWS = (2, 4, 8, 16)
POOL_GROUPS = 4
D_ML = ML_HEADS * HEAD_DIM
D_SB = SB_HEADS * HEAD_DIM
D_POOL = POOL_GROUPS * HEAD_DIM
CONV_W = 4
N_EXPERTS = 64
TOP_K = 8
ROUTED_SCALE = 2.5
LN_EPS = 1e-5
RMS_EPS = 1e-6

BLK = 128
PAD = (-N_META) % BLK
ML_HPS = 3
ML_GROUPS = ML_HEADS // ML_HPS
SB_KEY_BLOCKS = 4
SB_HPS = 3
SB_W = SB_HPS * HEAD_DIM
ROW_TILE = 512
EXPERT_TILE = 256
COMBINE_TILE = 128
TOK_ALIGN_LOG2 = 10
TOK_ALIGN = 1 << TOK_ALIGN_LOG2
TOK_WINDOW = 2 * TOK_ALIGN
NEG = -1e30
VMEM_LIMIT = 56 * 1024 * 1024

C_QK = 0
C_V = 2 * D_ML
C_O = C_V + D_ML
C_SBQ = C_O + D_ML
C_SBK = C_SBQ + D_SB
C_SBV = C_SBK + D_SB
C_POOL = C_SBV + D_SB
N_MAIN = C_POOL + D_POOL
O_SB = C_SBQ + 2 * ML_HEADS
XROWS = 16
ROW_PITCH = XROWS + 1

F32 = jnp.float32
BF16 = jnp.bfloat16


def _cparams(sem, vmem=VMEM_LIMIT):
    return pltpu.CompilerParams(dimension_semantics=sem, vmem_limit_bytes=vmem)


def _layer_norm(x, g, b):
    mu = jnp.mean(x, axis=-1, keepdims=True)
    xc = x - mu
    var = jnp.mean(xc * xc, axis=-1, keepdims=True)
    return xc * lax.rsqrt(var + LN_EPS) * g + b


def _sigmoid(x):
    return 1.0 / (1.0 + jnp.exp(-x))


def _softplus(x):
    return jnp.maximum(x, 0.0) + jnp.log(1.0 + jnp.exp(-jnp.abs(x)))


def _split_dot(a, b, split_lhs):
    x = a if split_lhs else b
    out = None
    for _ in range(3):
        part = x.astype(BF16)
        x = x - part.astype(F32)
        term = (jnp.dot(part, b, preferred_element_type=F32) if split_lhs
                else jnp.dot(a, part, preferred_element_type=F32))
        out = term if out is None else out + term
    return out


def _ln_kernel(x_ref, g_ref, b_ref, o_ref, ob_ref):
    y = _layer_norm(x_ref[...], g_ref[...], b_ref[...])
    o_ref[...] = y
    ob_ref[...] = y.astype(BF16)


def _ln_rows(x, g, b):
    T, D = x.shape
    row = pl.BlockSpec((ROW_TILE, D), lambda i: (i, 0))
    vec = pl.BlockSpec((1, D), lambda i: (0, 0))
    return pl.pallas_call(
        _ln_kernel,
        grid=(T // ROW_TILE,),
        in_specs=[row, vec, vec],
        out_specs=[row, row],
        out_shape=[jax.ShapeDtypeStruct((T, D), F32), jax.ShapeDtypeStruct((T, D), BF16)],
        compiler_params=_cparams(("parallel",)),
        name="ln_in",
    )(x, g.reshape(1, D), b.reshape(1, D))


def _mm_kernel(a_ref, w_ref, o_ref):
    o_ref[...] = jnp.dot(a_ref[...], w_ref[...], preferred_element_type=F32).astype(o_ref.dtype)


def _in_proj(a, w):
    M, K = a.shape
    N = w.shape[1]
    tn = N // 2
    return pl.pallas_call(
        _mm_kernel,
        grid=(N // tn, M // ROW_TILE),
        in_specs=[pl.BlockSpec((ROW_TILE, K), lambda j, i: (i, 0)),
                  pl.BlockSpec((K, tn), lambda j, i: (0, j))],
        out_specs=pl.BlockSpec((ROW_TILE, tn), lambda j, i: (i, j)),
        out_shape=jax.ShapeDtypeStruct((M, N), BF16),
        compiler_params=_cparams(("parallel", "parallel")),
        name="in_proj",
    )(a, w)


def _gates_kernel(hb_ref, wg_ref, bias_ref, o_ref, *, tm):
    j = pl.program_id(1)
    g = jnp.dot(hb_ref[0], wg_ref[...], preferred_element_type=F32) + bias_ref[...]
    col = jnp.bitwise_and(lax.broadcasted_iota(jnp.int32, g.shape, 1), BLK - 1)
    row = lax.broadcasted_iota(jnp.int32, g.shape, 0) + j * tm
    is_f = jnp.logical_and(col >= ML_HPS, col < 2 * ML_HPS)
    val = jnp.where(is_f, -_softplus(-g), g)
    pad_val = jnp.where(is_f, 0.0, NEG)
    o_ref[0] = jnp.where(row < PAD, pad_val, val)


def _gates(hb3, wg, bias):
    B, Lp, D = hb3.shape
    tm = Lp // 4
    Ng = wg.shape[1]
    return pl.pallas_call(
        functools.partial(_gates_kernel, tm=tm),
        grid=(B, Lp // tm),
        in_specs=[pl.BlockSpec((1, tm, D), lambda b, j: (b, j, 0)),
                  pl.BlockSpec((D, Ng), lambda b, j: (0, 0)),
                  pl.BlockSpec((1, Ng), lambda b, j: (0, 0))],
        out_specs=pl.BlockSpec((1, tm, Ng), lambda b, j: (b, j, 0)),
        out_shape=jax.ShapeDtypeStruct((B, Lp, Ng), F32),
        compiler_params=_cparams(("parallel", "parallel")),
        name="ml_gates",
    )(hb3, wg, bias)


def _conv_kernel(x_ref, w_ref, o_ref):
    c = pl.program_id(1)
    x = x_ref[0].astype(F32)
    row = lax.broadcasted_iota(jnp.int32, x.shape, 0)
    x = jnp.where(row < PAD, 0.0, x)
    w = w_ref[...]
    y = x * w[CONV_W - 1:CONV_W]
    for s in range(1, CONV_W):
        y = y + pltpu.roll(x, s, axis=0) * w[CONV_W - 1 - s:CONV_W - s]
    y = y * _sigmoid(y)
    scale = jnp.where(c < ML_HEADS, HEAD_DIM ** -0.5, 1.0).astype(F32)
    o_ref[0] = (y * scale).astype(BF16)


def _conv_silu(proj3, conv_w):
    B, Lp, _ = proj3.shape
    nc = 2 * D_ML // BLK
    return pl.pallas_call(
        _conv_kernel,
        grid=(B, nc),
        in_specs=[pl.BlockSpec((1, Lp, BLK), lambda b, c: (b, 0, c)),
                  pl.BlockSpec((CONV_W, BLK), lambda b, c: (0, c))],
        out_specs=pl.BlockSpec((1, Lp, BLK), lambda b, c: (b, 0, c)),
        out_shape=jax.ShapeDtypeStruct((B, Lp, 2 * D_ML), BF16),
        compiler_params=_cparams(("parallel", "parallel")),
        name="conv_silu",
    )(proj3, conv_w)


def _mlstm_kernel(q_ref, k_ref, v_ref, o_ref, gc_ref, gr_ref, ng_ref, out_ref, st_ref, m_ref):
    n_chunks = q_ref.shape[1] // BLK
    st_ref[...] = jnp.zeros_like(st_ref)
    m_ref[...] = jnp.zeros_like(m_ref)

    ri = lax.broadcasted_iota(jnp.int32, (BLK, BLK), 0)
    ci = lax.broadcasted_iota(jnp.int32, (BLK, BLK), 1)
    causal = ci <= ri
    lower_incl = jnp.where(causal, 1.0, 0.0).astype(BF16)
    upper_incl = jnp.where(ri <= ci, 1.0, 0.0).astype(BF16)
    ones_col = jnp.where(ci == 0, 1.0, 0.0).astype(BF16)

    def chunk(c, carry):
        r0 = pl.multiple_of(c * BLK, BLK)
        gcol = gc_ref[0, pl.ds(r0, BLK), :]
        grow = gr_ref[0, 0, :, pl.ds(r0, BLK)]
        bcol = _split_dot(lower_incl, gcol, split_lhs=False)
        brow = _split_dot(grow, upper_incl, split_lhs=True)
        for hh in range(ML_HPS):
            cs = slice(hh * HEAD_DIM, (hh + 1) * HEAD_DIM)
            q = q_ref[0, pl.ds(r0, BLK), cs]
            k = k_ref[0, pl.ds(r0, BLK), cs]
            v = v_ref[0, pl.ds(r0, BLK), cs]
            li_c = gcol[:, hh:hh + 1]
            b_c = bcol[:, ML_HPS + hh:ML_HPS + hh + 1]
            li_r = grow[hh:hh + 1, :]
            b_r = brow[ML_HPS + hh:ML_HPS + hh + 1, :]
            m = m_ref[hh, 0:1, 0:1]
            st = st_ref[hh]

            inter = b_c + m
            dmat = jnp.where(causal, b_c - b_r + li_r, NEG)
            m_t = jnp.maximum(inter, jnp.max(dmat, axis=-1, keepdims=True))
            w_inter = jnp.exp(inter - m_t)
            s_qk = lax.dot_general(q, k, (((1,), (1,)), ((), ())), preferred_element_type=F32)
            ws = jnp.exp(dmat - m_t) * s_qk
            v_aug = jnp.concatenate([v, ones_col], axis=1)
            nd = (w_inter * jnp.dot(q, st.astype(BF16), preferred_element_type=F32)
                  + jnp.dot(ws.astype(BF16), v_aug, preferred_element_type=F32))
            num = nd[:, :HEAD_DIM]
            den = nd[:, HEAD_DIM:HEAD_DIM + 1]
            h = num / jnp.maximum(jnp.abs(den), jnp.exp(-m_t))

            og = o_ref[0, pl.ds(r0, BLK), cs].astype(F32)
            hn = h * lax.rsqrt(jnp.mean(h * h, axis=-1, keepdims=True) + RMS_EPS)
            hn = hn * ng_ref[:, cs] * _sigmoid(og)
            out_ref[0, pl.ds(r0, BLK), cs] = hn.astype(BF16)

            b_last = b_r[:, BLK - 1:BLK]
            g_r = b_last - b_r + li_r
            m_new = jnp.maximum(b_last + m, jnp.max(g_r, axis=-1, keepdims=True))
            wg = jnp.exp(b_last - b_c + li_c - m_new)
            decay = jnp.exp(b_last + m - m_new)
            wv = (wg * v_aug.astype(F32)).astype(BF16)
            kv = lax.dot_general(k, wv, (((0,), (0,)), ((), ())), preferred_element_type=F32)
            st_ref[hh] = decay * st + kv
            m_ref[hh] = jnp.broadcast_to(m_new, m_ref.shape[1:])
        return carry

    lax.fori_loop(0, n_chunks, chunk, 0)


def _mlstm(qk3, proj3, gcol, grow, norm_g):
    B, Lp, _ = qk3.shape
    W = ML_HPS * HEAD_DIM
    nq = D_ML // W
    seq = lambda off: pl.BlockSpec((1, Lp, W), lambda b, g: (b, 0, off + g))
    return pl.pallas_call(
        _mlstm_kernel,
        grid=(B, ML_GROUPS),
        in_specs=[seq(0), seq(nq),
                  pl.BlockSpec((1, Lp, W), lambda b, g: (b, 0, C_V // W + g)),
                  pl.BlockSpec((1, Lp, W), lambda b, g: (b, 0, C_O // W + g)),
                  pl.BlockSpec((1, Lp, BLK), lambda b, g: (b, 0, g)),
                  pl.BlockSpec((1, 1, 8, Lp), lambda b, g: (b, g, 0, 0)),
                  pl.BlockSpec((1, W), lambda b, g: (0, g))],
        out_specs=pl.BlockSpec((1, Lp, W), lambda b, g: (b, 0, g)),
        out_shape=jax.ShapeDtypeStruct((B, Lp, D_ML), BF16),
        scratch_shapes=[pltpu.VMEM((ML_HPS, HEAD_DIM, 2 * HEAD_DIM), F32),
                        pltpu.VMEM((ML_HPS, 8, BLK), F32)],
        compiler_params=_cparams(("parallel", "parallel")),
        name="mlstm",
    )(qk3, qk3, proj3, proj3, gcol, grow, norm_g.reshape(1, D_ML))


def _sb_kernel(q_ref, k_ref, v_ref, g_ref, o_ref):
    i = pl.program_id(2)
    W = SB_KEY_BLOCKS * BLK
    r1 = lax.broadcasted_iota(jnp.int32, (BLK, BLK), 0)
    c1 = lax.broadcasted_iota(jnp.int32, (BLK, BLK), 1)
    upper_strict = jnp.where(r1 > c1, 1.0, 0.0).astype(BF16)
    rowp = lax.broadcasted_iota(jnp.int32, (BLK, W), 0) + i * BLK
    ci = lax.broadcasted_iota(jnp.int32, (BLK, W), 1)
    scale = HEAD_DIM ** -0.5
    heads = [slice(hh * HEAD_DIM, (hh + 1) * HEAD_DIM) for hh in range(SB_HPS)]
    qs = [(q_ref[0, :, hs].astype(F32) * scale).astype(BF16) for hs in heads]

    def one_head(q, k, v, carry, acc, valid):
        z = lax.dot_general(q, k, (((1,), (1,)), ((), ())), preferred_element_type=F32)
        sp = _softplus(z)
        lk = jnp.where(valid, -sp, 0.0)
        hi = lk.astype(BF16)
        lo = (lk - hi.astype(F32)).astype(BF16)
        pieces = [None] * SB_KEY_BLOCKS
        off = carry
        for c in reversed(range(SB_KEY_BLOCKS)):
            sl = slice(c * BLK, (c + 1) * BLK)
            local = (jnp.dot(hi[:, sl], upper_strict, preferred_element_type=F32)
                     + jnp.dot(lo[:, sl], upper_strict, preferred_element_type=F32))
            pieces[c] = local + off
            off = off + jnp.sum(lk[:, sl], axis=-1, keepdims=True)
        after = jnp.concatenate(pieces, axis=1)
        a = jnp.where(valid, jnp.exp(z - sp + after), 0.0)
        return off, acc + jnp.dot(a.astype(BF16), v, preferred_element_type=F32)

    def step(t, state):
        end = (i + 1) * BLK - t * W
        s0 = pl.multiple_of(jnp.maximum(end - W, 0), BLK)
        colp = ci + s0
        valid = jnp.logical_and(colp < jnp.minimum(rowp, end), colp >= PAD)
        return tuple(one_head(qs[hh], k_ref[0, pl.ds(s0, W), hs], v_ref[0, pl.ds(s0, W), hs],
                              carry, acc, valid)
                     for hh, (hs, (carry, acc)) in enumerate(zip(heads, state)))

    init = tuple((jnp.zeros((BLK, 1), F32), jnp.zeros((BLK, HEAD_DIM), F32)) for _ in heads)
    n_steps = (i + SB_KEY_BLOCKS) // SB_KEY_BLOCKS
    final = lax.fori_loop(0, n_steps, step, init)
    for hs, (_, acc) in zip(heads, final):
        hn = acc * lax.rsqrt(jnp.mean(acc * acc, axis=-1, keepdims=True) + RMS_EPS) * g_ref[:, hs]
        o_ref[0, :, hs] = hn.astype(BF16)


def _stick_breaking(proj3, norm_g):
    B, Lp, _ = proj3.shape
    nq = Lp // BLK
    return pl.pallas_call(
        _sb_kernel,
        grid=(B, SB_HEADS // SB_HPS, nq),
        in_specs=[pl.BlockSpec((1, BLK, SB_W), lambda b, h, i: (b, i, C_SBQ // SB_W + h)),
                  pl.BlockSpec((1, Lp, SB_W), lambda b, h, i: (b, 0, C_SBK // SB_W + h)),
                  pl.BlockSpec((1, Lp, SB_W), lambda b, h, i: (b, 0, C_SBV // SB_W + h)),
                  pl.BlockSpec((1, SB_W), lambda b, h, i: (0, h))],
        out_specs=pl.BlockSpec((1, BLK, SB_W), lambda b, h, i: (b, i, h)),
        out_shape=jax.ShapeDtypeStruct((B, Lp, D_SB), BF16),
        compiler_params=_cparams(("parallel", "parallel", "parallel")),
        name="stick_breaking",
    )(proj3, proj3, proj3, norm_g.reshape(1, D_SB))


def _pool_kernel(x_ref, w_ref, s_ref, o_ref):
    g = pl.program_id(1)
    x = x_ref[0].astype(F32)
    row = lax.broadcasted_iota(jnp.int32, x.shape, 0)
    x = jnp.where(row < PAD, 0.0, x)
    pos1 = (row - PAD + 1).astype(F32)
    sums = x
    mean = jnp.zeros_like(x)
    shift = 1
    for gi, win in enumerate(POOL_WINDOWS):
        while shift < win:
            sums = sums + pltpu.roll(sums, shift, axis=0)
            shift *= 2
        cnt = jnp.maximum(jnp.minimum(pos1, float(win)), 1.0)
        mean = jnp.where(g == gi, sums / cnt, mean)
    p = (mean - x).astype(BF16)
    y = jnp.dot(p, w_ref[0].astype(BF16), preferred_element_type=F32) * s_ref[...]
    o_ref[0] = y.astype(BF16)


def _pool(proj3, pool_w, pool_scale):
    B, Lp, _ = proj3.shape
    return pl.pallas_call(
        _pool_kernel,
        grid=(B, POOL_GROUPS),
        in_specs=[pl.BlockSpec((1, Lp, HEAD_DIM), lambda b, g: (b, 0, C_POOL // HEAD_DIM + g)),
                  pl.BlockSpec((1, HEAD_DIM, HEAD_DIM), lambda b, g: (g, 0, 0)),
                  pl.BlockSpec((1, HEAD_DIM), lambda b, g: (0, g))],
        out_specs=pl.BlockSpec((1, Lp, HEAD_DIM), lambda b, g: (b, 0, g)),
        out_shape=jax.ShapeDtypeStruct((B, Lp, D_POOL), BF16),
        compiler_params=_cparams(("parallel", "parallel")),
        name="pool_mixer",
    )(proj3, pool_w, pool_scale.reshape(1, D_POOL))


def _to_row_tiles(ref, y):
    for s in range(XROWS):
        ref[:, s, :] = y[:, s * BLK:(s + 1) * BLK]


def _from_pitched(ref, slot, n, first=0):
    return jnp.concatenate(
        [ref[slot, pl.ds(first * ROW_PITCH + s, n, stride=ROW_PITCH), :] for s in range(XROWS)], axis=1)


def _outproj_kernel(a1_ref, a2_ref, a3_ref, w_ref, h_ref, g_ref, b_ref, o_ref, ot_ref, *, alpha):
    acc = jnp.dot(a1_ref[...], w_ref[0:D_ML, :], preferred_element_type=F32)
    acc = acc + jnp.dot(a2_ref[...], w_ref[D_ML:D_ML + D_SB, :], preferred_element_type=F32)
    acc = acc + jnp.dot(a3_ref[...], w_ref[D_ML + D_SB:, :], preferred_element_type=F32)
    y = _layer_norm(alpha * h_ref[...] + acc, g_ref[...], b_ref[...])
    o_ref[...] = y
    _to_row_tiles(ot_ref, y)


def _outproj_ln(a_ml, a_sb, a_pool, w, h, g, b, alpha):
    T, D = h.shape
    row = lambda n: pl.BlockSpec((ROW_TILE, n), lambda i: (i, 0))
    vec = pl.BlockSpec((1, D), lambda i: (0, 0))
    return pl.pallas_call(
        functools.partial(_outproj_kernel, alpha=alpha),
        grid=(T // ROW_TILE,),
        in_specs=[row(D_ML), row(D_SB), row(D_POOL),
                  pl.BlockSpec(w.shape, lambda i: (0, 0)), row(D), vec, vec],
        out_specs=[row(D), pl.BlockSpec((ROW_TILE, XROWS, BLK), lambda i: (i, 0, 0))],
        out_shape=[jax.ShapeDtypeStruct((T, D), F32), jax.ShapeDtypeStruct((T, XROWS, BLK), F32)],
        compiler_params=_cparams(("parallel",)),
        name="out_proj_ln",
    )(a_ml, a_sb, a_pool, w, h, g.reshape(1, D), b.reshape(1, D))


def _router_kernel(h_ref, rw_ref, rb_ref, idx_ref, gate_ref, rank_ref, cnt_ref, run_ref, *, tm):
    first = jnp.logical_and(pl.program_id(0) == 0, pl.program_id(1) == 0)

    @pl.when(first)
    def _():
        run_ref[...] = jnp.zeros_like(run_ref)

    x = h_ref[0]
    logits = jnp.dot(x, rw_ref[...], preferred_element_type=F32, precision=lax.Precision.HIGHEST)
    scores = _sigmoid(logits)
    work = scores + rb_ref[...]
    E = scores.shape[1]
    lane = lax.broadcasted_iota(jnp.int32, (tm, E), 1).astype(F32)
    row = lax.broadcasted_iota(jnp.int32, (tm, 1), 0) + pl.program_id(1) * tm
    real = (row >= PAD).astype(F32)

    picks = []
    gates = []
    member = jnp.zeros((tm, E), F32)
    for _ in range(TOP_K):
        mx = jnp.max(work, axis=-1, keepdims=True)
        pick = jnp.min(jnp.where(work == mx, lane, float(E)), axis=-1, keepdims=True)
        onehot = lane == pick
        gates.append(jnp.sum(jnp.where(onehot, scores, 0.0), axis=-1, keepdims=True))
        picks.append(pick)
        member = member + jnp.where(onehot, real, 0.0)
        work = jnp.where(onehot, -jnp.inf, work)
    gsum = gates[0]
    for gk in gates[1:]:
        gsum = gsum + gk

    ri = lax.broadcasted_iota(jnp.int32, (tm, tm), 0)
    ci = lax.broadcasted_iota(jnp.int32, (tm, tm), 1)
    lower_strict = jnp.where(ci < ri, 1.0, 0.0).astype(BF16)
    before = jnp.dot(lower_strict, member.astype(BF16), preferred_element_type=F32) + run_ref[...]

    lane_o = lax.broadcasted_iota(jnp.int32, (tm, BLK), 1)
    idx_o = jnp.zeros((tm, BLK), F32)
    gate_o = jnp.zeros((tm, BLK), F32)
    rank_o = jnp.zeros((tm, BLK), F32)
    for kk in range(TOP_K):
        onehot = lane == picks[kk]
        rk = jnp.sum(jnp.where(onehot, before, 0.0), axis=-1, keepdims=True)
        sel = lane_o == kk
        idx_o = jnp.where(sel, picks[kk], idx_o)
        gate_o = jnp.where(sel, gates[kk] / gsum * ROUTED_SCALE, gate_o)
        rank_o = jnp.where(sel, rk, rank_o)
    idx_ref[0] = idx_o.astype(jnp.int32)
    gate_ref[0] = gate_o
    rank_ref[0] = rank_o.astype(jnp.int32)
    run_ref[...] = run_ref[...] + jnp.sum(member, axis=0, keepdims=True)
    cnt_ref[...] = run_ref[...].astype(jnp.int32)


def _router(h3, router_w, router_bias):
    B, Lp, D = h3.shape
    E = router_w.shape[1]
    tm = Lp // 4
    out = lambda: pl.BlockSpec((1, tm, BLK), lambda b, j: (b, j, 0))
    return pl.pallas_call(
        functools.partial(_router_kernel, tm=tm),
        grid=(B, Lp // tm),
        in_specs=[pl.BlockSpec((1, tm, D), lambda b, j: (b, j, 0)),
                  pl.BlockSpec((D, E), lambda b, j: (0, 0)),
                  pl.BlockSpec((1, E), lambda b, j: (0, 0))],
        out_specs=[out(), out(), out(), pl.BlockSpec((1, E), lambda b, j: (0, 0))],
        out_shape=[jax.ShapeDtypeStruct((B, Lp, BLK), jnp.int32),
                   jax.ShapeDtypeStruct((B, Lp, BLK), F32),
                   jax.ShapeDtypeStruct((B, Lp, BLK), jnp.int32),
                   jax.ShapeDtypeStruct((1, E), jnp.int32)],
        scratch_shapes=[pltpu.VMEM((1, E), F32)],
        compiler_params=_cparams(("arbitrary", "arbitrary")),
        name="router",
    )(h3, router_w, router_bias.reshape(1, E))


def _swiglu(x, w13, w2):
    h1 = jnp.dot(x, w13, preferred_element_type=F32)
    f = h1.shape[1] // 2
    gate, up = h1[:, :f], h1[:, f:]
    act = (gate * _sigmoid(gate) * up).astype(BF16)
    return jnp.dot(act, w2, preferred_element_type=F32)


def _shared_kernel(x_ref, w13_ref, w2_ref, o_ref):
    o_ref[...] = _swiglu(x_ref[...].astype(BF16), w13_ref[...], w2_ref[...])


def _shared_expert(h, w13, w2):
    T, D = h.shape
    row = pl.BlockSpec((ROW_TILE, D), lambda i: (i, 0))
    return pl.pallas_call(
        _shared_kernel,
        grid=(T // ROW_TILE,),
        in_specs=[row, pl.BlockSpec(w13.shape, lambda i: (0, 0)),
                  pl.BlockSpec(w2.shape, lambda i: (0, 0))],
        out_specs=row,
        out_shape=jax.ShapeDtypeStruct((T, D), F32),
        compiler_params=_cparams(("parallel",)),
        name="shared_expert",
    )(h, w13, w2)


def _moe_kernel(be_ref, bs_ref, bo_ref, bn_ref, nb_ref, tok_hbm, x_hbm, w13_hbm, w2_hbm, y_ref,
                win0, win1, wsem, xbuf, xsem, wf13, wf2, fsem, w13b, w2b, *, layer):
    i = pl.program_id(0)
    n_used = nb_ref[0]
    slot = lax.rem(i, 2)
    F = w2b.shape[0]
    wins = (win0, win1)

    def by_parity(p, fn):
        for s in range(2):
            pl.when(p == s)(functools.partial(fn, s))

    def window(blk, s):
        base = pl.multiple_of(lax.shift_left(lax.shift_right_logical(bs_ref[blk], TOK_ALIGN_LOG2),
                                             TOK_ALIGN_LOG2), TOK_ALIGN)
        return pltpu.make_async_copy(tok_hbm.at[pl.ds(base, TOK_WINDOW)], wins[s], wsem.at[s])

    def weights(e, s):
        return (pltpu.make_async_copy(w13_hbm.at[layer, e], wf13.at[s], fsem.at[0, s]),
                pltpu.make_async_copy(w2_hbm.at[layer, e], wf2.at[s], fsem.at[1, s]))

    def row_copy(tok, s, r):
        return pltpu.make_async_copy(x_hbm.at[tok], xbuf.at[s, pl.ds(r * ROW_PITCH, XROWS), :], xsem.at[s])

    def rows_wait(s):
        rows = x_hbm.at[pl.ds(0, EXPERT_TILE)]
        pltpu.make_async_copy(rows, rows, xsem.at[s]).wait()

    @pl.when(jnp.logical_and(i == 0, n_used > 0))
    def _():
        for c in weights(be_ref[0], 0):
            c.start()
        window(0, 0).start()
        window(0, 0).wait()
        first0 = jnp.bitwise_and(bs_ref[0], TOK_ALIGN - 1)

        def body(r, c):
            row_copy(win0[first0 + r], 0, r).start()
            return c
        lax.fori_loop(0, EXPERT_TILE, body, 0)

        @pl.when(n_used > 1)
        def _():
            window(1, 1).start()

    @pl.when(i + 1 < n_used)
    def _():
        by_parity(1 - slot, lambda s: window(i + 1, s).wait())

    @pl.when(i + 2 < n_used)
    def _():
        by_parity(slot, lambda s: window(i + 2, s).start())

    @pl.when(i < n_used)
    def _():
        e = be_ref[i]
        ws = lax.rem(bo_ref[i], 2)

        @pl.when(jnp.logical_or(i == 0, e != be_ref[jnp.maximum(i - 1, 0)]))
        def _():
            for c in weights(e, ws):
                c.wait()
            nxt = bn_ref[i]

            @pl.when(nxt < n_used)
            def _():
                for c in weights(be_ref[nxt], 1 - ws):
                    c.start()
            w13b[...] = wf13[ws].astype(BF16)
            w2b[...] = wf2[ws].astype(BF16)

        nblk = jnp.minimum(i + 1, n_used - 1)
        first = jnp.bitwise_and(bs_ref[nblk], TOK_ALIGN - 1)

        def issue(s):
            for r in range(EXPERT_TILE):
                row_copy(wins[s][first + r], 1 - slot, r).start(priority=r % 2)
        by_parity(lax.rem(nblk, 2), issue)

        rows_wait(slot)
        x = _from_pitched(xbuf, slot, EXPERT_TILE).astype(BF16)
        acts = []
        half = F // 2
        for c in range(2):
            gate = jnp.dot(x, w13b[:, c * half:(c + 1) * half], preferred_element_type=F32)
            up = jnp.dot(x, w13b[:, F + c * half:F + (c + 1) * half], preferred_element_type=F32)
            acts.append((gate * _sigmoid(gate) * up).astype(BF16))
        act = jnp.concatenate(acts, axis=1)
        for c in range(XROWS // 4):
            y = jnp.dot(act, w2b[:, c * 4 * BLK:(c + 1) * 4 * BLK], preferred_element_type=F32)
            for s in range(4):
                y_ref[pl.ds(c * 4 + s, EXPERT_TILE, stride=ROW_PITCH), :] = y[:, s * BLK:(s + 1) * BLK]
        y_ref[pl.ds(XROWS, EXPERT_TILE, stride=ROW_PITCH), :] = jnp.zeros((EXPERT_TILE, BLK), F32)

        @pl.when(i == n_used - 1)
        def _():
            rows_wait(1 - slot)

    @pl.when(i >= n_used)
    def _():
        y_ref[...] = jnp.zeros_like(y_ref)


def _routed_experts(x_tiles, sorted_tok, block_e, block_start, block_ord, block_next, n_used, w13, w2, layer):
    nb = block_e.shape[0]
    _, E, D, F2 = w13.shape
    F = F2 // 2
    grid_spec = pltpu.PrefetchScalarGridSpec(
        num_scalar_prefetch=5,
        grid=(nb,),
        in_specs=[pl.BlockSpec(memory_space=pl.ANY)] * 4,
        out_specs=pl.BlockSpec((EXPERT_TILE * ROW_PITCH, BLK), lambda i, *_: (i, 0)),
        scratch_shapes=[pltpu.SMEM((TOK_WINDOW,), jnp.int32),
                        pltpu.SMEM((TOK_WINDOW,), jnp.int32),
                        pltpu.SemaphoreType.DMA((2,)),
                        pltpu.VMEM((2, EXPERT_TILE * ROW_PITCH, BLK), F32),
                        pltpu.SemaphoreType.DMA((2,)),
                        pltpu.VMEM((2, D, F2), F32),
                        pltpu.VMEM((2, F, D), F32),
                        pltpu.SemaphoreType.DMA((2, 2)),
                        pltpu.VMEM((D, F2), BF16),
                        pltpu.VMEM((F, D), BF16)],
    )
    return pl.pallas_call(
        functools.partial(_moe_kernel, layer=layer),
        grid_spec=grid_spec,
        out_shape=jax.ShapeDtypeStruct((nb * EXPERT_TILE * ROW_PITCH, BLK), F32),
        compiler_params=_cparams(("arbitrary",)),
        name="routed_experts",
    )(block_e, block_start, block_ord, block_next, n_used, sorted_tok, x_tiles, w13, w2)


def _combine_kernel(pos_ref, posn_ref, y_hbm, gate_ref, sh_ref, h_ref, g_ref, b_ref,
                    o_ref, *rest, alpha, n_steps):
    ob_ref = rest[0] if len(rest) == 3 else None
    gbuf, sem = rest[-2:]
    i = pl.program_id(0)
    slot = lax.rem(i, 2)
    n_rows = TOP_K * COMBINE_TILE

    def copy(pos, s, r):
        return pltpu.make_async_copy(y_hbm.at[pl.ds(pos * ROW_PITCH, XROWS), :],
                                     gbuf.at[s, pl.ds(r * ROW_PITCH, XROWS), :], sem.at[s])

    def gather(pref, s):
        def body(r8, c):
            for u in range(8):
                r = r8 * 8 + u
                copy(pref[0, 0, r], s, r).start(priority=u % 2)
            return c
        lax.fori_loop(0, n_rows // 8, body, 0, unroll=2)

    @pl.when(i == 0)
    def _():
        gather(pos_ref, 0)

    @pl.when(i + 1 < n_steps)
    def _():
        gather(posn_ref, 1 - slot)

    all_rows = y_hbm.at[pl.ds(0, n_rows * XROWS), :]
    pltpu.make_async_copy(all_rows, all_rows, sem.at[slot]).wait()

    gate = gate_ref[...]
    acc = alpha * h_ref[...] + sh_ref[...]
    for kk in range(TOP_K):
        rows = _from_pitched(gbuf, slot, COMBINE_TILE, first=kk * COMBINE_TILE)
        acc = acc + gate[:, kk:kk + 1] * rows
    y = _layer_norm(acc, g_ref[...], b_ref[...])
    o_ref[...] = y
    if ob_ref is not None:
        ob_ref[...] = y.astype(BF16)


def _combine_ln(pos_blocks, y, gate, shared, h, g, b, alpha, seq_blocks=None):
    T, D = h.shape
    n_steps = T // COMBINE_TILE
    n_rows = TOP_K * COMBINE_TILE
    row = lambda n: pl.BlockSpec((COMBINE_TILE, n), lambda i: (i, 0))
    vec = pl.BlockSpec((1, D), lambda i: (0, 0))
    if seq_blocks is None:
        out_specs = [row(D), row(D)]
        out_shape = [jax.ShapeDtypeStruct((T, D), F32), jax.ShapeDtypeStruct((T, D), BF16)]
    else:
        keep = seq_blocks - 1
        out_map = lambda i: ((i // seq_blocks) * keep + jnp.maximum(i % seq_blocks - 1, 0), 0)
        out_specs = [pl.BlockSpec((COMBINE_TILE, D), out_map)]
        out_shape = [jax.ShapeDtypeStruct((T // seq_blocks * keep, D), F32)]
    return pl.pallas_call(
        functools.partial(_combine_kernel, alpha=alpha, n_steps=n_steps),
        grid=(n_steps,),
        in_specs=[
            pl.BlockSpec((1, 1, n_rows), lambda i: (i, 0, 0), memory_space=pltpu.SMEM),
            pl.BlockSpec((1, 1, n_rows), lambda i: (jnp.minimum(i + 1, n_steps - 1), 0, 0),
                         memory_space=pltpu.SMEM),
            pl.BlockSpec(memory_space=pl.ANY),
            row(BLK), row(D), row(D), vec, vec],
        out_specs=out_specs,
        out_shape=out_shape,
        scratch_shapes=[pltpu.VMEM((2, n_rows * ROW_PITCH, BLK), F32), pltpu.SemaphoreType.DMA((2,))],
        compiler_params=_cparams(("arbitrary",)),
        name="combine_ln",
    )(pos_blocks, pos_blocks, y, gate, shared, h, g.reshape(1, D), b.reshape(1, D))


def _wprep_kernel(w_ref, o_ref):
    o_ref[:, 0:C_SBQ] = w_ref[0, :, 0:C_SBQ].astype(BF16)
    o_ref[:, C_SBQ:N_MAIN] = w_ref[0, :, O_SB:O_SB + N_MAIN - C_SBQ].astype(BF16)


def _prep_in_proj(w_in_all, layer):
    _, D, n_in = w_in_all.shape
    tr = 256
    return pl.pallas_call(
        _wprep_kernel,
        grid=(D // tr,),
        in_specs=[pl.BlockSpec((1, tr, n_in), lambda i: (layer, i, 0))],
        out_specs=pl.BlockSpec((tr, N_MAIN), lambda i: (i, 0)),
        out_shape=jax.ShapeDtypeStruct((D, N_MAIN), BF16),
        compiler_params=_cparams(("parallel",)),
        name="w_in_prep",
    )(w_in_all)


def _arrange_gates(w_gates, b_igate, b_fgate):
    D = w_gates.shape[0]
    wg = jnp.zeros((D, ML_GROUPS, BLK), F32)
    bias = jnp.zeros((ML_GROUPS, BLK), F32)
    w_i = w_gates[:, :ML_HEADS].reshape(D, ML_GROUPS, ML_HPS)
    w_f = w_gates[:, ML_HEADS:].reshape(D, ML_GROUPS, ML_HPS)
    wg = wg.at[:, :, :ML_HPS].set(w_i).at[:, :, ML_HPS:2 * ML_HPS].set(w_f)
    bias = bias.at[:, :ML_HPS].set(b_igate.reshape(ML_GROUPS, ML_HPS))
    bias = bias.at[:, ML_HPS:2 * ML_HPS].set(b_fgate.reshape(ML_GROUPS, ML_HPS))
    return wg.reshape(D, ML_GROUPS * BLK).astype(BF16), bias.reshape(1, ML_GROUPS * BLK)


def _dispatch_plan(idx, rank, counts, B, Lp):
    T = B * Lp
    A = T * TOP_K
    A_real = B * (Lp - PAD) * TOP_K
    nb = -(-A_real // EXPERT_TILE) + N_EXPERTS
    E = N_EXPERTS
    eids = jnp.arange(E, dtype=jnp.int32)
    padded = (counts + EXPERT_TILE - 1) // EXPERT_TILE * EXPERT_TILE
    ends = jnp.cumsum(padded)
    poff = ends - padded
    off = jnp.cumsum(counts) - counts
    idx_k = idx[:, :, :TOP_K]
    onehot = idx_k[..., None] == eids
    pos = jnp.sum(jnp.where(onehot, poff, 0), axis=-1) + rank[:, :, :TOP_K]
    real = (jnp.arange(Lp) >= PAD)[None, :, None]
    pos = jnp.where(real, pos, 0).astype(jnp.int32)
    a_bits = (A - 1).bit_length()
    assert (E + 1) << a_bits < 2 ** 31
    keys = jnp.where(real, idx_k, E).reshape(A)
    packed = lax.shift_left(keys, a_bits) + jnp.arange(A, dtype=jnp.int32)
    sorted_tok = jnp.bitwise_and(jnp.sort(packed), (1 << a_bits) - 1) // TOP_K
    n_tok = -(-A // TOK_ALIGN) * TOK_ALIGN + TOK_WINDOW
    sorted_tok = jnp.concatenate([sorted_tok, jnp.full((n_tok - A,), PAD, jnp.int32)])
    blk0 = jnp.arange(nb, dtype=jnp.int32) * EXPERT_TILE
    block_e = jnp.minimum(jnp.sum(ends[None, :] <= blk0[:, None], axis=1), E - 1).astype(jnp.int32)
    sel = block_e[:, None] == eids
    block_start = jnp.sum(jnp.where(sel, off - poff, 0), axis=1) + blk0
    block_start = jnp.clip(block_start, 0, A).astype(jnp.int32)
    n_used = (ends[-1] // EXPERT_TILE).astype(jnp.int32).reshape(1)
    changed = jnp.concatenate([jnp.zeros((1,), jnp.int32),
                               (block_e[1:] != block_e[:-1]).astype(jnp.int32)])
    block_ord = jnp.cumsum(changed).astype(jnp.int32)
    block_next = (jnp.sum(jnp.where(sel, ends, 0), axis=1) // EXPERT_TILE).astype(jnp.int32)
    return pos.reshape(T, TOP_K), sorted_tok, (block_e, block_start, block_ord, block_next, n_used)


def kernel(x, meta, ln_in_g, ln_in_b, w_in, b_igate, b_fgate, conv_qk, ml_norm_g, sb_norm_g, pool_w, pool_scale, w_out, ln1_g, ln1_b, router_w, router_bias, exp_w13, exp_w2, sh_w13, sh_w2, ln2_g, ln2_b):
    B, S, D = x.shape
    depth = w_in.shape[0]
    alpha = (2 * depth) ** 0.25
    Lp = PAD + N_META + S
    T = B * Lp
    assert Lp % BLK == 0 and T % ROW_TILE == 0 and T % COMBINE_TILE == 0
    assert D == XROWS * BLK and COMBINE_TILE == PAD + N_META

    h0 = jnp.concatenate([jnp.zeros((B, PAD, D), x.dtype),
                          jnp.broadcast_to(meta[None].astype(x.dtype), (B, N_META, D)), x], axis=1)
    h, hb = _ln_rows(h0.reshape(T, D), ln_in_g, ln_in_b)

    for l in range(depth):
        w_gate, gate_bias = _arrange_gates(w_in[l, :, C_SBQ:O_SB], b_igate[l], b_fgate[l])
        proj3 = _in_proj(hb, _prep_in_proj(w_in, l)).reshape(B, Lp, N_MAIN)
        gcol = _gates(hb.reshape(B, Lp, D), w_gate, gate_bias)
        grow = jnp.transpose(gcol.reshape(B, Lp, ML_GROUPS, BLK)[..., :8], (0, 2, 3, 1))
        qk3 = _conv_silu(proj3, conv_qk[l])
        a_ml = _mlstm(qk3, proj3, gcol, grow, ml_norm_g[l])
        a_sb = _stick_breaking(proj3, sb_norm_g[l])
        a_pool = _pool(proj3, pool_w[l], pool_scale[l])
        h1, h1_tiles = _outproj_ln(a_ml.reshape(T, D_ML), a_sb.reshape(T, D_SB), a_pool.reshape(T, D_POOL),
                                   w_out[l].astype(BF16), h, ln1_g[l], ln1_b[l], alpha)

        idx, gate, rank, counts = _router(h1.reshape(B, Lp, D), router_w[l], router_bias[l])
        pos, sorted_tok, tables = _dispatch_plan(idx, rank, counts[0], B, Lp)
        y = _routed_experts(h1_tiles, sorted_tok, *tables, exp_w13, exp_w2, l)
        shared = _shared_expert(h1, sh_w13[l].astype(BF16), sh_w2[l].astype(BF16))
        n_steps = T // COMBINE_TILE
        pos_blocks = jnp.transpose(pos.reshape(n_steps, COMBINE_TILE, TOP_K), (0, 2, 1))
        pos_blocks = pos_blocks.reshape(n_steps, 1, TOP_K * COMBINE_TILE)
        args = (pos_blocks, y, gate.reshape(T, BLK), shared, h1, ln2_g[l], ln2_b[l], alpha)
        if l + 1 < depth:
            h, hb = _combine_ln(*args)
        else:
            (out,) = _combine_ln(*args, seq_blocks=Lp // COMBINE_TILE)

    return out.reshape(B, S, D)
```

```python
import functools

import jax
import jax.numpy as jnp
from jax import lax
from jax.experimental import pallas as pl
from jax.experimental.pallas import tpu as pltpu

N_META = 16
HEAD_DIM = 128
ML_HEADS = 6
SB_HEADS = 6
POOL_WINDOWS = (2, 4, 8, 16)
POOL_GROUPS = 4
D_ML = ML_HEADS * HEAD_DIM
D_SB = SB_HEADS * HEAD_DIM
D_POOL = POOL_GROUPS * HEAD_DIM
CONV_W = 4
N_EXPERTS = 64
TOP_K = 8
ROUTED_SCALE = 2.5
LN_EPS = 1e-5
RMS_EPS = 1e-6

BLK = 128
PAD = (-N_META) % BLK
ML_HPS = 3
ML_GROUPS = ML_HEADS // ML_HPS
SB_KEY_BLOCKS = 8
SB_HPS = 3
SB_W = SB_HPS * HEAD_DIM
ROW_TILE = 512
EXPERT_TILE = 256
COMBINE_TILE = 128
TOK_ALIGN_LOG2 = 10
TOK_ALIGN = 1 << TOK_ALIGN_LOG2
TOK_WINDOW = 2 * TOK_ALIGN
NEG = -1e30
VMEM_LIMIT = 56 * 1024 * 1024

C_QK = 0
C_V = 2 * D_ML
C_O = C_V + D_ML
C_SBQ = C_O + D_ML
C_SBK = C_SBQ + D_SB
C_SBV = C_SBK + D_SB
C_POOL = C_SBV + D_SB
N_MAIN = C_POOL + D_POOL
O_SB = C_SBQ + 2 * ML_HEADS
XROWS = 16
ROW_PITCH = XROWS + 1

F32 = jnp.float32
BF16 = jnp.bfloat16


def _cparams(sem, vmem=VMEM_LIMIT):
    return pltpu.CompilerParams(dimension_semantics=sem, vmem_limit_bytes=vmem)


def _layer_norm(x, g, b):
    mu = jnp.mean(x, axis=-1, keepdims=True)
    xc = x - mu
    var = jnp.mean(xc * xc, axis=-1, keepdims=True)
    return xc * lax.rsqrt(var + LN_EPS) * g + b


def _sigmoid(x):
    return 1.0 / (1.0 + jnp.exp(-x))


def _softplus(x):
    return jnp.maximum(x, 0.0) + jnp.log(1.0 + jnp.exp(-jnp.abs(x)))


def _split_dot(a, b, split_lhs):
    x = a if split_lhs else b
    out = None
    for _ in range(3):
        part = x.astype(BF16)
        x = x - part.astype(F32)
        term = (jnp.dot(part, b, preferred_element_type=F32) if split_lhs
                else jnp.dot(a, part, preferred_element_type=F32))
        out = term if out is None else out + term
    return out


def _ln_kernel(x_ref, g_ref, b_ref, o_ref, ob_ref):
    y = _layer_norm(x_ref[...], g_ref[...], b_ref[...])
    o_ref[...] = y
    ob_ref[...] = y.astype(BF16)


def _ln_rows(x, g, b):
    T, D = x.shape
    row = pl.BlockSpec((ROW_TILE, D), lambda i: (i, 0))
    vec = pl.BlockSpec((1, D), lambda i: (0, 0))
    return pl.pallas_call(
        _ln_kernel,
        grid=(T // ROW_TILE,),
        in_specs=[row, vec, vec],
        out_specs=[row, row],
        out_shape=[jax.ShapeDtypeStruct((T, D), F32), jax.ShapeDtypeStruct((T, D), BF16)],
        compiler_params=_cparams(("parallel",)),
        name="ln_in",
    )(x, g.reshape(1, D), b.reshape(1, D))


def _mm_kernel(a_ref, w_ref, o_ref):
    o_ref[...] = jnp.dot(a_ref[...], w_ref[...], preferred_element_type=F32).astype(o_ref.dtype)


def _in_proj(a, w):
    M, K = a.shape
    N = w.shape[1]
    tn = N // 2
    return pl.pallas_call(
        _mm_kernel,
        grid=(N // tn, M // ROW_TILE),
        in_specs=[pl.BlockSpec((ROW_TILE, K), lambda j, i: (i, 0)),
                  pl.BlockSpec((K, tn), lambda j, i: (0, j))],
        out_specs=pl.BlockSpec((ROW_TILE, tn), lambda j, i: (i, j)),
        out_shape=jax.ShapeDtypeStruct((M, N), BF16),
        compiler_params=_cparams(("parallel", "parallel")),
        name="in_proj",
    )(a, w)


def _gates_kernel(hb_ref, wg_ref, bias_ref, o_ref, *, tm):
    j = pl.program_id(1)
    g = jnp.dot(hb_ref[0], wg_ref[...], preferred_element_type=F32) + bias_ref[...]
    col = jnp.bitwise_and(lax.broadcasted_iota(jnp.int32, g.shape, 1), BLK - 1)
    row = lax.broadcasted_iota(jnp.int32, g.shape, 0) + j * tm
    is_f = jnp.logical_and(col >= ML_HPS, col < 2 * ML_HPS)
    val = jnp.where(is_f, -_softplus(-g), g)
    pad_val = jnp.where(is_f, 0.0, NEG)
    o_ref[0] = jnp.where(row < PAD, pad_val, val)


def _gates(hb3, wg, bias):
    B, Lp, D = hb3.shape
    tm = Lp // 4
    Ng = wg.shape[1]
    return pl.pallas_call(
        functools.partial(_gates_kernel, tm=tm),
        grid=(B, Lp // tm),
        in_specs=[pl.BlockSpec((1, tm, D), lambda b, j: (b, j, 0)),
                  pl.BlockSpec((D, Ng), lambda b, j: (0, 0)),
                  pl.BlockSpec((1, Ng), lambda b, j: (0, 0))],
        out_specs=pl.BlockSpec((1, tm, Ng), lambda b, j: (b, j, 0)),
        out_shape=jax.ShapeDtypeStruct((B, Lp, Ng), F32),
        compiler_params=_cparams(("parallel", "parallel")),
        name="ml_gates",
    )(hb3, wg, bias)


def _conv_kernel(x_ref, w_ref, o_ref):
    c = pl.program_id(1)
    x = x_ref[0].astype(F32)
    row = lax.broadcasted_iota(jnp.int32, x.shape, 0)
    x = jnp.where(row < PAD, 0.0, x)
    w = w_ref[...]
    y = x * w[CONV_W - 1:CONV_W]
    for s in range(1, CONV_W):
        y = y + pltpu.roll(x, s, axis=0) * w[CONV_W - 1 - s:CONV_W - s]
    y = y * _sigmoid(y)
    scale = jnp.where(c < ML_HEADS, HEAD_DIM ** -0.5, 1.0).astype(F32)
    o_ref[0] = (y * scale).astype(BF16)


def _conv_silu(proj3, conv_w):
    B, Lp, _ = proj3.shape
    nc = 2 * D_ML // BLK
    return pl.pallas_call(
        _conv_kernel,
        grid=(B, nc),
        in_specs=[pl.BlockSpec((1, Lp, BLK), lambda b, c: (b, 0, c)),
                  pl.BlockSpec((CONV_W, BLK), lambda b, c: (0, c))],
        out_specs=pl.BlockSpec((1, Lp, BLK), lambda b, c: (b, 0, c)),
        out_shape=jax.ShapeDtypeStruct((B, Lp, 2 * D_ML), BF16),
        compiler_params=_cparams(("parallel", "parallel")),
        name="conv_silu",
    )(proj3, conv_w)


def _mlstm_kernel(q_ref, k_ref, v_ref, o_ref, gc_ref, gr_ref, ng_ref, out_ref, st_ref, m_ref):
    n_chunks = q_ref.shape[1] // BLK
    st_ref[...] = jnp.zeros_like(st_ref)
    m_ref[...] = jnp.zeros_like(m_ref)

    ri = lax.broadcasted_iota(jnp.int32, (BLK, BLK), 0)
    ci = lax.broadcasted_iota(jnp.int32, (BLK, BLK), 1)
    causal = ci <= ri
    lower_incl = jnp.where(causal, 1.0, 0.0).astype(BF16)
    upper_incl = jnp.where(ri <= ci, 1.0, 0.0).astype(BF16)
    ones_col = jnp.where(ci == 0, 1.0, 0.0).astype(BF16)

    def chunk(c, carry):
        r0 = pl.multiple_of(c * BLK, BLK)
        gcol = gc_ref[0, pl.ds(r0, BLK), :]
        grow = gr_ref[0, 0, :, pl.ds(r0, BLK)]
        bcol = _split_dot(lower_incl, gcol, split_lhs=False)
        brow = _split_dot(grow, upper_incl, split_lhs=True)
        for hh in range(ML_HPS):
            cs = slice(hh * HEAD_DIM, (hh + 1) * HEAD_DIM)
            q = q_ref[0, pl.ds(r0, BLK), cs]
            k = k_ref[0, pl.ds(r0, BLK), cs]
            v = v_ref[0, pl.ds(r0, BLK), cs]
            li_c = gcol[:, hh:hh + 1]
            b_c = bcol[:, ML_HPS + hh:ML_HPS + hh + 1]
            li_r = grow[hh:hh + 1, :]
            b_r = brow[ML_HPS + hh:ML_HPS + hh + 1, :]
            m = m_ref[hh, 0:1, 0:1]
            st = st_ref[hh]

            inter = b_c + m
            dmat = jnp.where(causal, b_c - b_r + li_r, NEG)
            m_t = jnp.maximum(inter, jnp.max(dmat, axis=-1, keepdims=True))
            w_inter = jnp.exp(inter - m_t)
            s_qk = lax.dot_general(q, k, (((1,), (1,)), ((), ())), preferred_element_type=F32)
            ws = jnp.exp(dmat - m_t) * s_qk
            v_aug = jnp.concatenate([v, ones_col], axis=1)
            nd = (w_inter * jnp.dot(q, st.astype(BF16), preferred_element_type=F32)
                  + jnp.dot(ws.astype(BF16), v_aug, preferred_element_type=F32))
            num = nd[:, :HEAD_DIM]
            den = nd[:, HEAD_DIM:HEAD_DIM + 1]
            h = num / jnp.maximum(jnp.abs(den), jnp.exp(-m_t))

            og = o_ref[0, pl.ds(r0, BLK), cs].astype(F32)
            hn = h * lax.rsqrt(jnp.mean(h * h, axis=-1, keepdims=True) + RMS_EPS)
            hn = hn * ng_ref[:, cs] * _sigmoid(og)
            out_ref[0, pl.ds(r0, BLK), cs] = hn.astype(BF16)

            b_last = b_r[:, BLK - 1:BLK]
            g_r = b_last - b_r + li_r
            m_new = jnp.maximum(b_last + m, jnp.max(g_r, axis=-1, keepdims=True))
            wg = jnp.exp(b_last - b_c + li_c - m_new)
            decay = jnp.exp(b_last + m - m_new)
            wv = (wg * v_aug.astype(F32)).astype(BF16)
            kv = lax.dot_general(k, wv, (((0,), (0,)), ((), ())), preferred_element_type=F32)
            st_ref[hh] = decay * st + kv
            m_ref[hh] = jnp.broadcast_to(m_new, m_ref.shape[1:])
        return carry

    lax.fori_loop(0, n_chunks, chunk, 0)


def _mlstm(qk3, proj3, gcol, grow, norm_g):
    B, Lp, _ = qk3.shape
    W = ML_HPS * HEAD_DIM
    nq = D_ML // W
    seq = lambda off: pl.BlockSpec((1, Lp, W), lambda b, g: (b, 0, off + g))
    return pl.pallas_call(
        _mlstm_kernel,
        grid=(B, ML_GROUPS),
        in_specs=[seq(0), seq(nq),
                  pl.BlockSpec((1, Lp, W), lambda b, g: (b, 0, C_V // W + g)),
                  pl.BlockSpec((1, Lp, W), lambda b, g: (b, 0, C_O // W + g)),
                  pl.BlockSpec((1, Lp, BLK), lambda b, g: (b, 0, g)),
                  pl.BlockSpec((1, 1, 8, Lp), lambda b, g: (b, g, 0, 0)),
                  pl.BlockSpec((1, W), lambda b, g: (0, g))],
        out_specs=pl.BlockSpec((1, Lp, W), lambda b, g: (b, 0, g)),
        out_shape=jax.ShapeDtypeStruct((B, Lp, D_ML), BF16),
        scratch_shapes=[pltpu.VMEM((ML_HPS, HEAD_DIM, 2 * HEAD_DIM), F32),
                        pltpu.VMEM((ML_HPS, 8, BLK), F32)],
        compiler_params=_cparams(("parallel", "parallel")),
        name="mlstm",
    )(qk3, qk3, proj3, proj3, gcol, grow, norm_g.reshape(1, D_ML))


def _sb_kernel(q_ref, k_ref, v_ref, g_ref, o_ref):
    i = pl.program_id(2)
    W = SB_KEY_BLOCKS * BLK
    r1 = lax.broadcasted_iota(jnp.int32, (BLK, BLK), 0)
    c1 = lax.broadcasted_iota(jnp.int32, (BLK, BLK), 1)
    upper_strict = jnp.where(r1 > c1, 1.0, 0.0).astype(BF16)
    rowp = lax.broadcasted_iota(jnp.int32, (BLK, W), 0) + i * BLK
    ci = lax.broadcasted_iota(jnp.int32, (BLK, W), 1)
    scale = HEAD_DIM ** -0.5
    heads = [slice(hh * HEAD_DIM, (hh + 1) * HEAD_DIM) for hh in range(SB_HPS)]
    qs = [(q_ref[0, :, hs].astype(F32) * scale).astype(BF16) for hs in heads]

    def one_head(q, k, v, carry, acc, valid):
        z = lax.dot_general(q, k, (((1,), (1,)), ((), ())), preferred_element_type=F32)
        sp = _softplus(z)
        lk = jnp.where(valid, -sp, 0.0)
        hi = lk.astype(BF16)
        lo = (lk - hi.astype(F32)).astype(BF16)
        pieces = [None] * SB_KEY_BLOCKS
        off = carry
        for c in reversed(range(SB_KEY_BLOCKS)):
            sl = slice(c * BLK, (c + 1) * BLK)
            local = (jnp.dot(hi[:, sl], upper_strict, preferred_element_type=F32)
                     + jnp.dot(lo[:, sl], upper_strict, preferred_element_type=F32))
            pieces[c] = local + off
            off = off + jnp.sum(lk[:, sl], axis=-1, keepdims=True)
        after = jnp.concatenate(pieces, axis=1)
        a = jnp.where(valid, jnp.exp(z - sp + after), 0.0)
        return off, acc + jnp.dot(a.astype(BF16), v, preferred_element_type=F32)

    def step(t, state):
        end = (i + 1) * BLK - t * W
        s0 = pl.multiple_of(jnp.maximum(end - W, 0), BLK)
        colp = ci + s0
        valid = jnp.logical_and(colp < jnp.minimum(rowp, end), colp >= PAD)
        return tuple(one_head(qs[hh], k_ref[0, pl.ds(s0, W), hs], v_ref[0, pl.ds(s0, W), hs],
                              carry, acc, valid)
                     for hh, (hs, (carry, acc)) in enumerate(zip(heads, state)))

    init = tuple((jnp.zeros((BLK, 1), F32), jnp.zeros((BLK, HEAD_DIM), F32)) for _ in heads)
    n_steps = (i + SB_KEY_BLOCKS) // SB_KEY_BLOCKS
    final = lax.fori_loop(0, n_steps, step, init)
    for hs, (_, acc) in zip(heads, final):
        hn = acc * lax.rsqrt(jnp.mean(acc * acc, axis=-1, keepdims=True) + RMS_EPS) * g_ref[:, hs]
        o_ref[0, :, hs] = hn.astype(BF16)


def _stick_breaking(proj3, norm_g):
    B, Lp, _ = proj3.shape
    nq = Lp // BLK
    return pl.pallas_call(
        _sb_kernel,
        grid=(B, SB_HEADS // SB_HPS, nq),
        in_specs=[pl.BlockSpec((1, BLK, SB_W), lambda b, h, i: (b, i, C_SBQ // SB_W + h)),
                  pl.BlockSpec((1, Lp, SB_W), lambda b, h, i: (b, 0, C_SBK // SB_W + h)),
                  pl.BlockSpec((1, Lp, SB_W), lambda b, h, i: (b, 0, C_SBV // SB_W + h)),
                  pl.BlockSpec((1, SB_W), lambda b, h, i: (0, h))],
        out_specs=pl.BlockSpec((1, BLK, SB_W), lambda b, h, i: (b, i, h)),
        out_shape=jax.ShapeDtypeStruct((B, Lp, D_SB), BF16),
        compiler_params=_cparams(("parallel", "parallel", "parallel")),
        name="stick_breaking",
    )(proj3, proj3, proj3, norm_g.reshape(1, D_SB))


def _pool_kernel(x_ref, w_ref, s_ref, o_ref):
    g = pl.program_id(1)
    x = x_ref[0].astype(F32)
    row = lax.broadcasted_iota(jnp.int32, x.shape, 0)
    x = jnp.where(row < PAD, 0.0, x)
    pos1 = (row - PAD + 1).astype(F32)
    sums = x
    mean = jnp.zeros_like(x)
    shift = 1
    for gi, win in enumerate(POOL_WINDOWS):
        while shift < win:
            sums = sums + pltpu.roll(sums, shift, axis=0)
            shift *= 2
        cnt = jnp.maximum(jnp.minimum(pos1, float(win)), 1.0)
        mean = jnp.where(g == gi, sums / cnt, mean)
    p = (mean - x).astype(BF16)
    y = jnp.dot(p, w_ref[0].astype(BF16), preferred_element_type=F32) * s_ref[...]
    o_ref[0] = y.astype(BF16)


def _pool(proj3, pool_w, pool_scale):
    B, Lp, _ = proj3.shape
    return pl.pallas_call(
        _pool_kernel,
        grid=(B, POOL_GROUPS),
        in_specs=[pl.BlockSpec((1, Lp, HEAD_DIM), lambda b, g: (b, 0, C_POOL // HEAD_DIM + g)),
                  pl.BlockSpec((1, HEAD_DIM, HEAD_DIM), lambda b, g: (g, 0, 0)),
                  pl.BlockSpec((1, HEAD_DIM), lambda b, g: (0, g))],
        out_specs=pl.BlockSpec((1, Lp, HEAD_DIM), lambda b, g: (b, 0, g)),
        out_shape=jax.ShapeDtypeStruct((B, Lp, D_POOL), BF16),
        compiler_params=_cparams(("parallel", "parallel")),
        name="pool_mixer",
    )(proj3, pool_w, pool_scale.reshape(1, D_POOL))


def _to_pitched(ref, y):
    n = y.shape[0]
    for s in range(XROWS):
        ref[pl.ds(s, n, stride=ROW_PITCH), :] = y[:, s * BLK:(s + 1) * BLK]
    ref[pl.ds(XROWS, n, stride=ROW_PITCH), :] = jnp.zeros((n, BLK), y.dtype)


def _from_pitched(ref, slot, n, first=0):
    return jnp.concatenate(
        [ref[slot, pl.ds(first * ROW_PITCH + s, n, stride=ROW_PITCH), :] for s in range(XROWS)], axis=1)


def _outproj_kernel(a1_ref, a2_ref, a3_ref, w_ref, h_ref, g_ref, b_ref, o_ref, ot_ref, *, alpha):
    acc = jnp.dot(a1_ref[...], w_ref[0:D_ML, :], preferred_element_type=F32)
    acc = acc + jnp.dot(a2_ref[...], w_ref[D_ML:D_ML + D_SB, :], preferred_element_type=F32)
    acc = acc + jnp.dot(a3_ref[...], w_ref[D_ML + D_SB:, :], preferred_element_type=F32)
    y = _layer_norm(alpha * h_ref[...] + acc, g_ref[...], b_ref[...])
    o_ref[...] = y
    _to_pitched(ot_ref, y)


def _outproj_ln(a_ml, a_sb, a_pool, w, h, g, b, alpha):
    T, D = h.shape
    row = lambda n: pl.BlockSpec((ROW_TILE, n), lambda i: (i, 0))
    vec = pl.BlockSpec((1, D), lambda i: (0, 0))
    return pl.pallas_call(
        functools.partial(_outproj_kernel, alpha=alpha),
        grid=(T // ROW_TILE,),
        in_specs=[row(D_ML), row(D_SB), row(D_POOL),
                  pl.BlockSpec(w.shape, lambda i: (0, 0)), row(D), vec, vec],
        out_specs=[row(D), pl.BlockSpec((ROW_TILE * ROW_PITCH, BLK), lambda i: (i, 0))],
        out_shape=[jax.ShapeDtypeStruct((T, D), F32), jax.ShapeDtypeStruct((T * ROW_PITCH, BLK), F32)],
        compiler_params=_cparams(("parallel",)),
        name="out_proj_ln",
    )(a_ml, a_sb, a_pool, w, h, g.reshape(1, D), b.reshape(1, D))


def _router_kernel(h_ref, rw_ref, rb_ref, idx_ref, gate_ref, rank_ref, cnt_ref, run_ref, *, tm):
    first = jnp.logical_and(pl.program_id(0) == 0, pl.program_id(1) == 0)

    @pl.when(first)
    def _():
        run_ref[...] = jnp.zeros_like(run_ref)

    x = h_ref[0]
    logits = jnp.dot(x, rw_ref[...], preferred_element_type=F32, precision=lax.Precision.HIGHEST)
    scores = _sigmoid(logits)
    work = scores + rb_ref[...]
    E = scores.shape[1]
    lane = lax.broadcasted_iota(jnp.int32, (tm, E), 1).astype(F32)
    row = lax.broadcasted_iota(jnp.int32, (tm, 1), 0) + pl.program_id(1) * tm
    real = (row >= PAD).astype(F32)

    picks = []
    gates = []
    member = jnp.zeros((tm, E), F32)
    for _ in range(TOP_K):
        mx = jnp.max(work, axis=-1, keepdims=True)
        pick = jnp.min(jnp.where(work == mx, lane, float(E)), axis=-1, keepdims=True)
        onehot = lane == pick
        gates.append(jnp.sum(jnp.where(onehot, scores, 0.0), axis=-1, keepdims=True))
        picks.append(pick)
        member = member + jnp.where(onehot, real, 0.0)
        work = jnp.where(onehot, -jnp.inf, work)
    gsum = gates[0]
    for gk in gates[1:]:
        gsum = gsum + gk

    ri = lax.broadcasted_iota(jnp.int32, (tm, tm), 0)
    ci = lax.broadcasted_iota(jnp.int32, (tm, tm), 1)
    lower_strict = jnp.where(ci < ri, 1.0, 0.0).astype(BF16)
    before = jnp.dot(lower_strict, member.astype(BF16), preferred_element_type=F32) + run_ref[...]

    lane_o = lax.broadcasted_iota(jnp.int32, (tm, BLK), 1)
    idx_o = jnp.zeros((tm, BLK), F32)
    gate_o = jnp.zeros((tm, BLK), F32)
    rank_o = jnp.zeros((tm, BLK), F32)
    for kk in range(TOP_K):
        onehot = lane == picks[kk]
        rk = jnp.sum(jnp.where(onehot, before, 0.0), axis=-1, keepdims=True)
        sel = lane_o == kk
        idx_o = jnp.where(sel, picks[kk], idx_o)
        gate_o = jnp.where(sel, gates[kk] / gsum * ROUTED_SCALE, gate_o)
        rank_o = jnp.where(sel, rk, rank_o)
    idx_ref[0] = idx_o.astype(jnp.int32)
    gate_ref[0] = gate_o
    rank_ref[0] = rank_o.astype(jnp.int32)
    run_ref[...] = run_ref[...] + jnp.sum(member, axis=0, keepdims=True)
    cnt_ref[...] = run_ref[...].astype(jnp.int32)


def _router(h3, router_w, router_bias):
    B, Lp, D = h3.shape
    E = router_w.shape[1]
    tm = Lp // 4
    out = lambda: pl.BlockSpec((1, tm, BLK), lambda b, j: (b, j, 0))
    return pl.pallas_call(
        functools.partial(_router_kernel, tm=tm),
        grid=(B, Lp // tm),
        in_specs=[pl.BlockSpec((1, tm, D), lambda b, j: (b, j, 0)),
                  pl.BlockSpec((D, E), lambda b, j: (0, 0)),
                  pl.BlockSpec((1, E), lambda b, j: (0, 0))],
        out_specs=[out(), out(), out(), pl.BlockSpec((1, E), lambda b, j: (0, 0))],
        out_shape=[jax.ShapeDtypeStruct((B, Lp, BLK), jnp.int32),
                   jax.ShapeDtypeStruct((B, Lp, BLK), F32),
                   jax.ShapeDtypeStruct((B, Lp, BLK), jnp.int32),
                   jax.ShapeDtypeStruct((1, E), jnp.int32)],
        scratch_shapes=[pltpu.VMEM((1, E), F32)],
        compiler_params=_cparams(("arbitrary", "arbitrary")),
        name="router",
    )(h3, router_w, router_bias.reshape(1, E))


def _swiglu(x, w13, w2):
    h1 = jnp.dot(x, w13, preferred_element_type=F32)
    f = h1.shape[1] // 2
    gate, up = h1[:, :f], h1[:, f:]
    act = (gate * _sigmoid(gate) * up).astype(BF16)
    return jnp.dot(act, w2, preferred_element_type=F32)


def _shared_kernel(x_ref, w13_ref, w2_ref, o_ref):
    o_ref[...] = _swiglu(x_ref[...].astype(BF16), w13_ref[...], w2_ref[...])


def _shared_expert(h, w13, w2):
    T, D = h.shape
    row = pl.BlockSpec((ROW_TILE, D), lambda i: (i, 0))
    return pl.pallas_call(
        _shared_kernel,
        grid=(T // ROW_TILE,),
        in_specs=[row, pl.BlockSpec(w13.shape, lambda i: (0, 0)),
                  pl.BlockSpec(w2.shape, lambda i: (0, 0))],
        out_specs=row,
        out_shape=jax.ShapeDtypeStruct((T, D), F32),
        compiler_params=_cparams(("parallel",)),
        name="shared_expert",
    )(h, w13, w2)


def _moe_kernel(be_ref, bs_ref, bo_ref, bn_ref, nb_ref, tok_hbm, x_hbm, w13_hbm, w2_hbm, y_ref,
                win0, win1, wsem, xbuf, xsem, wf13, wf2, fsem, w13b, w2b, *, layer):
    i = pl.program_id(0)
    n_used = nb_ref[0]
    slot = lax.rem(i, 2)
    F = w2b.shape[0]
    wins = (win0, win1)

    def by_parity(p, fn):
        for s in range(2):
            pl.when(p == s)(functools.partial(fn, s))

    def window(blk, s):
        base = pl.multiple_of(lax.shift_left(lax.shift_right_logical(bs_ref[blk], TOK_ALIGN_LOG2),
                                             TOK_ALIGN_LOG2), TOK_ALIGN)
        return pltpu.make_async_copy(tok_hbm.at[pl.ds(base, TOK_WINDOW)], wins[s], wsem.at[s])

    def weights(e, s):
        return (pltpu.make_async_copy(w13_hbm.at[layer, e], wf13.at[s], fsem.at[0, s]),
                pltpu.make_async_copy(w2_hbm.at[layer, e], wf2.at[s], fsem.at[1, s]))

    def row_copy(tok, s, r):
        return pltpu.make_async_copy(x_hbm.at[pl.ds(tok * ROW_PITCH, XROWS), :],
                                     xbuf.at[s, pl.ds(r * ROW_PITCH, XROWS), :], xsem.at[s])

    def rows_wait(s):
        rows = x_hbm.at[pl.ds(0, EXPERT_TILE * XROWS), :]
        pltpu.make_async_copy(rows, rows, xsem.at[s]).wait()

    @pl.when(jnp.logical_and(i == 0, n_used > 0))
    def _():
        for c in weights(be_ref[0], 0):
            c.start()
        window(0, 0).start()
        window(0, 0).wait()
        first0 = jnp.bitwise_and(bs_ref[0], TOK_ALIGN - 1)

        def body(r, c):
            row_copy(win0[first0 + r], 0, r).start()
            return c
        lax.fori_loop(0, EXPERT_TILE, body, 0)

        @pl.when(n_used > 1)
        def _():
            window(1, 1).start()

    @pl.when(i + 1 < n_used)
    def _():
        by_parity(1 - slot, lambda s: window(i + 1, s).wait())

    @pl.when(i + 2 < n_used)
    def _():
        by_parity(slot, lambda s: window(i + 2, s).start())

    @pl.when(i < n_used)
    def _():
        e = be_ref[i]
        ws = lax.rem(bo_ref[i], 2)

        @pl.when(jnp.logical_or(i == 0, e != be_ref[jnp.maximum(i - 1, 0)]))
        def _():
            for c in weights(e, ws):
                c.wait()
            nxt = bn_ref[i]

            @pl.when(nxt < n_used)
            def _():
                for c in weights(be_ref[nxt], 1 - ws):
                    c.start()
            w13b[...] = wf13[ws].astype(BF16)
            w2b[...] = wf2[ws].astype(BF16)

        nblk = jnp.minimum(i + 1, n_used - 1)
        first = jnp.bitwise_and(bs_ref[nblk], TOK_ALIGN - 1)

        def issue(s):
            for r in range(EXPERT_TILE):
                row_copy(wins[s][first + r], 1 - slot, r).start(priority=r % 2)
        by_parity(lax.rem(nblk, 2), issue)

        rows_wait(slot)
        x = _from_pitched(xbuf, slot, EXPERT_TILE).astype(BF16)
        acts = []
        half = F // 2
        for c in range(2):
            gate = jnp.dot(x, w13b[:, c * half:(c + 1) * half], preferred_element_type=F32)
            up = jnp.dot(x, w13b[:, F + c * half:F + (c + 1) * half], preferred_element_type=F32)
            acts.append((gate * _sigmoid(gate) * up).astype(BF16))
        act = jnp.concatenate(acts, axis=1)
        for c in range(XROWS // 4):
            y = jnp.dot(act, w2b[:, c * 4 * BLK:(c + 1) * 4 * BLK], preferred_element_type=F32)
            for s in range(4):
                y_ref[pl.ds(c * 4 + s, EXPERT_TILE, stride=ROW_PITCH), :] = y[:, s * BLK:(s + 1) * BLK]
        y_ref[pl.ds(XROWS, EXPERT_TILE, stride=ROW_PITCH), :] = jnp.zeros((EXPERT_TILE, BLK), F32)

        @pl.when(i == n_used - 1)
        def _():
            rows_wait(1 - slot)

    @pl.when(i >= n_used)
    def _():
        y_ref[...] = jnp.zeros_like(y_ref)


def _routed_experts(x_tiles, sorted_tok, block_e, block_start, block_ord, block_next, n_used, w13, w2, layer):
    nb = block_e.shape[0]
    _, E, D, F2 = w13.shape
    F = F2 // 2
    grid_spec = pltpu.PrefetchScalarGridSpec(
        num_scalar_prefetch=5,
        grid=(nb,),
        in_specs=[pl.BlockSpec(memory_space=pl.ANY)] * 4,
        out_specs=pl.BlockSpec((EXPERT_TILE * ROW_PITCH, BLK), lambda i, *_: (i, 0)),
        scratch_shapes=[pltpu.SMEM((TOK_WINDOW,), jnp.int32),
                        pltpu.SMEM((TOK_WINDOW,), jnp.int32),
                        pltpu.SemaphoreType.DMA((2,)),
                        pltpu.VMEM((2, EXPERT_TILE * ROW_PITCH, BLK), F32),
                        pltpu.SemaphoreType.DMA((2,)),
                        pltpu.VMEM((2, D, F2), F32),
                        pltpu.VMEM((2, F, D), F32),
                        pltpu.SemaphoreType.DMA((2, 2)),
                        pltpu.VMEM((D, F2), BF16),
                        pltpu.VMEM((F, D), BF16)],
    )
    return pl.pallas_call(
        functools.partial(_moe_kernel, layer=layer),
        grid_spec=grid_spec,
        out_shape=jax.ShapeDtypeStruct((nb * EXPERT_TILE * ROW_PITCH, BLK), F32),
        compiler_params=_cparams(("arbitrary",)),
        name="routed_experts",
    )(block_e, block_start, block_ord, block_next, n_used, sorted_tok, x_tiles, w13, w2)


def _combine_kernel(pos_ref, posn_ref, y_hbm, gate_ref, sh_ref, h_ref, g_ref, b_ref,
                    o_ref, *rest, alpha, n_steps):
    ob_ref = rest[0] if len(rest) == 3 else None
    gbuf, sem = rest[-2:]
    i = pl.program_id(0)
    slot = lax.rem(i, 2)
    n_rows = TOP_K * COMBINE_TILE

    def copy(pos, s, r):
        return pltpu.make_async_copy(y_hbm.at[pl.ds(pos * ROW_PITCH, XROWS), :],
                                     gbuf.at[s, pl.ds(r * ROW_PITCH, XROWS), :], sem.at[s])

    def gather(pref, s):
        def body(r8, c):
            for u in range(8):
                r = r8 * 8 + u
                copy(pref[0, 0, r], s, r).start(priority=u % 2)
            return c
        lax.fori_loop(0, n_rows // 8, body, 0, unroll=2)

    @pl.when(i == 0)
    def _():
        gather(pos_ref, 0)

    @pl.when(i + 1 < n_steps)
    def _():
        gather(posn_ref, 1 - slot)

    all_rows = y_hbm.at[pl.ds(0, n_rows * XROWS), :]
    pltpu.make_async_copy(all_rows, all_rows, sem.at[slot]).wait()

    gate = gate_ref[...]
    acc = alpha * h_ref[...] + sh_ref[...]
    for kk in range(TOP_K):
        rows = _from_pitched(gbuf, slot, COMBINE_TILE, first=kk * COMBINE_TILE)
        acc = acc + gate[:, kk:kk + 1] * rows
    y = _layer_norm(acc, g_ref[...], b_ref[...])
    o_ref[...] = y
    if ob_ref is not None:
        ob_ref[...] = y.astype(BF16)


def _combine_ln(pos_blocks, y, gate, shared, h, g, b, alpha, seq_blocks=None):
    T, D = h.shape
    n_steps = T // COMBINE_TILE
    n_rows = TOP_K * COMBINE_TILE
    row = lambda n: pl.BlockSpec((COMBINE_TILE, n), lambda i: (i, 0))
    vec = pl.BlockSpec((1, D), lambda i: (0, 0))
    if seq_blocks is None:
        out_specs = [row(D), row(D)]
        out_shape = [jax.ShapeDtypeStruct((T, D), F32), jax.ShapeDtypeStruct((T, D), BF16)]
    else:
        keep = seq_blocks - 1
        out_map = lambda i: ((i // seq_blocks) * keep + jnp.maximum(i % seq_blocks - 1, 0), 0)
        out_specs = [pl.BlockSpec((COMBINE_TILE, D), out_map)]
        out_shape = [jax.ShapeDtypeStruct((T // seq_blocks * keep, D), F32)]
    return pl.pallas_call(
        functools.partial(_combine_kernel, alpha=alpha, n_steps=n_steps),
        grid=(n_steps,),
        in_specs=[
            pl.BlockSpec((1, 1, n_rows), lambda i: (i, 0, 0), memory_space=pltpu.SMEM),
            pl.BlockSpec((1, 1, n_rows), lambda i: (jnp.minimum(i + 1, n_steps - 1), 0, 0),
                         memory_space=pltpu.SMEM),
            pl.BlockSpec(memory_space=pl.ANY),
            row(BLK), row(D), row(D), vec, vec],
        out_specs=out_specs,
        out_shape=out_shape,
        scratch_shapes=[pltpu.VMEM((2, n_rows * ROW_PITCH, BLK), F32), pltpu.SemaphoreType.DMA((2,))],
        compiler_params=_cparams(("arbitrary",)),
        name="combine_ln",
    )(pos_blocks, pos_blocks, y, gate, shared, h, g.reshape(1, D), b.reshape(1, D))


def _wprep_kernel(w_ref, o_ref):
    o_ref[:, 0:C_SBQ] = w_ref[0, :, 0:C_SBQ].astype(BF16)
    o_ref[:, C_SBQ:N_MAIN] = w_ref[0, :, O_SB:O_SB + N_MAIN - C_SBQ].astype(BF16)


def _prep_in_proj(w_in_all, layer):
    _, D, n_in = w_in_all.shape
    tr = 256
    return pl.pallas_call(
        _wprep_kernel,
        grid=(D // tr,),
        in_specs=[pl.BlockSpec((1, tr, n_in), lambda i: (layer, i, 0))],
        out_specs=pl.BlockSpec((tr, N_MAIN), lambda i: (i, 0)),
        out_shape=jax.ShapeDtypeStruct((D, N_MAIN), BF16),
        compiler_params=_cparams(("parallel",)),
        name="w_in_prep",
    )(w_in_all)


def _arrange_gates(w_gates, b_igate, b_fgate):
    D = w_gates.shape[0]
    wg = jnp.zeros((D, ML_GROUPS, BLK), F32)
    bias = jnp.zeros((ML_GROUPS, BLK), F32)
    w_i = w_gates[:, :ML_HEADS].reshape(D, ML_GROUPS, ML_HPS)
    w_f = w_gates[:, ML_HEADS:].reshape(D, ML_GROUPS, ML_HPS)
    wg = wg.at[:, :, :ML_HPS].set(w_i).at[:, :, ML_HPS:2 * ML_HPS].set(w_f)
    bias = bias.at[:, :ML_HPS].set(b_igate.reshape(ML_GROUPS, ML_HPS))
    bias = bias.at[:, ML_HPS:2 * ML_HPS].set(b_fgate.reshape(ML_GROUPS, ML_HPS))
    return wg.reshape(D, ML_GROUPS * BLK).astype(BF16), bias.reshape(1, ML_GROUPS * BLK)


def _dispatch_plan(idx, rank, counts, B, Lp):
    T = B * Lp
    A = T * TOP_K
    A_real = B * (Lp - PAD) * TOP_K
    nb = -(-A_real // EXPERT_TILE) + N_EXPERTS
    E = N_EXPERTS
    eids = jnp.arange(E, dtype=jnp.int32)
    padded = (counts + EXPERT_TILE - 1) // EXPERT_TILE * EXPERT_TILE
    ends = jnp.cumsum(padded)
    poff = ends - padded
    off = jnp.cumsum(counts) - counts
    idx_k = idx[:, :, :TOP_K]
    onehot = idx_k[..., None] == eids
    pos = jnp.sum(jnp.where(onehot, poff, 0), axis=-1) + rank[:, :, :TOP_K]
    real = (jnp.arange(Lp) >= PAD)[None, :, None]
    pos = jnp.where(real, pos, 0).astype(jnp.int32)
    a_bits = (A - 1).bit_length()
    assert (E + 1) << a_bits < 2 ** 31
    keys = jnp.where(real, idx_k, E).reshape(A)
    packed = lax.shift_left(keys, a_bits) + jnp.arange(A, dtype=jnp.int32)
    sorted_tok = jnp.bitwise_and(jnp.sort(packed), (1 << a_bits) - 1) // TOP_K
    n_tok = -(-A // TOK_ALIGN) * TOK_ALIGN + TOK_WINDOW
    sorted_tok = jnp.concatenate([sorted_tok, jnp.full((n_tok - A,), PAD, jnp.int32)])
    blk0 = jnp.arange(nb, dtype=jnp.int32) * EXPERT_TILE
    block_e = jnp.minimum(jnp.sum(ends[None, :] <= blk0[:, None], axis=1), E - 1).astype(jnp.int32)
    sel = block_e[:, None] == eids
    block_start = jnp.sum(jnp.where(sel, off - poff, 0), axis=1) + blk0
    block_start = jnp.clip(block_start, 0, A).astype(jnp.int32)
    n_used = (ends[-1] // EXPERT_TILE).astype(jnp.int32).reshape(1)
    changed = jnp.concatenate([jnp.zeros((1,), jnp.int32),
                               (block_e[1:] != block_e[:-1]).astype(jnp.int32)])
    block_ord = jnp.cumsum(changed).astype(jnp.int32)
    block_next = (jnp.sum(jnp.where(sel, ends, 0), axis=1) // EXPERT_TILE).astype(jnp.int32)
    return pos.reshape(T, TOP_K), sorted_tok, (block_e, block_start, block_ord, block_next, n_used)


def kernel(x, meta, ln_in_g, ln_in_b, w_in, b_igate, b_fgate, conv_qk, ml_norm_g, sb_norm_g, pool_w, pool_scale, w_out, ln1_g, ln1_b, router_w, router_bias, exp_w13, exp_w2, sh_w13, sh_w2, ln2_g, ln2_b):
    B, S, D = x.shape
    depth = w_in.shape[0]
    alpha = (2 * depth) ** 0.25
    Lp = PAD + N_META + S
    T = B * Lp
    assert Lp % BLK == 0 and T % ROW_TILE == 0 and T % COMBINE_TILE == 0
    assert D == XROWS * BLK and COMBINE_TILE == PAD + N_META

    h0 = jnp.concatenate([jnp.zeros((B, PAD, D), x.dtype),
                          jnp.broadcast_to(meta[None].astype(x.dtype), (B, N_META, D)), x], axis=1)
    h, hb = _ln_rows(h0.reshape(T, D), ln_in_g, ln_in_b)

    for l in range(depth):
        w_gate, gate_bias = _arrange_gates(w_in[l, :, C_SBQ:O_SB], b_igate[l], b_fgate[l])
        proj3 = _in_proj(hb, _prep_in_proj(w_in, l)).reshape(B, Lp, N_MAIN)
        gcol = _gates(hb.reshape(B, Lp, D), w_gate, gate_bias)
        grow = jnp.transpose(gcol.reshape(B, Lp, ML_GROUPS, BLK)[..., :8], (0, 2, 3, 1))
        qk3 = _conv_silu(proj3, conv_qk[l])
        a_ml = _mlstm(qk3, proj3, gcol, grow, ml_norm_g[l])
        a_sb = _stick_breaking(proj3, sb_norm_g[l])
        a_pool = _pool(proj3, pool_w[l], pool_scale[l])
        h1, h1_tiles = _outproj_ln(a_ml.reshape(T, D_ML), a_sb.reshape(T, D_SB), a_pool.reshape(T, D_POOL),
                                   w_out[l].astype(BF16), h, ln1_g[l], ln1_b[l], alpha)

        idx, gate, rank, counts = _router(h1.reshape(B, Lp, D), router_w[l], router_bias[l])
        pos, sorted_tok, tables = _dispatch_plan(idx, rank, counts[0], B, Lp)
        y = _routed_experts(h1_tiles, sorted_tok, *tables, exp_w13, exp_w2, l)
        shared = _shared_expert(h1, sh_w13[l].astype(BF16), sh_w2[l].astype(BF16))
        n_steps = T // COMBINE_TILE
        pos_blocks = jnp.transpose(pos.reshape(n_steps, COMBINE_TILE, TOP_K), (0, 2, 1))
        pos_blocks = pos_blocks.reshape(n_steps, 1, TOP_K * COMBINE_TILE)
        args = (pos_blocks, y, gate.reshape(T, BLK), shared, h1, ln2_g[l], ln2_b[l], alpha)
        if l + 1 < depth:
            h, hb = _combine_ln(*args)
        else:
            (out,) = _combine_ln(*args, seq_blocks=Lp // COMBINE_TILE)

    return out.reshape(B, S, D)
```

```python
import functools

import jax
import jax.numpy as jnp
from jax import lax
from jax.experimental import pallas as pl
from jax.experimental.pallas import tpu as pltpu

N_META = 16
HEAD_DIM = 128
ML_HEADS = 6
SB_HEADS = 6
POOL_WINDOWS = (2, 4, 8, 16)
POOL_GROUPS = 4
D_ML = ML_HEADS * HEAD_DIM
D_SB = SB_HEADS * HEAD_DIM
D_POOL = POOL_GROUPS * HEAD_DIM
CONV_W = 4
N_EXPERTS = 64
TOP_K = 8
ROUTED_SCALE = 2.5
LN_EPS = 1e-5
RMS_EPS = 1e-6

BLK = 128
PAD = (-N_META) % BLK
ML_HPS = 3
ML_GROUPS = ML_HEADS // ML_HPS
SB_KEY_BLOCKS = 8
SB_HPS = 3
SB_W = SB_HPS * HEAD_DIM
ROW_TILE = 512
EXPERT_TILE = 256
COMBINE_TILE = 128
TOK_ALIGN_LOG2 = 10
TOK_ALIGN = 1 << TOK_ALIGN_LOG2
TOK_WINDOW = 2 * TOK_ALIGN
NEG = -1e30
VMEM_LIMIT = 56 * 1024 * 1024

C_QK = 0
C_V = 2 * D_ML
C_O = C_V + D_ML
C_SBQ = C_O + D_ML
C_SBK = C_SBQ + D_SB
C_SBV = C_SBK + D_SB
C_POOL = C_SBV + D_SB
N_MAIN = C_POOL + D_POOL
O_SB = C_SBQ + 2 * ML_HEADS
XROWS = 16
ROW_PITCH = XROWS + 1

F32 = jnp.float32
BF16 = jnp.bfloat16


def _cparams(sem, vmem=VMEM_LIMIT):
    return pltpu.CompilerParams(dimension_semantics=sem, vmem_limit_bytes=vmem)


def _layer_norm(x, g, b):
    mu = jnp.mean(x, axis=-1, keepdims=True)
    xc = x - mu
    var = jnp.mean(xc * xc, axis=-1, keepdims=True)
    return xc * lax.rsqrt(var + LN_EPS) * g + b


def _sigmoid(x):
    return 1.0 / (1.0 + jnp.exp(-x))


def _softplus(x):
    return jnp.maximum(x, 0.0) + jnp.log(1.0 + jnp.exp(-jnp.abs(x)))


def _split_dot(a, b, split_lhs):
    x = a if split_lhs else b
    out = None
    for _ in range(3):
        part = x.astype(BF16)
        x = x - part.astype(F32)
        term = (jnp.dot(part, b, preferred_element_type=F32) if split_lhs
                else jnp.dot(a, part, preferred_element_type=F32))
        out = term if out is None else out + term
    return out


def _ln_kernel(x_ref, g_ref, b_ref, o_ref, ob_ref):
    y = _layer_norm(x_ref[...], g_ref[...], b_ref[...])
    o_ref[...] = y
    ob_ref[...] = y.astype(BF16)


def _ln_rows(x, g, b):
    T, D = x.shape
    row = pl.BlockSpec((ROW_TILE, D), lambda i: (i, 0))
    vec = pl.BlockSpec((1, D), lambda i: (0, 0))
    return pl.pallas_call(
        _ln_kernel,
        grid=(T // ROW_TILE,),
        in_specs=[row, vec, vec],
        out_specs=[row, row],
        out_shape=[jax.ShapeDtypeStruct((T, D), F32), jax.ShapeDtypeStruct((T, D), BF16)],
        compiler_params=_cparams(("parallel",)),
        name="ln_in",
    )(x, g.reshape(1, D), b.reshape(1, D))


def _mm_kernel(a_ref, w_ref, o_ref):
    o_ref[...] = jnp.dot(a_ref[...], w_ref[...], preferred_element_type=F32).astype(o_ref.dtype)


def _in_proj(a, w):
    M, K = a.shape
    N = w.shape[1]
    tn = N // 2
    return pl.pallas_call(
        _mm_kernel,
        grid=(N // tn, M // ROW_TILE),
        in_specs=[pl.BlockSpec((ROW_TILE, K), lambda j, i: (i, 0)),
                  pl.BlockSpec((K, tn), lambda j, i: (0, j))],
        out_specs=pl.BlockSpec((ROW_TILE, tn), lambda j, i: (i, j)),
        out_shape=jax.ShapeDtypeStruct((M, N), BF16),
        compiler_params=_cparams(("parallel", "parallel")),
        name="in_proj",
    )(a, w)


def _gates_kernel(hb_ref, wg_ref, bias_ref, o_ref, *, tm):
    j = pl.program_id(1)
    g = jnp.dot(hb_ref[0], wg_ref[...], preferred_element_type=F32) + bias_ref[...]
    col = jnp.bitwise_and(lax.broadcasted_iota(jnp.int32, g.shape, 1), BLK - 1)
    row = lax.broadcasted_iota(jnp.int32, g.shape, 0) + j * tm
    is_f = jnp.logical_and(col >= ML_HPS, col < 2 * ML_HPS)
    val = jnp.where(is_f, -_softplus(-g), g)
    pad_val = jnp.where(is_f, 0.0, NEG)
    o_ref[0] = jnp.where(row < PAD, pad_val, val)


def _gates(hb3, wg, bias):
    B, Lp, D = hb3.shape
    tm = Lp // 4
    Ng = wg.shape[1]
    return pl.pallas_call(
        functools.partial(_gates_kernel, tm=tm),
        grid=(B, Lp // tm),
        in_specs=[pl.BlockSpec((1, tm, D), lambda b, j: (b, j, 0)),
                  pl.BlockSpec((D, Ng), lambda b, j: (0, 0)),
                  pl.BlockSpec((1, Ng), lambda b, j: (0, 0))],
        out_specs=pl.BlockSpec((1, tm, Ng), lambda b, j: (b, j, 0)),
        out_shape=jax.ShapeDtypeStruct((B, Lp, Ng), F32),
        compiler_params=_cparams(("parallel", "parallel")),
        name="ml_gates",
    )(hb3, wg, bias)


def _conv_kernel(x_ref, w_ref, o_ref):
    c = pl.program_id(1)
    x = x_ref[0].astype(F32)
    row = lax.broadcasted_iota(jnp.int32, x.shape, 0)
    x = jnp.where(row < PAD, 0.0, x)
    w = w_ref[...]
    y = x * w[CONV_W - 1:CONV_W]
    for s in range(1, CONV_W):
        y = y + pltpu.roll(x, s, axis=0) * w[CONV_W - 1 - s:CONV_W - s]
    y = y * _sigmoid(y)
    scale = jnp.where(c < ML_HEADS, HEAD_DIM ** -0.5, 1.0).astype(F32)
    o_ref[0] = (y * scale).astype(BF16)


def _conv_silu(proj3, conv_w):
    B, Lp, _ = proj3.shape
    nc = 2 * D_ML // BLK
    return pl.pallas_call(
        _conv_kernel,
        grid=(B, nc),
        in_specs=[pl.BlockSpec((1, Lp, BLK), lambda b, c: (b, 0, c)),
                  pl.BlockSpec((CONV_W, BLK), lambda b, c: (0, c))],
        out_specs=pl.BlockSpec((1, Lp, BLK), lambda b, c: (b, 0, c)),
        out_shape=jax.ShapeDtypeStruct((B, Lp, 2 * D_ML), BF16),
        compiler_params=_cparams(("parallel", "parallel")),
        name="conv_silu",
    )(proj3, conv_w)


def _mlstm_kernel(q_ref, k_ref, v_ref, o_ref, gc_ref, gr_ref, ng_ref, out_ref, st_ref, m_ref):
    n_chunks = q_ref.shape[1] // BLK
    st_ref[...] = jnp.zeros_like(st_ref)
    m_ref[...] = jnp.zeros_like(m_ref)

    ri = lax.broadcasted_iota(jnp.int32, (BLK, BLK), 0)
    ci = lax.broadcasted_iota(jnp.int32, (BLK, BLK), 1)
    causal = ci <= ri
    lower_incl = jnp.where(causal, 1.0, 0.0).astype(BF16)
    upper_incl = jnp.where(ri <= ci, 1.0, 0.0).astype(BF16)
    ones_col = jnp.where(ci == 0, 1.0, 0.0).astype(BF16)

    def chunk(c, carry):
        r0 = pl.multiple_of(c * BLK, BLK)
        gcol = gc_ref[0, pl.ds(r0, BLK), :]
        grow = gr_ref[0, 0, :, pl.ds(r0, BLK)]
        bcol = _split_dot(lower_incl, gcol, split_lhs=False)
        brow = _split_dot(grow, upper_incl, split_lhs=True)
        for hh in range(ML_HPS):
            cs = slice(hh * HEAD_DIM, (hh + 1) * HEAD_DIM)
            q = q_ref[0, pl.ds(r0, BLK), cs]
            k = k_ref[0, pl.ds(r0, BLK), cs]
            v = v_ref[0, pl.ds(r0, BLK), cs]
            li_c = gcol[:, hh:hh + 1]
            b_c = bcol[:, ML_HPS + hh:ML_HPS + hh + 1]
            li_r = grow[hh:hh + 1, :]
            b_r = brow[ML_HPS + hh:ML_HPS + hh + 1, :]
            m = m_ref[hh, 0:1, 0:1]
            st = st_ref[hh]

            inter = b_c + m
            dmat = jnp.where(causal, b_c - b_r + li_r, NEG)
            m_t = jnp.maximum(inter, jnp.max(dmat, axis=-1, keepdims=True))
            w_inter = jnp.exp(inter - m_t)
            s_qk = lax.dot_general(q, k, (((1,), (1,)), ((), ())), preferred_element_type=F32)
            ws = jnp.exp(dmat - m_t) * s_qk
            v_aug = jnp.concatenate([v, ones_col], axis=1)
            nd = (w_inter * jnp.dot(q, st.astype(BF16), preferred_element_type=F32)
                  + jnp.dot(ws.astype(BF16), v_aug, preferred_element_type=F32))
            num = nd[:, :HEAD_DIM]
            den = nd[:, HEAD_DIM:HEAD_DIM + 1]
            h = num / jnp.maximum(jnp.abs(den), jnp.exp(-m_t))

            og = o_ref[0, pl.ds(r0, BLK), cs].astype(F32)
            hn = h * lax.rsqrt(jnp.mean(h * h, axis=-1, keepdims=True) + RMS_EPS)
            hn = hn * ng_ref[:, cs] * _sigmoid(og)
            out_ref[0, pl.ds(r0, BLK), cs] = hn.astype(BF16)

            b_last = b_r[:, BLK - 1:BLK]
            g_r = b_last - b_r + li_r
            m_new = jnp.maximum(b_last + m, jnp.max(g_r, axis=-1, keepdims=True))
            wg = jnp.exp(b_last - b_c + li_c - m_new)
            decay = jnp.exp(b_last + m - m_new)
            wv = (wg * v_aug.astype(F32)).astype(BF16)
            kv = lax.dot_general(k, wv, (((0,), (0,)), ((), ())), preferred_element_type=F32)
            st_ref[hh] = decay * st + kv
            m_ref[hh] = jnp.broadcast_to(m_new, m_ref.shape[1:])
        return carry

    lax.fori_loop(0, n_chunks, chunk, 0)


def _mlstm(qk3, proj3, gcol, grow, norm_g):
    B, Lp, _ = qk3.shape
    W = ML_HPS * HEAD_DIM
    nq = D_ML // W
    seq = lambda off: pl.BlockSpec((1, Lp, W), lambda b, g: (b, 0, off + g))
    return pl.pallas_call(
        _mlstm_kernel,
        grid=(B, ML_GROUPS),
        in_specs=[seq(0), seq(nq),
                  pl.BlockSpec((1, Lp, W), lambda b, g: (b, 0, C_V // W + g)),
                  pl.BlockSpec((1, Lp, W), lambda b, g: (b, 0, C_O // W + g)),
                  pl.BlockSpec((1, Lp, BLK), lambda b, g: (b, 0, g)),
                  pl.BlockSpec((1, 1, 8, Lp), lambda b, g: (b, g, 0, 0)),
                  pl.BlockSpec((1, W), lambda b, g: (0, g))],
        out_specs=pl.BlockSpec((1, Lp, W), lambda b, g: (b, 0, g)),
        out_shape=jax.ShapeDtypeStruct((B, Lp, D_ML), BF16),
        scratch_shapes=[pltpu.VMEM((ML_HPS, HEAD_DIM, 2 * HEAD_DIM), F32),
                        pltpu.VMEM((ML_HPS, 8, BLK), F32)],
        compiler_params=_cparams(("parallel", "parallel")),
        name="mlstm",
    )(qk3, qk3, proj3, proj3, gcol, grow, norm_g.reshape(1, D_ML))


def _sb_kernel(q_ref, k_ref, v_ref, g_ref, o_ref):
    i = pl.program_id(2)
    r1 = lax.broadcasted_iota(jnp.int32, (BLK, BLK), 0)
    c1 = lax.broadcasted_iota(jnp.int32, (BLK, BLK), 1)
    upper_strict = jnp.where(r1 > c1, 1.0, 0.0).astype(BF16)
    scale = HEAD_DIM ** -0.5
    heads = [slice(hh * HEAD_DIM, (hh + 1) * HEAD_DIM) for hh in range(SB_HPS)]
    qs = [(q_ref[0, :, hs].astype(F32) * scale).astype(BF16) for hs in heads]

    def one_head(q, k, v, carry, acc, valid, n_blocks):
        z = lax.dot_general(q, k, (((1,), (1,)), ((), ())), preferred_element_type=F32)
        sp = _softplus(z)
        lk = jnp.where(valid, -sp, 0.0)
        hi = lk.astype(BF16)
        lo = (lk - hi.astype(F32)).astype(BF16)
        pieces = [None] * n_blocks
        off = carry
        for c in reversed(range(n_blocks)):
            sl = slice(c * BLK, (c + 1) * BLK)
            local = (jnp.dot(hi[:, sl], upper_strict, preferred_element_type=F32)
                     + jnp.dot(lo[:, sl], upper_strict, preferred_element_type=F32))
            pieces[c] = local + off
            off = off + jnp.sum(lk[:, sl], axis=-1, keepdims=True)
        after = jnp.concatenate(pieces, axis=1)
        a = jnp.where(valid, jnp.exp(z - sp + after), 0.0)
        return off, acc + jnp.dot(a.astype(BF16), v, preferred_element_type=F32)

    def make_step(n_blocks):
        W = n_blocks * BLK
        rowp = lax.broadcasted_iota(jnp.int32, (BLK, W), 0) + i * BLK
        ci = lax.broadcasted_iota(jnp.int32, (BLK, W), 1)

        def step(end, state):
            s0 = pl.multiple_of(jnp.maximum(end - W, 0), BLK)
            colp = ci + s0
            valid = jnp.logical_and(colp < jnp.minimum(rowp, end), colp >= PAD)
            return tuple(one_head(qs[hh], k_ref[0, pl.ds(s0, W), hs], v_ref[0, pl.ds(s0, W), hs],
                                  carry, acc, valid, n_blocks)
                         for hh, (hs, (carry, acc)) in enumerate(zip(heads, state)))
        return step

    wide, narrow = make_step(SB_KEY_BLOCKS), make_step(SB_KEY_BLOCKS // 2)
    init = tuple((jnp.zeros((BLK, 1), F32), jnp.zeros((BLK, HEAD_DIM), F32)) for _ in heads)
    n_wide = (i + 1) // SB_KEY_BLOCKS
    rem = i + 1 - n_wide * SB_KEY_BLOCKS
    state = lax.fori_loop(0, n_wide, lambda t, s: wide((i + 1 - t * SB_KEY_BLOCKS) * BLK, s), init)
    final = lax.cond(
        rem == 0, lambda s: s,
        lambda s: lax.cond(rem <= SB_KEY_BLOCKS // 2,
                           lambda s2: narrow(rem * BLK, s2), lambda s2: wide(rem * BLK, s2), s),
        state)
    for hs, (_, acc) in zip(heads, final):
        hn = acc * lax.rsqrt(jnp.mean(acc * acc, axis=-1, keepdims=True) + RMS_EPS) * g_ref[:, hs]
        o_ref[0, :, hs] = hn.astype(BF16)


def _stick_breaking(proj3, norm_g):
    B, Lp, _ = proj3.shape
    nq = Lp // BLK
    return pl.pallas_call(
        _sb_kernel,
        grid=(B, SB_HEADS // SB_HPS, nq),
        in_specs=[pl.BlockSpec((1, BLK, SB_W), lambda b, h, i: (b, i, C_SBQ // SB_W + h)),
                  pl.BlockSpec((1, Lp, SB_W), lambda b, h, i: (b, 0, C_SBK // SB_W + h)),
                  pl.BlockSpec((1, Lp, SB_W), lambda b, h, i: (b, 0, C_SBV // SB_W + h)),
                  pl.BlockSpec((1, SB_W), lambda b, h, i: (0, h))],
        out_specs=pl.BlockSpec((1, BLK, SB_W), lambda b, h, i: (b, i, h)),
        out_shape=jax.ShapeDtypeStruct((B, Lp, D_SB), BF16),
        compiler_params=_cparams(("parallel", "parallel", "parallel")),
        name="stick_breaking",
    )(proj3, proj3, proj3, norm_g.reshape(1, D_SB))


def _pool_kernel(x_ref, w_ref, s_ref, o_ref):
    g = pl.program_id(1)
    x = x_ref[0].astype(F32)
    row = lax.broadcasted_iota(jnp.int32, x.shape, 0)
    x = jnp.where(row < PAD, 0.0, x)
    pos1 = (row - PAD + 1).astype(F32)
    sums = x
    mean = jnp.zeros_like(x)
    shift = 1
    for gi, win in enumerate(POOL_WINDOWS):
        while shift < win:
            sums = sums + pltpu.roll(sums, shift, axis=0)
            shift *= 2
        cnt = jnp.maximum(jnp.minimum(pos1, float(win)), 1.0)
        mean = jnp.where(g == gi, sums / cnt, mean)
    p = (mean - x).astype(BF16)
    y = jnp.dot(p, w_ref[0].astype(BF16), preferred_element_type=F32) * s_ref[...]
    o_ref[0] = y.astype(BF16)


def _pool(proj3, pool_w, pool_scale):
    B, Lp, _ = proj3.shape
    return pl.pallas_call(
        _pool_kernel,
        grid=(B, POOL_GROUPS),
        in_specs=[pl.BlockSpec((1, Lp, HEAD_DIM), lambda b, g: (b, 0, C_POOL // HEAD_DIM + g)),
                  pl.BlockSpec((1, HEAD_DIM, HEAD_DIM), lambda b, g: (g, 0, 0)),
                  pl.BlockSpec((1, HEAD_DIM), lambda b, g: (0, g))],
        out_specs=pl.BlockSpec((1, Lp, HEAD_DIM), lambda b, g: (b, 0, g)),
        out_shape=jax.ShapeDtypeStruct((B, Lp, D_POOL), BF16),
        compiler_params=_cparams(("parallel", "parallel")),
        name="pool_mixer",
    )(proj3, pool_w, pool_scale.reshape(1, D_POOL))


def _to_pitched(ref, y):
    n = y.shape[0]
    for s in range(XROWS):
        ref[pl.ds(s, n, stride=ROW_PITCH), :] = y[:, s * BLK:(s + 1) * BLK]
    ref[pl.ds(XROWS, n, stride=ROW_PITCH), :] = jnp.zeros((n, BLK), y.dtype)


def _from_pitched(ref, slot, n, first=0):
    return jnp.concatenate(
        [ref[slot, pl.ds(first * ROW_PITCH + s, n, stride=ROW_PITCH), :] for s in range(XROWS)], axis=1)


def _outproj_kernel(a1_ref, a2_ref, a3_ref, w_ref, h_ref, g_ref, b_ref, o_ref, ot_ref, *, alpha):
    acc = jnp.dot(a1_ref[...], w_ref[0:D_ML, :], preferred_element_type=F32)
    acc = acc + jnp.dot(a2_ref[...], w_ref[D_ML:D_ML + D_SB, :], preferred_element_type=F32)
    acc = acc + jnp.dot(a3_ref[...], w_ref[D_ML + D_SB:, :], preferred_element_type=F32)
    y = _layer_norm(alpha * h_ref[...] + acc, g_ref[...], b_ref[...])
    o_ref[...] = y
    _to_pitched(ot_ref, y)


def _outproj_ln(a_ml, a_sb, a_pool, w, h, g, b, alpha):
    T, D = h.shape
    row = lambda n: pl.BlockSpec((ROW_TILE, n), lambda i: (i, 0))
    vec = pl.BlockSpec((1, D), lambda i: (0, 0))
    return pl.pallas_call(
        functools.partial(_outproj_kernel, alpha=alpha),
        grid=(T // ROW_TILE,),
        in_specs=[row(D_ML), row(D_SB), row(D_POOL),
                  pl.BlockSpec(w.shape, lambda i: (0, 0)), row(D), vec, vec],
        out_specs=[row(D), pl.BlockSpec((ROW_TILE * ROW_PITCH, BLK), lambda i: (i, 0))],
        out_shape=[jax.ShapeDtypeStruct((T, D), F32), jax.ShapeDtypeStruct((T * ROW_PITCH, BLK), F32)],
        compiler_params=_cparams(("parallel",)),
        name="out_proj_ln",
    )(a_ml, a_sb, a_pool, w, h, g.reshape(1, D), b.reshape(1, D))


def _router_kernel(h_ref, rw_ref, rb_ref, idx_ref, gate_ref, rank_ref, cnt_ref, run_ref, *, tm):
    first = jnp.logical_and(pl.program_id(0) == 0, pl.program_id(1) == 0)

    @pl.when(first)
    def _():
        run_ref[...] = jnp.zeros_like(run_ref)

    x = h_ref[0]
    logits = jnp.dot(x, rw_ref[...], preferred_element_type=F32, precision=lax.Precision.HIGHEST)
    scores = _sigmoid(logits)
    work = scores + rb_ref[...]
    E = scores.shape[1]
    lane = lax.broadcasted_iota(jnp.int32, (tm, E), 1).astype(F32)
    row = lax.broadcasted_iota(jnp.int32, (tm, 1), 0) + pl.program_id(1) * tm
    real = (row >= PAD).astype(F32)

    picks = []
    gates = []
    member = jnp.zeros((tm, E), F32)
    for _ in range(TOP_K):
        mx = jnp.max(work, axis=-1, keepdims=True)
        pick = jnp.min(jnp.where(work == mx, lane, float(E)), axis=-1, keepdims=True)
        onehot = lane == pick
        gates.append(jnp.sum(jnp.where(onehot, scores, 0.0), axis=-1, keepdims=True))
        picks.append(pick)
        member = member + jnp.where(onehot, real, 0.0)
        work = jnp.where(onehot, -jnp.inf, work)
    gsum = gates[0]
    for gk in gates[1:]:
        gsum = gsum + gk

    ri = lax.broadcasted_iota(jnp.int32, (tm, tm), 0)
    ci = lax.broadcasted_iota(jnp.int32, (tm, tm), 1)
    lower_strict = jnp.where(ci < ri, 1.0, 0.0).astype(BF16)
    before = jnp.dot(lower_strict, member.astype(BF16), preferred_element_type=F32) + run_ref[...]

    lane_o = lax.broadcasted_iota(jnp.int32, (tm, BLK), 1)
    idx_o = jnp.zeros((tm, BLK), F32)
    gate_o = jnp.zeros((tm, BLK), F32)
    rank_o = jnp.zeros((tm, BLK), F32)
    for kk in range(TOP_K):
        onehot = lane == picks[kk]
        rk = jnp.sum(jnp.where(onehot, before, 0.0), axis=-1, keepdims=True)
        sel = lane_o == kk
        idx_o = jnp.where(sel, picks[kk], idx_o)
        gate_o = jnp.where(sel, gates[kk] / gsum * ROUTED_SCALE, gate_o)
        rank_o = jnp.where(sel, rk, rank_o)
    idx_ref[0] = idx_o.astype(jnp.int32)
    gate_ref[0] = gate_o
    rank_ref[0] = rank_o.astype(jnp.int32)
    run_ref[...] = run_ref[...] + jnp.sum(member, axis=0, keepdims=True)
    cnt_ref[...] = run_ref[...].astype(jnp.int32)


def _router(h3, router_w, router_bias):
    B, Lp, D = h3.shape
    E = router_w.shape[1]
    tm = Lp // 4
    out = lambda: pl.BlockSpec((1, tm, BLK), lambda b, j: (b, j, 0))
    return pl.pallas_call(
        functools.partial(_router_kernel, tm=tm),
        grid=(B, Lp // tm),
        in_specs=[pl.BlockSpec((1, tm, D), lambda b, j: (b, j, 0)),
                  pl.BlockSpec((D, E), lambda b, j: (0, 0)),
                  pl.BlockSpec((1, E), lambda b, j: (0, 0))],
        out_specs=[out(), out(), out(), pl.BlockSpec((1, E), lambda b, j: (0, 0))],
        out_shape=[jax.ShapeDtypeStruct((B, Lp, BLK), jnp.int32),
                   jax.ShapeDtypeStruct((B, Lp, BLK), F32),
                   jax.ShapeDtypeStruct((B, Lp, BLK), jnp.int32),
                   jax.ShapeDtypeStruct((1, E), jnp.int32)],
        scratch_shapes=[pltpu.VMEM((1, E), F32)],
        compiler_params=_cparams(("arbitrary", "arbitrary")),
        name="router",
    )(h3, router_w, router_bias.reshape(1, E))


def _swiglu(x, w13, w2):
    h1 = jnp.dot(x, w13, preferred_element_type=F32)
    f = h1.shape[1] // 2
    gate, up = h1[:, :f], h1[:, f:]
    act = (gate * _sigmoid(gate) * up).astype(BF16)
    return jnp.dot(act, w2, preferred_element_type=F32)


def _shared_kernel(x_ref, w13_ref, w2_ref, o_ref):
    o_ref[...] = _swiglu(x_ref[...].astype(BF16), w13_ref[...], w2_ref[...])


def _shared_expert(h, w13, w2):
    T, D = h.shape
    row = pl.BlockSpec((ROW_TILE, D), lambda i: (i, 0))
    return pl.pallas_call(
        _shared_kernel,
        grid=(T // ROW_TILE,),
        in_specs=[row, pl.BlockSpec(w13.shape, lambda i: (0, 0)),
                  pl.BlockSpec(w2.shape, lambda i: (0, 0))],
        out_specs=row,
        out_shape=jax.ShapeDtypeStruct((T, D), F32),
        compiler_params=_cparams(("parallel",)),
        name="shared_expert",
    )(h, w13, w2)


def _moe_kernel(be_ref, bs_ref, bo_ref, bn_ref, nb_ref, tok_hbm, x_hbm, w13_hbm, w2_hbm, y_ref,
                win0, win1, wsem, xbuf, xsem, wf13, wf2, fsem, w13b, w2b, *, layer):
    i = pl.program_id(0)
    n_used = nb_ref[0]
    slot = lax.rem(i, 2)
    F = w2b.shape[0]
    wins = (win0, win1)

    def by_parity(p, fn):
        for s in range(2):
            pl.when(p == s)(functools.partial(fn, s))

    def window(blk, s):
        base = pl.multiple_of(lax.shift_left(lax.shift_right_logical(bs_ref[blk], TOK_ALIGN_LOG2),
                                             TOK_ALIGN_LOG2), TOK_ALIGN)
        return pltpu.make_async_copy(tok_hbm.at[pl.ds(base, TOK_WINDOW)], wins[s], wsem.at[s])

    def weights(e, s):
        return (pltpu.make_async_copy(w13_hbm.at[layer, e], wf13.at[s], fsem.at[0, s]),
                pltpu.make_async_copy(w2_hbm.at[layer, e], wf2.at[s], fsem.at[1, s]))

    def row_copy(tok, s, r):
        return pltpu.make_async_copy(x_hbm.at[pl.ds(tok * ROW_PITCH, XROWS), :],
                                     xbuf.at[s, pl.ds(r * ROW_PITCH, XROWS), :], xsem.at[s])

    def rows_wait(s):
        rows = x_hbm.at[pl.ds(0, EXPERT_TILE * XROWS), :]
        pltpu.make_async_copy(rows, rows, xsem.at[s]).wait()

    @pl.when(jnp.logical_and(i == 0, n_used > 0))
    def _():
        for c in weights(be_ref[0], 0):
            c.start()
        window(0, 0).start()
        window(0, 0).wait()
        first0 = jnp.bitwise_and(bs_ref[0], TOK_ALIGN - 1)

        def body(r, c):
            row_copy(win0[first0 + r], 0, r).start()
            return c
        lax.fori_loop(0, EXPERT_TILE, body, 0)

        @pl.when(n_used > 1)
        def _():
            window(1, 1).start()

    @pl.when(i + 1 < n_used)
    def _():
        by_parity(1 - slot, lambda s: window(i + 1, s).wait())

    @pl.when(i + 2 < n_used)
    def _():
        by_parity(slot, lambda s: window(i + 2, s).start())

    @pl.when(i < n_used)
    def _():
        e = be_ref[i]
        ws = lax.rem(bo_ref[i], 2)

        @pl.when(jnp.logical_or(i == 0, e != be_ref[jnp.maximum(i - 1, 0)]))
        def _():
            for c in weights(e, ws):
                c.wait()
            nxt = bn_ref[i]

            @pl.when(nxt < n_used)
            def _():
                for c in weights(be_ref[nxt], 1 - ws):
                    c.start()
            w13b[...] = wf13[ws].astype(BF16)
            w2b[...] = wf2[ws].astype(BF16)

        nblk = jnp.minimum(i + 1, n_used - 1)
        first = jnp.bitwise_and(bs_ref[nblk], TOK_ALIGN - 1)

        def issue(s):
            for r in range(EXPERT_TILE):
                row_copy(wins[s][first + r], 1 - slot, r).start(priority=r % 2)
        by_parity(lax.rem(nblk, 2), issue)

        rows_wait(slot)
        x = _from_pitched(xbuf, slot, EXPERT_TILE).astype(BF16)
        acts = []
        half = F // 2
        for c in range(2):
            gate = jnp.dot(x, w13b[:, c * half:(c + 1) * half], preferred_element_type=F32)
            up = jnp.dot(x, w13b[:, F + c * half:F + (c + 1) * half], preferred_element_type=F32)
            acts.append((gate * _sigmoid(gate) * up).astype(BF16))
        act = jnp.concatenate(acts, axis=1)
        for c in range(XROWS // 4):
            y = jnp.dot(act, w2b[:, c * 4 * BLK:(c + 1) * 4 * BLK], preferred_element_type=F32)
            for s in range(4):
                y_ref[pl.ds(c * 4 + s, EXPERT_TILE, stride=ROW_PITCH), :] = y[:, s * BLK:(s + 1) * BLK]
        y_ref[pl.ds(XROWS, EXPERT_TILE, stride=ROW_PITCH), :] = jnp.zeros((EXPERT_TILE, BLK), F32)

        @pl.when(i == n_used - 1)
        def _():
            rows_wait(1 - slot)

    @pl.when(i >= n_used)
    def _():
        y_ref[...] = jnp.zeros_like(y_ref)


def _routed_experts(x_tiles, sorted_tok, block_e, block_start, block_ord, block_next, n_used, w13, w2, layer):
    nb = block_e.shape[0]
    _, E, D, F2 = w13.shape
    F = F2 // 2
    grid_spec = pltpu.PrefetchScalarGridSpec(
        num_scalar_prefetch=5,
        grid=(nb,),
        in_specs=[pl.BlockSpec(memory_space=pl.ANY)] * 4,
        out_specs=pl.BlockSpec((EXPERT_TILE * ROW_PITCH, BLK), lambda i, *_: (i, 0)),
        scratch_shapes=[pltpu.SMEM((TOK_WINDOW,), jnp.int32),
                        pltpu.SMEM((TOK_WINDOW,), jnp.int32),
                        pltpu.SemaphoreType.DMA((2,)),
                        pltpu.VMEM((2, EXPERT_TILE * ROW_PITCH, BLK), F32),
                        pltpu.SemaphoreType.DMA((2,)),
                        pltpu.VMEM((2, D, F2), F32),
                        pltpu.VMEM((2, F, D), F32),
                        pltpu.SemaphoreType.DMA((2, 2)),
                        pltpu.VMEM((D, F2), BF16),
                        pltpu.VMEM((F, D), BF16)],
    )
    return pl.pallas_call(
        functools.partial(_moe_kernel, layer=layer),
        grid_spec=grid_spec,
        out_shape=jax.ShapeDtypeStruct((nb * EXPERT_TILE * ROW_PITCH, BLK), F32),
        compiler_params=_cparams(("arbitrary",)),
        name="routed_experts",
    )(block_e, block_start, block_ord, block_next, n_used, sorted_tok, x_tiles, w13, w2)


def _combine_kernel(pos_ref, posn_ref, y_hbm, gate_ref, sh_ref, h_ref, g_ref, b_ref,
                    o_ref, *rest, alpha, n_steps):
    ob_ref = rest[0] if len(rest) == 3 else None
    gbuf, sem = rest[-2:]
    i = pl.program_id(0)
    slot = lax.rem(i, 2)
    n_rows = TOP_K * COMBINE_TILE

    def copy(pos, s, r):
        return pltpu.make_async_copy(y_hbm.at[pl.ds(pos * ROW_PITCH, XROWS), :],
                                     gbuf.at[s, pl.ds(r * ROW_PITCH, XROWS), :], sem.at[s])

    def gather(pref, s):
        def body(r8, c):
            for u in range(8):
                r = r8 * 8 + u
                copy(pref[0, 0, r], s, r).start(priority=u % 2)
            return c
        lax.fori_loop(0, n_rows // 8, body, 0, unroll=2)

    @pl.when(i == 0)
    def _():
        gather(pos_ref, 0)

    @pl.when(i + 1 < n_steps)
    def _():
        gather(posn_ref, 1 - slot)

    all_rows = y_hbm.at[pl.ds(0, n_rows * XROWS), :]
    pltpu.make_async_copy(all_rows, all_rows, sem.at[slot]).wait()

    gate = gate_ref[...]
    acc = alpha * h_ref[...] + sh_ref[...]
    for kk in range(TOP_K):
        rows = _from_pitched(gbuf, slot, COMBINE_TILE, first=kk * COMBINE_TILE)
        acc = acc + gate[:, kk:kk + 1] * rows
    y = _layer_norm(acc, g_ref[...], b_ref[...])
    o_ref[...] = y
    if ob_ref is not None:
        ob_ref[...] = y.astype(BF16)


def _combine_ln(pos_blocks, y, gate, shared, h, g, b, alpha, seq_blocks=None):
    T, D = h.shape
    n_steps = T // COMBINE_TILE
    n_rows = TOP_K * COMBINE_TILE
    row = lambda n: pl.BlockSpec((COMBINE_TILE, n), lambda i: (i, 0))
    vec = pl.BlockSpec((1, D), lambda i: (0, 0))
    if seq_blocks is None:
        out_specs = [row(D), row(D)]
        out_shape = [jax.ShapeDtypeStruct((T, D), F32), jax.ShapeDtypeStruct((T, D), BF16)]
    else:
        keep = seq_blocks - 1
        out_map = lambda i: ((i // seq_blocks) * keep + jnp.maximum(i % seq_blocks - 1, 0), 0)
        out_specs = [pl.BlockSpec((COMBINE_TILE, D), out_map)]
        out_shape = [jax.ShapeDtypeStruct((T // seq_blocks * keep, D), F32)]
    return pl.pallas_call(
        functools.partial(_combine_kernel, alpha=alpha, n_steps=n_steps),
        grid=(n_steps,),
        in_specs=[
            pl.BlockSpec((1, 1, n_rows), lambda i: (i, 0, 0), memory_space=pltpu.SMEM),
            pl.BlockSpec((1, 1, n_rows), lambda i: (jnp.minimum(i + 1, n_steps - 1), 0, 0),
                         memory_space=pltpu.SMEM),
            pl.BlockSpec(memory_space=pl.ANY),
            row(BLK), row(D), row(D), vec, vec],
        out_specs=out_specs,
        out_shape=out_shape,
        scratch_shapes=[pltpu.VMEM((2, n_rows * ROW_PITCH, BLK), F32), pltpu.SemaphoreType.DMA((2,))],
        compiler_params=_cparams(("arbitrary",)),
        name="combine_ln",
    )(pos_blocks, pos_blocks, y, gate, shared, h, g.reshape(1, D), b.reshape(1, D))


def _wprep_kernel(w_ref, o_ref):
    o_ref[:, 0:C_SBQ] = w_ref[0, :, 0:C_SBQ].astype(BF16)
    o_ref[:, C_SBQ:N_MAIN] = w_ref[0, :, O_SB:O_SB + N_MAIN - C_SBQ].astype(BF16)


def _prep_in_proj(w_in_all, layer):
    _, D, n_in = w_in_all.shape
    tr = 256
    return pl.pallas_call(
        _wprep_kernel,
        grid=(D // tr,),
        in_specs=[pl.BlockSpec((1, tr, n_in), lambda i: (layer, i, 0))],
        out_specs=pl.BlockSpec((tr, N_MAIN), lambda i: (i, 0)),
        out_shape=jax.ShapeDtypeStruct((D, N_MAIN), BF16),
        compiler_params=_cparams(("parallel",)),
        name="w_in_prep",
    )(w_in_all)


def _arrange_gates(w_gates, b_igate, b_fgate):
    D = w_gates.shape[0]
    wg = jnp.zeros((D, ML_GROUPS, BLK), F32)
    bias = jnp.zeros((ML_GROUPS, BLK), F32)
    w_i = w_gates[:, :ML_HEADS].reshape(D, ML_GROUPS, ML_HPS)
    w_f = w_gates[:, ML_HEADS:].reshape(D, ML_GROUPS, ML_HPS)
    wg = wg.at[:, :, :ML_HPS].set(w_i).at[:, :, ML_HPS:2 * ML_HPS].set(w_f)
    bias = bias.at[:, :ML_HPS].set(b_igate.reshape(ML_GROUPS, ML_HPS))
    bias = bias.at[:, ML_HPS:2 * ML_HPS].set(b_fgate.reshape(ML_GROUPS, ML_HPS))
    return wg.reshape(D, ML_GROUPS * BLK).astype(BF16), bias.reshape(1, ML_GROUPS * BLK)


def _dispatch_plan(idx, rank, counts, B, Lp):
    T = B * Lp
    A = T * TOP_K
    A_real = B * (Lp - PAD) * TOP_K
    nb = -(-A_real // EXPERT_TILE) + N_EXPERTS
    E = N_EXPERTS
    eids = jnp.arange(E, dtype=jnp.int32)
    padded = (counts + EXPERT_TILE - 1) // EXPERT_TILE * EXPERT_TILE
    ends = jnp.cumsum(padded)
    poff = ends - padded
    off = jnp.cumsum(counts) - counts
    idx_k = idx[:, :, :TOP_K]
    onehot = idx_k[..., None] == eids
    pos = jnp.sum(jnp.where(onehot, poff, 0), axis=-1) + rank[:, :, :TOP_K]
    real = (jnp.arange(Lp) >= PAD)[None, :, None]
    pos = jnp.where(real, pos, 0).astype(jnp.int32)
    a_bits = (A - 1).bit_length()
    assert (E + 1) << a_bits < 2 ** 31
    keys = jnp.where(real, idx_k, E).reshape(A)
    packed = lax.shift_left(keys, a_bits) + jnp.arange(A, dtype=jnp.int32)
    sorted_tok = jnp.bitwise_and(jnp.sort(packed), (1 << a_bits) - 1) // TOP_K
    n_tok = -(-A // TOK_ALIGN) * TOK_ALIGN + TOK_WINDOW
    sorted_tok = jnp.concatenate([sorted_tok, jnp.full((n_tok - A,), PAD, jnp.int32)])
    blk0 = jnp.arange(nb, dtype=jnp.int32) * EXPERT_TILE
    block_e = jnp.minimum(jnp.sum(ends[None, :] <= blk0[:, None], axis=1), E - 1).astype(jnp.int32)
    sel = block_e[:, None] == eids
    block_start = jnp.sum(jnp.where(sel, off - poff, 0), axis=1) + blk0
    block_start = jnp.clip(block_start, 0, A).astype(jnp.int32)
    n_used = (ends[-1] // EXPERT_TILE).astype(jnp.int32).reshape(1)
    changed = jnp.concatenate([jnp.zeros((1,), jnp.int32),
                               (block_e[1:] != block_e[:-1]).astype(jnp.int32)])
    block_ord = jnp.cumsum(changed).astype(jnp.int32)
    block_next = (jnp.sum(jnp.where(sel, ends, 0), axis=1) // EXPERT_TILE).astype(jnp.int32)
    return pos.reshape(T, TOP_K), sorted_tok, (block_e, block_start, block_ord, block_next, n_used)


def kernel(x, meta, ln_in_g, ln_in_b, w_in, b_igate, b_fgate, conv_qk, ml_norm_g, sb_norm_g, pool_w, pool_scale, w_out, ln1_g, ln1_b, router_w, router_bias, exp_w13, exp_w2, sh_w13, sh_w2, ln2_g, ln2_b):
    B, S, D = x.shape
    depth = w_in.shape[0]
    alpha = (2 * depth) ** 0.25
    Lp = PAD + N_META + S
    T = B * Lp
    assert Lp % BLK == 0 and T % ROW_TILE == 0 and T % COMBINE_TILE == 0
    assert D == XROWS * BLK and COMBINE_TILE == PAD + N_META

    h0 = jnp.concatenate([jnp.zeros((B, PAD, D), x.dtype),
                          jnp.broadcast_to(meta[None].astype(x.dtype), (B, N_META, D)), x], axis=1)
    h, hb = _ln_rows(h0.reshape(T, D), ln_in_g, ln_in_b)

    for l in range(depth):
        w_gate, gate_bias = _arrange_gates(w_in[l, :, C_SBQ:O_SB], b_igate[l], b_fgate[l])
        proj3 = _in_proj(hb, _prep_in_proj(w_in, l)).reshape(B, Lp, N_MAIN)
        gcol = _gates(hb.reshape(B, Lp, D), w_gate, gate_bias)
        grow = jnp.transpose(gcol.reshape(B, Lp, ML_GROUPS, BLK)[..., :8], (0, 2, 3, 1))
        qk3 = _conv_silu(proj3, conv_qk[l])
        a_ml = _mlstm(qk3, proj3, gcol, grow, ml_norm_g[l])
        a_sb = _stick_breaking(proj3, sb_norm_g[l])
        a_pool = _pool(proj3, pool_w[l], pool_scale[l])
        h1, h1_tiles = _outproj_ln(a_ml.reshape(T, D_ML), a_sb.reshape(T, D_SB), a_pool.reshape(T, D_POOL),
                                   w_out[l].astype(BF16), h, ln1_g[l], ln1_b[l], alpha)

        idx, gate, rank, counts = _router(h1.reshape(B, Lp, D), router_w[l], router_bias[l])
        pos, sorted_tok, tables = _dispatch_plan(idx, rank, counts[0], B, Lp)
        y = _routed_experts(h1_tiles, sorted_tok, *tables, exp_w13, exp_w2, l)
        shared = _shared_expert(h1, sh_w13[l].astype(BF16), sh_w2[l].astype(BF16))
        n_steps = T // COMBINE_TILE
        pos_blocks = jnp.transpose(pos.reshape(n_steps, COMBINE_TILE, TOP_K), (0, 2, 1))
        pos_blocks = pos_blocks.reshape(n_steps, 1, TOP_K * COMBINE_TILE)
        args = (pos_blocks, y, gate.reshape(T, BLK), shared, h1, ln2_g[l], ln2_b[l], alpha)
        if l + 1 < depth:
            h, hb = _combine_ln(*args)
        else:
            (out,) = _combine_ln(*args, seq_blocks=Lp // COMBINE_TILE)

    return out.reshape(B, S, D)
```

```python
import functools

import jax
import jax.numpy as jnp
from jax import lax
from jax.experimental import pallas as pl
from jax.experimental.pallas import tpu as pltpu

N_META = 16
HEAD_DIM = 128
ML_HEADS = 6
SB_HEADS = 6
POOL_WINDOWS = (2, 4, 8, 16)
POOL_GROUPS = 4
D_ML = ML_HEADS * HEAD_DIM
D_SB = SB_HEADS * HEAD_DIM
D_POOL = POOL_GROUPS * HEAD_DIM
CONV_W = 4
N_EXPERTS = 64
TOP_K = 8
ROUTED_SCALE = 2.5
LN_EPS = 1e-5
RMS_EPS = 1e-6

BLK = 128
PAD = (-N_META) % BLK
ML_HPS = 3
ML_GROUPS = ML_HEADS // ML_HPS
SB_KEY_BLOCKS = 8
SB_HPS = 3
SB_W = SB_HPS * HEAD_DIM
ROW_TILE = 512
EXPERT_TILE = 256
COMBINE_TILE = 128
TOK_ALIGN_LOG2 = 10
TOK_ALIGN = 1 << TOK_ALIGN_LOG2
TOK_WINDOW = 2 * TOK_ALIGN
NEG = -1e30
VMEM_LIMIT = 56 * 1024 * 1024

C_QK = 0
C_V = 2 * D_ML
C_O = C_V + D_ML
C_SBQ = C_O + D_ML
C_SBK = C_SBQ + D_SB
C_SBV = C_SBK + D_SB
C_POOL = C_SBV + D_SB
N_MAIN = C_POOL + D_POOL
O_SB = C_SBQ + 2 * ML_HEADS
XROWS = 16
ROW_PITCH = XROWS + 1

F32 = jnp.float32
BF16 = jnp.bfloat16


def _cparams(sem, vmem=VMEM_LIMIT):
    return pltpu.CompilerParams(dimension_semantics=sem, vmem_limit_bytes=vmem)


def _layer_norm(x, g, b):
    mu = jnp.mean(x, axis=-1, keepdims=True)
    xc = x - mu
    var = jnp.mean(xc * xc, axis=-1, keepdims=True)
    return xc * lax.rsqrt(var + LN_EPS) * g + b


def _sigmoid(x):
    return 1.0 / (1.0 + jnp.exp(-x))


def _softplus(x):
    return jnp.maximum(x, 0.0) + jnp.log(1.0 + jnp.exp(-jnp.abs(x)))


def _split_dot(a, b, split_lhs):
    x = a if split_lhs else b
    out = None
    for _ in range(3):
        part = x.astype(BF16)
        x = x - part.astype(F32)
        term = (jnp.dot(part, b, preferred_element_type=F32) if split_lhs
                else jnp.dot(a, part, preferred_element_type=F32))
        out = term if out is None else out + term
    return out


def _ln_kernel(x_ref, g_ref, b_ref, o_ref, ob_ref):
    y = _layer_norm(x_ref[...], g_ref[...], b_ref[...])
    o_ref[...] = y
    ob_ref[...] = y.astype(BF16)


def _ln_rows(x, g, b):
    T, D = x.shape
    row = pl.BlockSpec((ROW_TILE, D), lambda i: (i, 0))
    vec = pl.BlockSpec((1, D), lambda i: (0, 0))
    return pl.pallas_call(
        _ln_kernel,
        grid=(T // ROW_TILE,),
        in_specs=[row, vec, vec],
        out_specs=[row, row],
        out_shape=[jax.ShapeDtypeStruct((T, D), F32), jax.ShapeDtypeStruct((T, D), BF16)],
        compiler_params=_cparams(("parallel",)),
        name="ln_in",
    )(x, g.reshape(1, D), b.reshape(1, D))


def _mm_kernel(a_ref, w_ref, o_ref):
    o_ref[...] = jnp.dot(a_ref[...], w_ref[...], preferred_element_type=F32).astype(o_ref.dtype)


def _in_proj(a, w):
    M, K = a.shape
    N = w.shape[1]
    tn = N // 2
    return pl.pallas_call(
        _mm_kernel,
        grid=(N // tn, M // ROW_TILE),
        in_specs=[pl.BlockSpec((ROW_TILE, K), lambda j, i: (i, 0)),
                  pl.BlockSpec((K, tn), lambda j, i: (0, j))],
        out_specs=pl.BlockSpec((ROW_TILE, tn), lambda j, i: (i, j)),
        out_shape=jax.ShapeDtypeStruct((M, N), BF16),
        compiler_params=_cparams(("parallel", "parallel")),
        name="in_proj",
    )(a, w)


def _gates_kernel(hb_ref, wg_ref, bias_ref, o_ref, *, tm):
    j = pl.program_id(1)
    g = jnp.dot(hb_ref[0], wg_ref[...], preferred_element_type=F32) + bias_ref[...]
    col = jnp.bitwise_and(lax.broadcasted_iota(jnp.int32, g.shape, 1), BLK - 1)
    row = lax.broadcasted_iota(jnp.int32, g.shape, 0) + j * tm
    is_f = jnp.logical_and(col >= ML_HPS, col < 2 * ML_HPS)
    val = jnp.where(is_f, -_softplus(-g), g)
    pad_val = jnp.where(is_f, 0.0, NEG)
    o_ref[0] = jnp.where(row < PAD, pad_val, val)


def _gates(hb3, wg, bias):
    B, Lp, D = hb3.shape
    tm = Lp // 4
    Ng = wg.shape[1]
    return pl.pallas_call(
        functools.partial(_gates_kernel, tm=tm),
        grid=(B, Lp // tm),
        in_specs=[pl.BlockSpec((1, tm, D), lambda b, j: (b, j, 0)),
                  pl.BlockSpec((D, Ng), lambda b, j: (0, 0)),
                  pl.BlockSpec((1, Ng), lambda b, j: (0, 0))],
        out_specs=pl.BlockSpec((1, tm, Ng), lambda b, j: (b, j, 0)),
        out_shape=jax.ShapeDtypeStruct((B, Lp, Ng), F32),
        compiler_params=_cparams(("parallel", "parallel")),
        name="ml_gates",
    )(hb3, wg, bias)


def _conv_kernel(x_ref, w_ref, o_ref):
    c = pl.program_id(1)
    x = x_ref[0].astype(F32)
    row = lax.broadcasted_iota(jnp.int32, x.shape, 0)
    x = jnp.where(row < PAD, 0.0, x)
    w = w_ref[...]
    y = x * w[CONV_W - 1:CONV_W]
    for s in range(1, CONV_W):
        y = y + pltpu.roll(x, s, axis=0) * w[CONV_W - 1 - s:CONV_W - s]
    y = y * _sigmoid(y)
    scale = jnp.where(c < ML_HEADS, HEAD_DIM ** -0.5, 1.0).astype(F32)
    o_ref[0] = (y * scale).astype(BF16)


def _conv_silu(proj3, conv_w):
    B, Lp, _ = proj3.shape
    nc = 2 * D_ML // BLK
    return pl.pallas_call(
        _conv_kernel,
        grid=(B, nc),
        in_specs=[pl.BlockSpec((1, Lp, BLK), lambda b, c: (b, 0, c)),
                  pl.BlockSpec((CONV_W, BLK), lambda b, c: (0, c))],
        out_specs=pl.BlockSpec((1, Lp, BLK), lambda b, c: (b, 0, c)),
        out_shape=jax.ShapeDtypeStruct((B, Lp, 2 * D_ML), BF16),
        compiler_params=_cparams(("parallel", "parallel")),
        name="conv_silu",
    )(proj3, conv_w)


def _mlstm_kernel(q_ref, k_ref, v_ref, o_ref, gc_ref, gr_ref, ng_ref, out_ref, st_ref, m_ref):
    n_chunks = q_ref.shape[1] // BLK
    st_ref[...] = jnp.zeros_like(st_ref)
    m_ref[...] = jnp.zeros_like(m_ref)

    ri = lax.broadcasted_iota(jnp.int32, (BLK, BLK), 0)
    ci = lax.broadcasted_iota(jnp.int32, (BLK, BLK), 1)
    causal = ci <= ri
    lower_incl = jnp.where(causal, 1.0, 0.0).astype(BF16)
    upper_incl = jnp.where(ri <= ci, 1.0, 0.0).astype(BF16)
    ones_col = jnp.where(ci == 0, 1.0, 0.0).astype(BF16)

    def chunk(c, carry):
        r0 = pl.multiple_of(c * BLK, BLK)
        gcol = gc_ref[0, pl.ds(r0, BLK), :]
        grow = gr_ref[0, 0, :, pl.ds(r0, BLK)]
        bcol = _split_dot(lower_incl, gcol, split_lhs=False)
        brow = _split_dot(grow, upper_incl, split_lhs=True)
        for hh in range(ML_HPS):
            cs = slice(hh * HEAD_DIM, (hh + 1) * HEAD_DIM)
            q = q_ref[0, pl.ds(r0, BLK), cs]
            k = k_ref[0, pl.ds(r0, BLK), cs]
            v = v_ref[0, pl.ds(r0, BLK), cs]
            li_c = gcol[:, hh:hh + 1]
            b_c = bcol[:, ML_HPS + hh:ML_HPS + hh + 1]
            li_r = grow[hh:hh + 1, :]
            b_r = brow[ML_HPS + hh:ML_HPS + hh + 1, :]
            m = m_ref[hh, 0:1, 0:1]
            st = st_ref[hh]

            inter = b_c + m
            dmat = jnp.where(causal, b_c - b_r + li_r, NEG)
            m_t = jnp.maximum(inter, jnp.max(dmat, axis=-1, keepdims=True))
            w_inter = jnp.exp(inter - m_t)
            s_qk = lax.dot_general(q, k, (((1,), (1,)), ((), ())), preferred_element_type=F32)
            ws = jnp.exp(dmat - m_t) * s_qk
            v_aug = jnp.concatenate([v, ones_col], axis=1)
            nd = (w_inter * jnp.dot(q, st.astype(BF16), preferred_element_type=F32)
                  + jnp.dot(ws.astype(BF16), v_aug, preferred_element_type=F32))
            num = nd[:, :HEAD_DIM]
            den = nd[:, HEAD_DIM:HEAD_DIM + 1]
            h = num / jnp.maximum(jnp.abs(den), jnp.exp(-m_t))

            og = o_ref[0, pl.ds(r0, BLK), cs].astype(F32)
            hn = h * lax.rsqrt(jnp.mean(h * h, axis=-1, keepdims=True) + RMS_EPS)
            hn = hn * ng_ref[:, cs] * _sigmoid(og)
            out_ref[0, pl.ds(r0, BLK), cs] = hn.astype(BF16)

            b_last = b_r[:, BLK - 1:BLK]
            g_r = b_last - b_r + li_r
            m_new = jnp.maximum(b_last + m, jnp.max(g_r, axis=-1, keepdims=True))
            wg = jnp.exp(b_last - b_c + li_c - m_new)
            decay = jnp.exp(b_last + m - m_new)
            wv = (wg * v_aug.astype(F32)).astype(BF16)
            kv = lax.dot_general(k, wv, (((0,), (0,)), ((), ())), preferred_element_type=F32)
            st_ref[hh] = decay * st + kv
            m_ref[hh] = jnp.broadcast_to(m_new, m_ref.shape[1:])
        return carry

    lax.fori_loop(0, n_chunks, chunk, 0)


def _mlstm(qk3, proj3, gcol, grow, norm_g):
    B, Lp, _ = qk3.shape
    W = ML_HPS * HEAD_DIM
    nq = D_ML // W
    seq = lambda off: pl.BlockSpec((1, Lp, W), lambda b, g: (b, 0, off + g))
    return pl.pallas_call(
        _mlstm_kernel,
        grid=(B, ML_GROUPS),
        in_specs=[seq(0), seq(nq),
                  pl.BlockSpec((1, Lp, W), lambda b, g: (b, 0, C_V // W + g)),
                  pl.BlockSpec((1, Lp, W), lambda b, g: (b, 0, C_O // W + g)),
                  pl.BlockSpec((1, Lp, BLK), lambda b, g: (b, 0, g)),
                  pl.BlockSpec((1, 1, 8, Lp), lambda b, g: (b, g, 0, 0)),
                  pl.BlockSpec((1, W), lambda b, g: (0, g))],
        out_specs=pl.BlockSpec((1, Lp, W), lambda b, g: (b, 0, g)),
        out_shape=jax.ShapeDtypeStruct((B, Lp, D_ML), BF16),
        scratch_shapes=[pltpu.VMEM((ML_HPS, HEAD_DIM, 2 * HEAD_DIM), F32),
                        pltpu.VMEM((ML_HPS, 8, BLK), F32)],
        compiler_params=_cparams(("parallel", "parallel")),
        name="mlstm",
    )(qk3, qk3, proj3, proj3, gcol, grow, norm_g.reshape(1, D_ML))


def _sb_kernel(q_ref, k_ref, v_ref, g_ref, o_ref):
    i = pl.program_id(2)
    r1 = lax.broadcasted_iota(jnp.int32, (BLK, BLK), 0)
    c1 = lax.broadcasted_iota(jnp.int32, (BLK, BLK), 1)
    upper_strict = jnp.where(r1 > c1, 1.0, 0.0).astype(BF16)
    scale = HEAD_DIM ** -0.5
    heads = [slice(hh * HEAD_DIM, (hh + 1) * HEAD_DIM) for hh in range(SB_HPS)]
    qs = [(q_ref[0, :, hs].astype(F32) * scale).astype(BF16) for hs in heads]

    def one_head(q, k, v, carry, acc, valid, n_blocks):
        z = lax.dot_general(q, k, (((1,), (1,)), ((), ())), preferred_element_type=F32)
        sp = _softplus(z)
        lk = jnp.where(valid, -sp, 0.0)
        hi = lk.astype(BF16)
        lo = (lk - hi.astype(F32)).astype(BF16)
        pieces = [None] * n_blocks
        off = carry
        for c in reversed(range(n_blocks)):
            sl = slice(c * BLK, (c + 1) * BLK)
            local = (jnp.dot(hi[:, sl], upper_strict, preferred_element_type=F32)
                     + jnp.dot(lo[:, sl], upper_strict, preferred_element_type=F32))
            pieces[c] = local + off
            off = off + jnp.sum(lk[:, sl], axis=-1, keepdims=True)
        after = jnp.concatenate(pieces, axis=1)
        a = jnp.where(valid, jnp.exp(z - sp + after), 0.0)
        return off, acc + jnp.dot(a.astype(BF16), v, preferred_element_type=F32)

    def make_step(n_blocks):
        W = n_blocks * BLK
        rowp = lax.broadcasted_iota(jnp.int32, (BLK, W), 0) + i * BLK
        ci = lax.broadcasted_iota(jnp.int32, (BLK, W), 1)

        def step(end, state):
            s0 = pl.multiple_of(jnp.maximum(end - W, 0), BLK)
            colp = ci + s0
            valid = jnp.logical_and(colp < jnp.minimum(rowp, end), colp >= PAD)
            return tuple(one_head(qs[hh], k_ref[0, pl.ds(s0, W), hs], v_ref[0, pl.ds(s0, W), hs],
                                  carry, acc, valid, n_blocks)
                         for hh, (hs, (carry, acc)) in enumerate(zip(heads, state)))
        return step

    wide, narrow = make_step(SB_KEY_BLOCKS), make_step(SB_KEY_BLOCKS // 2)
    init = tuple((jnp.zeros((BLK, 1), F32), jnp.zeros((BLK, HEAD_DIM), F32)) for _ in heads)
    n_wide = (i + 1) // SB_KEY_BLOCKS
    rem = i + 1 - n_wide * SB_KEY_BLOCKS
    state = lax.fori_loop(0, n_wide, lambda t, s: wide((i + 1 - t * SB_KEY_BLOCKS) * BLK, s), init)
    final = lax.cond(
        rem == 0, lambda s: s,
        lambda s: lax.cond(rem <= SB_KEY_BLOCKS // 2,
                           lambda s2: narrow(rem * BLK, s2), lambda s2: wide(rem * BLK, s2), s),
        state)
    for hs, (_, acc) in zip(heads, final):
        hn = acc * lax.rsqrt(jnp.mean(acc * acc, axis=-1, keepdims=True) + RMS_EPS) * g_ref[:, hs]
        o_ref[0, :, hs] = hn.astype(BF16)


def _stick_breaking(proj3, norm_g):
    B, Lp, _ = proj3.shape
    nq = Lp // BLK
    return pl.pallas_call(
        _sb_kernel,
        grid=(B, SB_HEADS // SB_HPS, nq),
        in_specs=[pl.BlockSpec((1, BLK, SB_W), lambda b, h, i: (b, i, C_SBQ // SB_W + h)),
                  pl.BlockSpec((1, Lp, SB_W), lambda b, h, i: (b, 0, C_SBK // SB_W + h)),
                  pl.BlockSpec((1, Lp, SB_W), lambda b, h, i: (b, 0, C_SBV // SB_W + h)),
                  pl.BlockSpec((1, SB_W), lambda b, h, i: (0, h))],
        out_specs=pl.BlockSpec((1, BLK, SB_W), lambda b, h, i: (b, i, h)),
        out_shape=jax.ShapeDtypeStruct((B, Lp, D_SB), BF16),
        compiler_params=_cparams(("parallel", "parallel", "parallel")),
        name="stick_breaking",
    )(proj3, proj3, proj3, norm_g.reshape(1, D_SB))


def _pool_kernel(x_ref, w_ref, s_ref, o_ref):
    g = pl.program_id(1)
    x = x_ref[0].astype(F32)
    row = lax.broadcasted_iota(jnp.int32, x.shape, 0)
    x = jnp.where(row < PAD, 0.0, x)
    pos1 = (row - PAD + 1).astype(F32)
    sums = x
    mean = jnp.zeros_like(x)
    shift = 1
    for gi, win in enumerate(POOL_WINDOWS):
        while shift < win:
            sums = sums + pltpu.roll(sums, shift, axis=0)
            shift *= 2
        cnt = jnp.maximum(jnp.minimum(pos1, float(win)), 1.0)
        mean = jnp.where(g == gi, sums / cnt, mean)
    p = (mean - x).astype(BF16)
    y = jnp.dot(p, w_ref[0].astype(BF16), preferred_element_type=F32) * s_ref[...]
    o_ref[0] = y.astype(BF16)


def _pool(proj3, pool_w, pool_scale):
    B, Lp, _ = proj3.shape
    return pl.pallas_call(
        _pool_kernel,
        grid=(B, POOL_GROUPS),
        in_specs=[pl.BlockSpec((1, Lp, HEAD_DIM), lambda b, g: (b, 0, C_POOL // HEAD_DIM + g)),
                  pl.BlockSpec((1, HEAD_DIM, HEAD_DIM), lambda b, g: (g, 0, 0)),
                  pl.BlockSpec((1, HEAD_DIM), lambda b, g: (0, g))],
        out_specs=pl.BlockSpec((1, Lp, HEAD_DIM), lambda b, g: (b, 0, g)),
        out_shape=jax.ShapeDtypeStruct((B, Lp, D_POOL), BF16),
        compiler_params=_cparams(("parallel", "parallel")),
        name="pool_mixer",
    )(proj3, pool_w, pool_scale.reshape(1, D_POOL))


def _to_pitched(ref, y):
    n = y.shape[0]
    for s in range(XROWS):
        ref[pl.ds(s, n, stride=ROW_PITCH), :] = y[:, s * BLK:(s + 1) * BLK]
    ref[pl.ds(XROWS, n, stride=ROW_PITCH), :] = jnp.zeros((n, BLK), y.dtype)


def _from_pitched(ref, slot, n, first=0):
    return jnp.concatenate(
        [ref[slot, pl.ds(first * ROW_PITCH + s, n, stride=ROW_PITCH), :] for s in range(XROWS)], axis=1)


def _outproj_kernel(a1_ref, a2_ref, a3_ref, w_ref, h_ref, g_ref, b_ref, o_ref, ot_ref, *, alpha):
    acc = jnp.dot(a1_ref[...], w_ref[0:D_ML, :], preferred_element_type=F32)
    acc = acc + jnp.dot(a2_ref[...], w_ref[D_ML:D_ML + D_SB, :], preferred_element_type=F32)
    acc = acc + jnp.dot(a3_ref[...], w_ref[D_ML + D_SB:, :], preferred_element_type=F32)
    y = _layer_norm(alpha * h_ref[...] + acc, g_ref[...], b_ref[...])
    o_ref[...] = y
    _to_pitched(ot_ref, y)


def _outproj_ln(a_ml, a_sb, a_pool, w, h, g, b, alpha):
    T, D = h.shape
    row = lambda n: pl.BlockSpec((ROW_TILE, n), lambda i: (i, 0))
    vec = pl.BlockSpec((1, D), lambda i: (0, 0))
    return pl.pallas_call(
        functools.partial(_outproj_kernel, alpha=alpha),
        grid=(T // ROW_TILE,),
        in_specs=[row(D_ML), row(D_SB), row(D_POOL),
                  pl.BlockSpec(w.shape, lambda i: (0, 0)), row(D), vec, vec],
        out_specs=[row(D), pl.BlockSpec((ROW_TILE * ROW_PITCH, BLK), lambda i: (i, 0))],
        out_shape=[jax.ShapeDtypeStruct((T, D), F32), jax.ShapeDtypeStruct((T * ROW_PITCH, BLK), F32)],
        compiler_params=_cparams(("parallel",)),
        name="out_proj_ln",
    )(a_ml, a_sb, a_pool, w, h, g.reshape(1, D), b.reshape(1, D))


def _router_kernel(h_ref, rw_ref, rb_ref, idx_ref, gate_ref, rank_ref, cnt_ref, run_ref, *, tm):
    first = jnp.logical_and(pl.program_id(0) == 0, pl.program_id(1) == 0)

    @pl.when(first)
    def _():
        run_ref[...] = jnp.zeros_like(run_ref)

    x = h_ref[0]
    logits = jnp.dot(x, rw_ref[...], preferred_element_type=F32, precision=lax.Precision.HIGHEST)
    scores = _sigmoid(logits)
    work = scores + rb_ref[...]
    E = scores.shape[1]
    lane = lax.broadcasted_iota(jnp.int32, (tm, E), 1).astype(F32)
    row = lax.broadcasted_iota(jnp.int32, (tm, 1), 0) + pl.program_id(1) * tm
    real = (row >= PAD).astype(F32)

    picks = []
    gates = []
    member = jnp.zeros((tm, E), F32)
    for _ in range(TOP_K):
        mx = jnp.max(work, axis=-1, keepdims=True)
        pick = jnp.min(jnp.where(work == mx, lane, float(E)), axis=-1, keepdims=True)
        onehot = lane == pick
        gates.append(jnp.sum(jnp.where(onehot, scores, 0.0), axis=-1, keepdims=True))
        picks.append(pick)
        member = member + jnp.where(onehot, real, 0.0)
        work = jnp.where(onehot, -jnp.inf, work)
    gsum = gates[0]
    for gk in gates[1:]:
        gsum = gsum + gk

    ri = lax.broadcasted_iota(jnp.int32, (tm, tm), 0)
    ci = lax.broadcasted_iota(jnp.int32, (tm, tm), 1)
    lower_strict = jnp.where(ci < ri, 1.0, 0.0).astype(BF16)
    before = jnp.dot(lower_strict, member.astype(BF16), preferred_element_type=F32) + run_ref[...]

    lane_o = lax.broadcasted_iota(jnp.int32, (tm, BLK), 1)
    idx_o = jnp.zeros((tm, BLK), F32)
    gate_o = jnp.zeros((tm, BLK), F32)
    rank_o = jnp.zeros((tm, BLK), F32)
    for kk in range(TOP_K):
        onehot = lane == picks[kk]
        rk = jnp.sum(jnp.where(onehot, before, 0.0), axis=-1, keepdims=True)
        sel = lane_o == kk
        idx_o = jnp.where(sel, picks[kk], idx_o)
        gate_o = jnp.where(sel, gates[kk] / gsum * ROUTED_SCALE, gate_o)
        rank_o = jnp.where(sel, rk, rank_o)
    idx_ref[0] = idx_o.astype(jnp.int32)
    gate_ref[0] = gate_o
    rank_ref[0] = rank_o.astype(jnp.int32)
    run_ref[...] = run_ref[...] + jnp.sum(member, axis=0, keepdims=True)
    cnt_ref[...] = run_ref[...].astype(jnp.int32)


def _router(h3, router_w, router_bias):
    B, Lp, D = h3.shape
    E = router_w.shape[1]
    tm = Lp // 4
    out = lambda: pl.BlockSpec((1, tm, BLK), lambda b, j: (b, j, 0))
    return pl.pallas_call(
        functools.partial(_router_kernel, tm=tm),
        grid=(B, Lp // tm),
        in_specs=[pl.BlockSpec((1, tm, D), lambda b, j: (b, j, 0)),
                  pl.BlockSpec((D, E), lambda b, j: (0, 0)),
                  pl.BlockSpec((1, E), lambda b, j: (0, 0))],
        out_specs=[out(), out(), out(), pl.BlockSpec((1, E), lambda b, j: (0, 0))],
        out_shape=[jax.ShapeDtypeStruct((B, Lp, BLK), jnp.int32),
                   jax.ShapeDtypeStruct((B, Lp, BLK), F32),
                   jax.ShapeDtypeStruct((B, Lp, BLK), jnp.int32),
                   jax.ShapeDtypeStruct((1, E), jnp.int32)],
        scratch_shapes=[pltpu.VMEM((1, E), F32)],
        compiler_params=_cparams(("arbitrary", "arbitrary")),
        name="router",
    )(h3, router_w, router_bias.reshape(1, E))


def _swiglu(x, w13, w2):
    h1 = jnp.dot(x, w13, preferred_element_type=F32)
    f = h1.shape[1] // 2
    gate, up = h1[:, :f], h1[:, f:]
    act = (gate * _sigmoid(gate) * up).astype(BF16)
    return jnp.dot(act, w2, preferred_element_type=F32)


def _shared_kernel(x_ref, w13_ref, w2_ref, o_ref):
    o_ref[...] = _swiglu(x_ref[...].astype(BF16), w13_ref[...], w2_ref[...])


def _shared_expert(h, w13, w2):
    T, D = h.shape
    row = pl.BlockSpec((ROW_TILE, D), lambda i: (i, 0))
    return pl.pallas_call(
        _shared_kernel,
        grid=(T // ROW_TILE,),
        in_specs=[row, pl.BlockSpec(w13.shape, lambda i: (0, 0)),
                  pl.BlockSpec(w2.shape, lambda i: (0, 0))],
        out_specs=row,
        out_shape=jax.ShapeDtypeStruct((T, D), F32),
        compiler_params=_cparams(("parallel",)),
        name="shared_expert",
    )(h, w13, w2)


def _moe_kernel(be_ref, bs_ref, bo_ref, bn_ref, nb_ref, tok_hbm, x_hbm, w13_hbm, w2_hbm, y_ref,
                win0, win1, wsem, xbuf, xsem, wf13, wf2, fsem, w13b, w2b, *, layer):
    i = pl.program_id(0)
    n_used = nb_ref[0]
    slot = lax.rem(i, 2)
    F = w2b.shape[0]
    wins = (win0, win1)

    def by_parity(p, fn):
        for s in range(2):
            pl.when(p == s)(functools.partial(fn, s))

    def window(blk, s):
        base = pl.multiple_of(lax.shift_left(lax.shift_right_logical(bs_ref[blk], TOK_ALIGN_LOG2),
                                             TOK_ALIGN_LOG2), TOK_ALIGN)
        return pltpu.make_async_copy(tok_hbm.at[pl.ds(base, TOK_WINDOW)], wins[s], wsem.at[s])

    def weights(e, s):
        return (pltpu.make_async_copy(w13_hbm.at[layer, e], wf13.at[s], fsem.at[0, s]),
                pltpu.make_async_copy(w2_hbm.at[layer, e], wf2.at[s], fsem.at[1, s]))

    def row_copy(tok, s, r):
        return pltpu.make_async_copy(x_hbm.at[pl.ds(tok * ROW_PITCH, XROWS), :],
                                     xbuf.at[s, pl.ds(r * ROW_PITCH, XROWS), :], xsem.at[s])

    def rows_wait(s):
        rows = x_hbm.at[pl.ds(0, EXPERT_TILE * XROWS), :]
        pltpu.make_async_copy(rows, rows, xsem.at[s]).wait()

    @pl.when(jnp.logical_and(i == 0, n_used > 0))
    def _():
        for c in weights(be_ref[0], 0):
            c.start()
        window(0, 0).start()
        window(0, 0).wait()
        first0 = jnp.bitwise_and(bs_ref[0], TOK_ALIGN - 1)

        def body(r, c):
            row_copy(win0[first0 + r], 0, r).start()
            return c
        lax.fori_loop(0, EXPERT_TILE, body, 0)

        @pl.when(n_used > 1)
        def _():
            window(1, 1).start()

    @pl.when(i + 1 < n_used)
    def _():
        by_parity(1 - slot, lambda s: window(i + 1, s).wait())

    @pl.when(i + 2 < n_used)
    def _():
        by_parity(slot, lambda s: window(i + 2, s).start())

    @pl.when(i < n_used)
    def _():
        e = be_ref[i]
        ws = lax.rem(bo_ref[i], 2)

        @pl.when(jnp.logical_or(i == 0, e != be_ref[jnp.maximum(i - 1, 0)]))
        def _():
            for c in weights(e, ws):
                c.wait()
            nxt = bn_ref[i]

            @pl.when(nxt < n_used)
            def _():
                for c in weights(be_ref[nxt], 1 - ws):
                    c.start()
            w13b[...] = wf13[ws].astype(BF16)
            w2b[...] = wf2[ws].astype(BF16)

        nblk = jnp.minimum(i + 1, n_used - 1)
        first = jnp.bitwise_and(bs_ref[nblk], TOK_ALIGN - 1)

        def issue(s):
            for r in range(EXPERT_TILE):
                row_copy(wins[s][first + r], 1 - slot, r).start(priority=r % 2)
        by_parity(lax.rem(nblk, 2), issue)

        rows_wait(slot)
        x = _from_pitched(xbuf, slot, EXPERT_TILE).astype(BF16)
        acts = []
        half = F // 2
        for c in range(2):
            gate = jnp.dot(x, w13b[:, c * half:(c + 1) * half], preferred_element_type=F32)
            up = jnp.dot(x, w13b[:, F + c * half:F + (c + 1) * half], preferred_element_type=F32)
            acts.append((gate * _sigmoid(gate) * up).astype(BF16))
        act = jnp.concatenate(acts, axis=1)
        for c in range(XROWS // 4):
            y = jnp.dot(act, w2b[:, c * 4 * BLK:(c + 1) * 4 * BLK], preferred_element_type=F32)
            for s in range(4):
                y_ref[pl.ds(c * 4 + s, EXPERT_TILE, stride=ROW_PITCH), :] = y[:, s * BLK:(s + 1) * BLK]
        y_ref[pl.ds(XROWS, EXPERT_TILE, stride=ROW_PITCH), :] = jnp.zeros((EXPERT_TILE, BLK), F32)

        @pl.when(i == n_used - 1)
        def _():
            rows_wait(1 - slot)

    @pl.when(i >= n_used)
    def _():
        y_ref[...] = jnp.zeros_like(y_ref)


def _routed_experts(x_tiles, sorted_tok, block_e, block_start, block_ord, block_next, n_used, w13, w2, layer):
    nb = block_e.shape[0]
    _, E, D, F2 = w13.shape
    F = F2 // 2
    grid_spec = pltpu.PrefetchScalarGridSpec(
        num_scalar_prefetch=5,
        grid=(nb,),
        in_specs=[pl.BlockSpec(memory_space=pl.ANY)] * 4,
        out_specs=pl.BlockSpec((EXPERT_TILE * ROW_PITCH, BLK), lambda i, *_: (i, 0)),
        scratch_shapes=[pltpu.SMEM((TOK_WINDOW,), jnp.int32),
                        pltpu.SMEM((TOK_WINDOW,), jnp.int32),
                        pltpu.SemaphoreType.DMA((2,)),
                        pltpu.VMEM((2, EXPERT_TILE * ROW_PITCH, BLK), F32),
                        pltpu.SemaphoreType.DMA((2,)),
                        pltpu.VMEM((2, D, F2), F32),
                        pltpu.VMEM((2, F, D), F32),
                        pltpu.SemaphoreType.DMA((2, 2)),
                        pltpu.VMEM((D, F2), BF16),
                        pltpu.VMEM((F, D), BF16)],
    )
    return pl.pallas_call(
        functools.partial(_moe_kernel, layer=layer),
        grid_spec=grid_spec,
        out_shape=jax.ShapeDtypeStruct((nb * EXPERT_TILE * ROW_PITCH, BLK), F32),
        compiler_params=_cparams(("arbitrary",)),
        name="routed_experts",
    )(block_e, block_start, block_ord, block_next, n_used, sorted_tok, x_tiles, w13, w2)


def _combine_kernel(pos_ref, posn_ref, y_hbm, gate_ref, w13_ref, w2_ref, h_ref, g_ref, b_ref,
                    o_ref, *rest, alpha, n_steps):
    ob_ref = rest[0] if len(rest) == 3 else None
    gbuf, sem = rest[-2:]
    i = pl.program_id(0)
    slot = lax.rem(i, 2)
    n_rows = TOP_K * COMBINE_TILE

    def copy(pos, s, r):
        return pltpu.make_async_copy(y_hbm.at[pl.ds(pos * ROW_PITCH, XROWS), :],
                                     gbuf.at[s, pl.ds(r * ROW_PITCH, XROWS), :], sem.at[s])

    def gather(pref, s):
        def body(r8, c):
            for u in range(8):
                r = r8 * 8 + u
                copy(pref[0, 0, r], s, r).start(priority=u % 2)
            return c
        lax.fori_loop(0, n_rows // 8, body, 0, unroll=2)

    @pl.when(i == 0)
    def _():
        gather(pos_ref, 0)

    @pl.when(i + 1 < n_steps)
    def _():
        gather(posn_ref, 1 - slot)

    h_tile = h_ref[...]
    acc = alpha * h_tile + _swiglu(h_tile.astype(BF16), w13_ref[...], w2_ref[...])

    all_rows = y_hbm.at[pl.ds(0, n_rows * XROWS), :]
    pltpu.make_async_copy(all_rows, all_rows, sem.at[slot]).wait()

    gate = gate_ref[...]
    for kk in range(TOP_K):
        rows = _from_pitched(gbuf, slot, COMBINE_TILE, first=kk * COMBINE_TILE)
        acc = acc + gate[:, kk:kk + 1] * rows
    y = _layer_norm(acc, g_ref[...], b_ref[...])
    o_ref[...] = y
    if ob_ref is not None:
        ob_ref[...] = y.astype(BF16)


def _combine_ln(pos_blocks, y, gate, w13, w2, h, g, b, alpha, seq_blocks=None):
    T, D = h.shape
    n_steps = T // COMBINE_TILE
    n_rows = TOP_K * COMBINE_TILE
    row = lambda n: pl.BlockSpec((COMBINE_TILE, n), lambda i: (i, 0))
    vec = pl.BlockSpec((1, D), lambda i: (0, 0))
    if seq_blocks is None:
        out_specs = [row(D), row(D)]
        out_shape = [jax.ShapeDtypeStruct((T, D), F32), jax.ShapeDtypeStruct((T, D), BF16)]
    else:
        keep = seq_blocks - 1
        out_map = lambda i: ((i // seq_blocks) * keep + jnp.maximum(i % seq_blocks - 1, 0), 0)
        out_specs = [pl.BlockSpec((COMBINE_TILE, D), out_map)]
        out_shape = [jax.ShapeDtypeStruct((T // seq_blocks * keep, D), F32)]
    return pl.pallas_call(
        functools.partial(_combine_kernel, alpha=alpha, n_steps=n_steps),
        grid=(n_steps,),
        in_specs=[
            pl.BlockSpec((1, 1, n_rows), lambda i: (i, 0, 0), memory_space=pltpu.SMEM),
            pl.BlockSpec((1, 1, n_rows), lambda i: (jnp.minimum(i + 1, n_steps - 1), 0, 0),
                         memory_space=pltpu.SMEM),
            pl.BlockSpec(memory_space=pl.ANY),
            row(BLK), pl.BlockSpec(w13.shape, lambda i: (0, 0)), pl.BlockSpec(w2.shape, lambda i: (0, 0)),
            row(D), vec, vec],
        out_specs=out_specs,
        out_shape=out_shape,
        scratch_shapes=[pltpu.VMEM((2, n_rows * ROW_PITCH, BLK), F32), pltpu.SemaphoreType.DMA((2,))],
        compiler_params=_cparams(("arbitrary",)),
        name="combine_ln",
    )(pos_blocks, pos_blocks, y, gate, w13, w2, h, g.reshape(1, D), b.reshape(1, D))


def _wprep_kernel(w_ref, o_ref):
    o_ref[:, 0:C_SBQ] = w_ref[0, :, 0:C_SBQ].astype(BF16)
    o_ref[:, C_SBQ:N_MAIN] = w_ref[0, :, O_SB:O_SB + N_MAIN - C_SBQ].astype(BF16)


def _prep_in_proj(w_in_all, layer):
    _, D, n_in = w_in_all.shape
    tr = 256
    return pl.pallas_call(
        _wprep_kernel,
        grid=(D // tr,),
        in_specs=[pl.BlockSpec((1, tr, n_in), lambda i: (layer, i, 0))],
        out_specs=pl.BlockSpec((tr, N_MAIN), lambda i: (i, 0)),
        out_shape=jax.ShapeDtypeStruct((D, N_MAIN), BF16),
        compiler_params=_cparams(("parallel",)),
        name="w_in_prep",
    )(w_in_all)


def _arrange_gates(w_gates, b_igate, b_fgate):
    D = w_gates.shape[0]
    wg = jnp.zeros((D, ML_GROUPS, BLK), F32)
    bias = jnp.zeros((ML_GROUPS, BLK), F32)
    w_i = w_gates[:, :ML_HEADS].reshape(D, ML_GROUPS, ML_HPS)
    w_f = w_gates[:, ML_HEADS:].reshape(D, ML_GROUPS, ML_HPS)
    wg = wg.at[:, :, :ML_HPS].set(w_i).at[:, :, ML_HPS:2 * ML_HPS].set(w_f)
    bias = bias.at[:, :ML_HPS].set(b_igate.reshape(ML_GROUPS, ML_HPS))
    bias = bias.at[:, ML_HPS:2 * ML_HPS].set(b_fgate.reshape(ML_GROUPS, ML_HPS))
    return wg.reshape(D, ML_GROUPS * BLK).astype(BF16), bias.reshape(1, ML_GROUPS * BLK)


def _dispatch_plan(idx, rank, counts, B, Lp):
    T = B * Lp
    A = T * TOP_K
    A_real = B * (Lp - PAD) * TOP_K
    nb = -(-A_real // EXPERT_TILE) + N_EXPERTS
    E = N_EXPERTS
    eids = jnp.arange(E, dtype=jnp.int32)
    padded = (counts + EXPERT_TILE - 1) // EXPERT_TILE * EXPERT_TILE
    ends = jnp.cumsum(padded)
    poff = ends - padded
    off = jnp.cumsum(counts) - counts
    idx_k = idx[:, :, :TOP_K]
    onehot = idx_k[..., None] == eids
    pos = jnp.sum(jnp.where(onehot, poff, 0), axis=-1) + rank[:, :, :TOP_K]
    real = (jnp.arange(Lp) >= PAD)[None, :, None]
    pos = jnp.where(real, pos, 0).astype(jnp.int32)
    a_bits = (A - 1).bit_length()
    assert (E + 1) << a_bits < 2 ** 31
    keys = jnp.where(real, idx_k, E).reshape(A)
    packed = lax.shift_left(keys, a_bits) + jnp.arange(A, dtype=jnp.int32)
    sorted_tok = jnp.bitwise_and(jnp.sort(packed), (1 << a_bits) - 1) // TOP_K
    n_tok = -(-A // TOK_ALIGN) * TOK_ALIGN + TOK_WINDOW
    sorted_tok = jnp.concatenate([sorted_tok, jnp.full((n_tok - A,), PAD, jnp.int32)])
    blk0 = jnp.arange(nb, dtype=jnp.int32) * EXPERT_TILE
    block_e = jnp.minimum(jnp.sum(ends[None, :] <= blk0[:, None], axis=1), E - 1).astype(jnp.int32)
    sel = block_e[:, None] == eids
    block_start = jnp.sum(jnp.where(sel, off - poff, 0), axis=1) + blk0
    block_start = jnp.clip(block_start, 0, A).astype(jnp.int32)
    n_used = (ends[-1] // EXPERT_TILE).astype(jnp.int32).reshape(1)
    changed = jnp.concatenate([jnp.zeros((1,), jnp.int32),
                               (block_e[1:] != block_e[:-1]).astype(jnp.int32)])
    block_ord = jnp.cumsum(changed).astype(jnp.int32)
    block_next = (jnp.sum(jnp.where(sel, ends, 0), axis=1) // EXPERT_TILE).astype(jnp.int32)
    return pos.reshape(T, TOP_K), sorted_tok, (block_e, block_start, block_ord, block_next, n_used)


def kernel(x, meta, ln_in_g, ln_in_b, w_in, b_igate, b_fgate, conv_qk, ml_norm_g, sb_norm_g, pool_w, pool_scale, w_out, ln1_g, ln1_b, router_w, router_bias, exp_w13, exp_w2, sh_w13, sh_w2, ln2_g, ln2_b):
    B, S, D = x.shape
    depth = w_in.shape[0]
    alpha = (2 * depth) ** 0.25
    Lp = PAD + N_META + S
    T = B * Lp
    assert Lp % BLK == 0 and T % ROW_TILE == 0 and T % COMBINE_TILE == 0
    assert D == XROWS * BLK and COMBINE_TILE == PAD + N_META

    h0 = jnp.concatenate([jnp.zeros((B, PAD, D), x.dtype),
                          jnp.broadcast_to(meta[None].astype(x.dtype), (B, N_META, D)), x], axis=1)
    h, hb = _ln_rows(h0.reshape(T, D), ln_in_g, ln_in_b)

    for l in range(depth):
        w_gate, gate_bias = _arrange_gates(w_in[l, :, C_SBQ:O_SB], b_igate[l], b_fgate[l])
        proj3 = _in_proj(hb, _prep_in_proj(w_in, l)).reshape(B, Lp, N_MAIN)
        gcol = _gates(hb.reshape(B, Lp, D), w_gate, gate_bias)
        grow = jnp.transpose(gcol.reshape(B, Lp, ML_GROUPS, BLK)[..., :8], (0, 2, 3, 1))
        qk3 = _conv_silu(proj3, conv_qk[l])
        a_ml = _mlstm(qk3, proj3, gcol, grow, ml_norm_g[l])
        a_sb = _stick_breaking(proj3, sb_norm_g[l])
        a_pool = _pool(proj3, pool_w[l], pool_scale[l])
        h1, h1_tiles = _outproj_ln(a_ml.reshape(T, D_ML), a_sb.reshape(T, D_SB), a_pool.reshape(T, D_POOL),
                                   w_out[l].astype(BF16), h, ln1_g[l], ln1_b[l], alpha)

        idx, gate, rank, counts = _router(h1.reshape(B, Lp, D), router_w[l], router_bias[l])
        pos, sorted_tok, tables = _dispatch_plan(idx, rank, counts[0], B, Lp)
        y = _routed_experts(h1_tiles, sorted_tok, *tables, exp_w13, exp_w2, l)
        shared = (sh_w13[l].astype(BF16), sh_w2[l].astype(BF16))
        n_steps = T // COMBINE_TILE
        pos_blocks = jnp.transpose(pos.reshape(n_steps, COMBINE_TILE, TOP_K), (0, 2, 1))
        pos_blocks = pos_blocks.reshape(n_steps, 1, TOP_K * COMBINE_TILE)
        args = (pos_blocks, y, gate.reshape(T, BLK), *shared, h1, ln2_g[l], ln2_b[l], alpha)
        if l + 1 < depth:
            h, hb = _combine_ln(*args)
        else:
            (out,) = _combine_ln(*args, seq_blocks=Lp // COMBINE_TILE)

    return out.reshape(B, S, D)
```
